```python
import jax, jax.numpy as jnp
from jax import lax
import numpy as np

D_MODEL = 2048
BATCH = 2
SEQ = 4096
DEPTH = 1
DEC_BATCH = 2
DEC_SEQ = 16384
PAST_LEN = 128

POOL_GROUPS = 4
POOL_WINDOWS = (2, 4, 8, 16)
POOL_WIDTH = D_MODEL // 2
POOL_GC = POOL_WIDTH // POOL_GROUPS
HEAD_DIM = 128
N_HEADS = D_MODEL // HEAD_DIM
N_KV_HEADS = N_HEADS // 4
Q_PER_KV = N_HEADS // N_KV_HEADS
ATTN_WIDTH = N_HEADS * HEAD_DIM
KV_WIDTH = N_KV_HEADS * HEAD_DIM
WINDOW = 128
BLOCK = 128
ROT_DIM = HEAD_DIM // 4
ROPE_THETA = 500000.0
N_BRANCHES = 2
IN_WIDTH = POOL_WIDTH + ATTN_WIDTH + 2 * KV_WIDTH + N_BRANCHES * D_MODEL
N_GROUPS = 4
EXPERTS_PER_GROUP = 8
N_EXPERTS = N_GROUPS * EXPERTS_PER_GROUP
TOP_K = 2
D_FF = D_MODEL // 2
MOE_BLOCK = 128
EPS = 1e-6

kernel_name = "hybrid_pool_bandattn_hmoe_encoder"


def rms_norm(x, g):
    xf = x.astype(jnp.float32)
    y = xf * lax.rsqrt(jnp.mean(xf * xf, axis=-1, keepdims=True) + EPS)
    return (y * g.astype(jnp.float32)).astype(x.dtype)


def multiscale_pool(xp, w_grp, scale):
    B, S, _ = xp.shape
    xg = xp.reshape(B, S, POOL_GROUPS, POOL_GC).astype(jnp.float32)
    cs = jnp.concatenate([jnp.zeros((B, 1, POOL_GROUPS, POOL_GC), jnp.float32),
                          jnp.cumsum(xg, axis=1)], axis=1)
    half = jnp.array([w // 2 for w in POOL_WINDOWS], jnp.int32)
    t = jnp.arange(S, dtype=jnp.int32)[:, None]
    lo = jnp.clip(t - half[None, :], 0, S)
    hi = jnp.clip(t + half[None, :], 0, S)
    g = jnp.arange(POOL_GROUPS, dtype=jnp.int32)[None, :]
    win_sum = cs[:, hi, g, :] - cs[:, lo, g, :]
    cnt = (hi - lo).astype(jnp.float32)[None, :, :, None]
    mixed = (win_sum / cnt - xg).astype(xp.dtype)
    y = jnp.einsum('bsgc,gcd->bsgd', mixed, w_grp)
    return y.reshape(B, S, POOL_WIDTH) * scale


def partial_rotary(x, pos):
    half = ROT_DIM // 2
    inv = ROPE_THETA ** (-jnp.arange(half, dtype=jnp.float32) / half)
    ang = pos.astype(jnp.float32)[:, None] * inv[None, :]
    cos = jnp.cos(ang)[None, :, None, :]
    sin = jnp.sin(ang)[None, :, None, :]
    xf = x.astype(jnp.float32)
    x1 = xf[..., :half]
    x2 = xf[..., half:ROT_DIM]
    out = jnp.concatenate([x1 * cos - x2 * sin, x2 * cos + x1 * sin, xf[..., ROT_DIM:]], axis=-1)
    return out.astype(x.dtype)


def banded_gqa_attention(q, k, v, q_g, k_g, sink):
    B, S = q.shape[0], q.shape[1]
    nb = S // BLOCK
    pos = jnp.arange(S, dtype=jnp.int32)
    q = partial_rotary(rms_norm(q, q_g), pos)
    k = partial_rotary(rms_norm(k, k_g), pos)
    qb = q.reshape(B, nb, BLOCK, N_KV_HEADS, Q_PER_KV, HEAD_DIM)

    def neighbours(t):
        tp = jnp.pad(t, ((0, 0), (BLOCK, BLOCK), (0, 0), (0, 0)))
        tp = tp.reshape(B, nb + 2, BLOCK, N_KV_HEADS, HEAD_DIM)
        return jnp.concatenate([tp[:, :-2], tp[:, 1:-1], tp[:, 2:]], axis=2)

    kb = neighbours(k)
    vb = neighbours(v)
    s = jnp.einsum('bnqkgd,bnjkd->bnkgqj', qb, kb,
                   preferred_element_type=jnp.float32) * (HEAD_DIM ** -0.5)
    blk = jnp.arange(nb, dtype=jnp.int32)[:, None, None]
    qpos = blk * BLOCK + jnp.arange(BLOCK, dtype=jnp.int32)[None, :, None]
    kpos = (blk - 1) * BLOCK + jnp.arange(3 * BLOCK, dtype=jnp.int32)[None, None, :]
    valid = (jnp.abs(qpos - kpos) <= WINDOW) & (kpos >= 0) & (kpos < S)
    s = jnp.where(valid[None, :, None, None], s, -1e30)
    sink_b = sink.astype(jnp.float32).reshape(1, 1, N_KV_HEADS, Q_PER_KV, 1, 1)
    m = jnp.maximum(jnp.max(s, axis=-1, keepdims=True), sink_b)
    p = jnp.exp(s - m)
    p = p / (jnp.sum(p, axis=-1, keepdims=True) + jnp.exp(sink_b - m))
    o = jnp.einsum('bnkgqj,bnjkd->bnqkgd', p.astype(v.dtype), vb)
    return o.reshape(B, S, ATTN_WIDTH)


def gated_mixer(xn, w_in, pool_w, pool_scale, pool_proj, q_g, k_g, sink, attn_proj, w_out):
    B, S, _ = xn.shape
    u = xn @ w_in
    c0 = POOL_WIDTH
    c1 = c0 + ATTN_WIDTH
    c2 = c1 + KV_WIDTH
    c3 = c2 + KV_WIDTH
    xp, q, k, v, gl = jnp.split(u, [c0, c1, c2, c3], axis=-1)
    pool_out = multiscale_pool(xp, pool_w, pool_scale) @ pool_proj
    attn_out = banded_gqa_attention(q.reshape(B, S, N_HEADS, HEAD_DIM),
                                    k.reshape(B, S, N_KV_HEADS, HEAD_DIM),
                                    v.reshape(B, S, N_KV_HEADS, HEAD_DIM),
                                    q_g, k_g, sink) @ attn_proj
    g_pool, g_attn = jnp.split(jax.nn.sigmoid(gl), N_BRANCHES, axis=-1)
    return (g_pool * pool_out + g_attn * attn_out) @ w_out


def hierarchical_route(x, wg, bg, we, be):
    T = x.shape[0]
    lg = jnp.einsum('td,dg->tg', x, wg, preferred_element_type=jnp.float32) + bg.astype(jnp.float32)
    pg = jax.nn.softmax(lg, axis=-1)
    grp = jnp.argmax(lg, axis=-1).astype(jnp.int32)
    p_grp = jnp.take_along_axis(pg, grp[:, None], axis=1)
    le = jnp.einsum('td,de->te', x, we, preferred_element_type=jnp.float32) + be.astype(jnp.float32)
    le = le.reshape(T, N_GROUPS, EXPERTS_PER_GROUP)
    le_g = jnp.take_along_axis(le, grp[:, None, None], axis=1)[:, 0]
    top_l, top_i = lax.top_k(le_g, TOP_K)
    p_e = jax.nn.softmax(top_l, axis=-1)
    expert = grp[:, None] * EXPERTS_PER_GROUP + top_i.astype(jnp.int32)
    return expert, p_grp * p_e


def grouped_experts(x, expert, weight, w_gate, w_up, w_down):
    T, D = x.shape
    M = T * TOP_K
    flat_e = expert.reshape(M)
    flat_tok = jnp.arange(M, dtype=jnp.int32) // TOP_K
    flat_w = weight.reshape(M)
    order = jnp.argsort(flat_e)
    se = flat_e[order]
    counts = jnp.bincount(flat_e, length=N_EXPERTS).astype(jnp.int32)
    padded = (counts + MOE_BLOCK - 1) // MOE_BLOCK * MOE_BLOCK
    pad_end = jnp.cumsum(padded)
    pad_start = pad_end - padded
    start = jnp.cumsum(counts) - counts
    dest = pad_start[se] + jnp.arange(M, dtype=jnp.int32) - start[se]
    nblk = -(-M // MOE_BLOCK) + N_EXPERTS
    P = nblk * MOE_BLOCK
    slot_tok = jnp.full((P,), T, jnp.int32).at[dest].set(flat_tok[order])
    slot_w = jnp.zeros((P,), flat_w.dtype).at[dest].set(flat_w[order])
    blk_start = jnp.arange(nblk, dtype=jnp.int32) * MOE_BLOCK
    blk_exp = jnp.minimum(jnp.searchsorted(pad_end, blk_start, side='right'), N_EXPERTS - 1).astype(jnp.int32)
    xpad = jnp.concatenate([x, jnp.zeros((1, D), x.dtype)], axis=0)

    def block_ffn(args):
        tok, e = args
        xb = xpad[tok]
        h = jax.nn.silu(xb @ w_gate[e]) * (xb @ w_up[e])
        return h @ w_down[e]

    yb = lax.map(block_ffn, (slot_tok.reshape(nblk, MOE_BLOCK), blk_exp))
    y = yb.reshape(P, D) * slot_w[:, None].astype(x.dtype)
    return jnp.zeros((T + 1, D), x.dtype).at[slot_tok].add(y)[:T]


def setup_inputs(seed: int = 0) -> dict:
    key = jax.random.key(seed)
    ks = jax.random.split(key, 24)
    f32 = jnp.float32
    nrm = lambda k, shape, s: jax.random.normal(k, shape, f32) * s
    return {
        "x_prompt": nrm(ks[0], (BATCH, SEQ, D_MODEL), 1.0),
        "x_sample": nrm(ks[1], (DEC_BATCH, DEC_SEQ, D_MODEL), 1.0),
        "norm1_g": 1.0 + nrm(ks[2], (DEPTH, D_MODEL), 0.02),
        "w_in": nrm(ks[3], (DEPTH, D_MODEL, IN_WIDTH), D_MODEL ** -0.5),
        "pool_w": nrm(ks[4], (DEPTH, POOL_GROUPS, POOL_GC, POOL_GC), POOL_GC ** -0.5),
        "pool_scale": 1.0 + nrm(ks[5], (DEPTH, POOL_WIDTH), 0.02),
        "pool_proj": nrm(ks[6], (DEPTH, POOL_WIDTH, D_MODEL), POOL_WIDTH ** -0.5),
        "q_norm_g": 1.0 + nrm(ks[7], (DEPTH, HEAD_DIM), 0.02),
        "k_norm_g": 1.0 + nrm(ks[8], (DEPTH, HEAD_DIM), 0.02),
        "sink": nrm(ks[9], (DEPTH, N_HEADS), 0.5),
        "attn_proj": nrm(ks[10], (DEPTH, ATTN_WIDTH, D_MODEL), ATTN_WIDTH ** -0.5),
        "w_out": nrm(ks[11], (DEPTH, D_MODEL, D_MODEL), D_MODEL ** -0.5),
        "norm2_g": 1.0 + nrm(ks[12], (DEPTH, D_MODEL), 0.02),
        "router_group_w": nrm(ks[13], (DEPTH, D_MODEL, N_GROUPS), D_MODEL ** -0.5),
        "router_group_b": nrm(ks[14], (DEPTH, N_GROUPS), 0.01),
        "router_expert_w": nrm(ks[15], (DEPTH, D_MODEL, N_EXPERTS), D_MODEL ** -0.5),
        "router_expert_b": nrm(ks[16], (DEPTH, N_EXPERTS), 0.01),
        "w_gate": nrm(ks[17], (DEPTH, N_EXPERTS, D_MODEL, D_FF), D_MODEL ** -0.5),
        "w_up": nrm(ks[18], (DEPTH, N_EXPERTS, D_MODEL, D_FF), D_MODEL ** -0.5),
        "w_down": nrm(ks[19], (DEPTH, N_EXPERTS, D_FF, D_MODEL), D_FF ** -0.5),
    }


def reference(x_prompt, x_sample, norm1_g, w_in, pool_w, pool_scale, pool_proj, q_norm_g,
              k_norm_g, sink, attn_proj, w_out, norm2_g, router_group_w, router_group_b,
              router_expert_w, router_expert_b, w_gate, w_up, w_down):
    def trunk(x):
        B, S, D = x.shape
        for l in range(DEPTH):
            xn = rms_norm(x, norm1_g[l])
            x = x + gated_mixer(xn, w_in[l], pool_w[l], pool_scale[l], pool_proj[l],
                                q_norm_g[l], k_norm_g[l], sink[l], attn_proj[l], w_out[l])
            hn = rms_norm(x, norm2_g[l]).reshape(B * S, D)
            expert, weight = hierarchical_route(hn, router_group_w[l], router_group_b[l],
                                                router_expert_w[l], router_expert_b[l])
            x = x + grouped_experts(hn, expert, weight, w_gate[l], w_up[l], w_down[l]).reshape(B, S, D)
        return x

    y_prompt = trunk(x_prompt)
    y_sample = trunk(x_sample)
    return (y_prompt, y_sample)
```

```python
import functools

import jax
import jax.numpy as jnp
from jax import lax
from jax.experimental import pallas as pl
from jax.experimental.pallas import tpu as pltpu

F32 = jnp.float32
BF16 = jnp.bfloat16

D_MODEL = 2048
POOL_GROUPS = 4
POOL_HALF = (1, 2, 4, 8)
POOL_WIDTH = D_MODEL // 2
POOL_GC = POOL_WIDTH // POOL_GROUPS
HEAD_DIM = 128
N_HEADS = D_MODEL // HEAD_DIM
N_KV_HEADS = N_HEADS // 4
Q_PER_KV = N_HEADS // N_KV_HEADS
ATTN_WIDTH = N_HEADS * HEAD_DIM
KV_WIDTH = N_KV_HEADS * HEAD_DIM
WINDOW = 128
BLOCK = 128
ROT_DIM = HEAD_DIM // 4
ROT_HALF = ROT_DIM // 2
ROPE_THETA = 500000.0
IN_WIDTH = POOL_WIDTH + ATTN_WIDTH + 2 * KV_WIDTH + 2 * D_MODEL
N_GROUPS = 4
EXPERTS_PER_GROUP = 8
N_EXPERTS = N_GROUPS * EXPERTS_PER_GROUP
TOP_K = 2
D_FF = D_MODEL // 2
EPS = 1e-6
NEG = -1e30

LANES = 128
POOL_HALO = 16
MIB = 1024 * 1024

COL_Q = 0
COL_K = COL_Q + ATTN_WIDTH
COL_V = COL_K + KV_WIDTH
COL_P = COL_V + KV_WIDTH
COL_GP = COL_P + POOL_WIDTH
COL_GA = COL_GP + D_MODEL
QK_WIDTH = ATTN_WIDTH + KV_WIDTH


def _params(vmem_mib, n_axes=1):
    return pltpu.CompilerParams(dimension_semantics=("arbitrary",) * n_axes,
                                vmem_limit_bytes=vmem_mib * MIB)


def _resident(shape):
    return pl.BlockSpec(shape, lambda *_: (0,) * len(shape), pipeline_mode=pl.Buffered(1))


def _in_proj_kernel(x_ref, g1_ref, w_ref, hg_ref, cos_ref, sin_ref, o_ref, xn_ref, *, n_qk_blocks):
    j = pl.program_id(1)

    @pl.when(j == 0)
    def _():
        x = x_ref[...]
        ms = jnp.mean(x * x, axis=-1, keepdims=True)
        xn_ref[...] = (x * lax.rsqrt(ms + EPS) * g1_ref[...]).astype(BF16)

    acc = jnp.dot(xn_ref[...], w_ref[...], preferred_element_type=F32)
    tn = acc.shape[1]

    @pl.when(j < n_qk_blocks)
    def _():
        cos = cos_ref[...]
        sin = sin_ref[...]
        lane = lax.broadcasted_iota(jnp.int32, (1, HEAD_DIM), 1)
        for h in range(tn // HEAD_DIM):
            cols = slice(h * HEAD_DIM, (h + 1) * HEAD_DIM)
            a = acc[:, cols]
            ms = jnp.mean(a * a, axis=-1, keepdims=True)
            y = a * lax.rsqrt(ms + EPS) * hg_ref[:, cols]
            partner = jnp.where(lane < ROT_HALF,
                                pltpu.roll(y, HEAD_DIM - ROT_HALF, 1),
                                pltpu.roll(y, ROT_HALF, 1))
            o_ref[:, cols] = (y * cos + partner * sin).astype(BF16)

    @pl.when(j >= n_qk_blocks)
    def _():
        o_ref[...] = acc.astype(BF16)


def _in_proj(x2d, g1, w_perm, head_gain, cos_t, sin_t, seq):
    T = x2d.shape[0]
    tm = min(1024, seq)
    tn = 512
    n_pos_tiles = seq // tm
    kern = functools.partial(_in_proj_kernel, n_qk_blocks=QK_WIDTH // tn)
    return pl.pallas_call(
        kern,
        grid=(T // tm, IN_WIDTH // tn),
        in_specs=[
            pl.BlockSpec((tm, D_MODEL), lambda i, j: (i, 0)),
            pl.BlockSpec((1, D_MODEL), lambda i, j: (0, 0)),
            pl.BlockSpec((D_MODEL, tn), lambda i, j: (0, j)),
            pl.BlockSpec((1, tn), lambda i, j: (0, jnp.minimum(j, QK_WIDTH // tn - 1))),
            pl.BlockSpec((tm, HEAD_DIM), lambda i, j: (i % n_pos_tiles, 0)),
            pl.BlockSpec((tm, HEAD_DIM), lambda i, j: (i % n_pos_tiles, 0)),
        ],
        out_specs=pl.BlockSpec((tm, tn), lambda i, j: (i, j)),
        out_shape=jax.ShapeDtypeStruct((T, IN_WIDTH), BF16),
        scratch_shapes=[pltpu.VMEM((tm, D_MODEL), BF16)],
        compiler_params=_params(48, 2),
        name="in_proj",
    )(x2d, g1, w_perm, head_gain, cos_t, sin_t)


def _band_attn_kernel(sink_ref, q_ref, kp_ref, km_ref, kn_ref, vp_ref, vm_ref, vn_ref, o_ref,
                      kbuf, vbuf, *, seq, tq):
    t = pl.program_id(0)
    pos0 = (t * tq) % seq
    kbuf[0:BLOCK, :] = kp_ref[...]
    kbuf[BLOCK:BLOCK + tq, :] = km_ref[...]
    kbuf[BLOCK + tq:, :] = kn_ref[...]
    vbuf[0:BLOCK, :] = vp_ref[...]
    vbuf[BLOCK:BLOCK + tq, :] = vm_ref[...]
    vbuf[BLOCK + tq:, :] = vn_ref[...]

    scale = HEAD_DIM ** -0.5
    row = lax.broadcasted_iota(jnp.int32, (BLOCK, 3 * BLOCK), 0)
    col = lax.broadcasted_iota(jnp.int32, (BLOCK, 3 * BLOCK), 1)
    gid = lax.broadcasted_iota(jnp.int32, (Q_PER_KV, 1, 1), 0)

    def q_block(qb, carry):
        r0 = pl.multiple_of(qb * BLOCK, BLOCK)
        base = pos0 + r0 - BLOCK
        lo = jnp.maximum(row, -base)
        hi = jnp.minimum(row + 2 * WINDOW, seq - 1 - base)
        valid = ((col >= lo) & (col <= hi))[None]
        for kv in range(N_KV_HEADS):
            kc = slice(kv * HEAD_DIM, (kv + 1) * HEAD_DIM)
            q4 = jnp.concatenate(
                [q_ref[pl.ds(r0, BLOCK), (kv * Q_PER_KV + g) * HEAD_DIM:(kv * Q_PER_KV + g + 1) * HEAD_DIM]
                 for g in range(Q_PER_KV)], axis=0)
            kcat = kbuf[pl.ds(r0, 3 * BLOCK), kc]
            vcat = vbuf[pl.ds(r0, 3 * BLOCK), kc]
            s = lax.dot_general(q4, kcat, (((1,), (1,)), ((), ())), preferred_element_type=F32)
            s = (s * scale).reshape(Q_PER_KV, BLOCK, 3 * BLOCK)
            s = jnp.where(valid, s, NEG)
            sink = jnp.zeros((Q_PER_KV, 1, 1), F32)
            for g in range(Q_PER_KV):
                sink = jnp.where(gid == g, sink_ref[0, kv * Q_PER_KV + g], sink)
            m = jnp.maximum(jnp.max(s, axis=-1, keepdims=True), sink)
            p = jnp.exp(s - m)
            p = p / (jnp.sum(p, axis=-1, keepdims=True) + jnp.exp(sink - m))
            p2 = p.astype(BF16).reshape(Q_PER_KV * BLOCK, 3 * BLOCK)
            o = jnp.dot(p2, vcat, preferred_element_type=F32)
            for g in range(Q_PER_KV):
                hc = (kv * Q_PER_KV + g) * HEAD_DIM
                o_ref[pl.ds(r0, BLOCK), hc:hc + HEAD_DIM] = o[g * BLOCK:(g + 1) * BLOCK].astype(BF16)
        return carry

    lax.fori_loop(0, tq // BLOCK, q_block, 0)


def _band_attn(u, sink, seq):
    T = u.shape[0]
    tq = min(512, seq)
    nb = tq // BLOCK
    last_blk = T // BLOCK - 1
    kcol = COL_K // KV_WIDTH
    vcol = COL_V // KV_WIDTH
    prev_map = lambda c: (lambda t: (jnp.maximum(t * nb - 1, 0), c))
    next_map = lambda c: (lambda t: (jnp.minimum((t + 1) * nb, last_blk), c))
    main_map = lambda c: (lambda t: (t, c))
    kern = functools.partial(_band_attn_kernel, seq=seq, tq=tq)
    return pl.pallas_call(
        kern,
        grid=(T // tq,),
        in_specs=[
            pl.BlockSpec(memory_space=pltpu.SMEM),
            pl.BlockSpec((tq, ATTN_WIDTH), lambda t: (t, COL_Q // ATTN_WIDTH)),
            pl.BlockSpec((BLOCK, KV_WIDTH), prev_map(kcol)),
            pl.BlockSpec((tq, KV_WIDTH), main_map(kcol)),
            pl.BlockSpec((BLOCK, KV_WIDTH), next_map(kcol)),
            pl.BlockSpec((BLOCK, KV_WIDTH), prev_map(vcol)),
            pl.BlockSpec((tq, KV_WIDTH), main_map(vcol)),
            pl.BlockSpec((BLOCK, KV_WIDTH), next_map(vcol)),
        ],
        out_specs=pl.BlockSpec((tq, ATTN_WIDTH), lambda t: (t, 0)),
        out_shape=jax.ShapeDtypeStruct((T, ATTN_WIDTH), BF16),
        scratch_shapes=[pltpu.VMEM((tq + 2 * BLOCK, KV_WIDTH), BF16),
                        pltpu.VMEM((tq + 2 * BLOCK, KV_WIDTH), BF16)],
        compiler_params=_params(40),
        name="band_attn",
    )(sink, u, u, u, u, u, u, u)


def _mix_out_kernel(x_ref, xpp_ref, xpm_ref, xpn_ref, at_ref, gp_ref, ga_ref,
                    pw_ref, ps_ref, pp_ref, ap_ref, wo_ref, g2_ref, rwh_ref, rwl_ref, rb_ref,
                    h_ref, hn_ref, ids_ref, wts_ref, xb, pin, *, seq, tm):
    t = pl.program_id(0)
    pos0 = (t * tm) % seq

    xb[0:POOL_HALO, :] = jnp.where(pos0 > 0, xpp_ref[...].astype(F32), 0.0)
    xb[POOL_HALO:POOL_HALO + tm, :] = xpm_ref[...].astype(F32)
    xb[POOL_HALO + tm:, :] = jnp.where(pos0 + tm < seq, xpn_ref[...].astype(F32), 0.0)
    tpos = pos0 + lax.broadcasted_iota(jnp.int32, (tm, 1), 0)
    for g, half in enumerate(POOL_HALF):
        cols = slice(g * POOL_GC, (g + 1) * POOL_GC)
        win = xb[POOL_HALO - half:POOL_HALO - half + tm, cols]
        for d in range(-half + 1, half):
            win = win + xb[POOL_HALO + d:POOL_HALO + d + tm, cols]
        cnt = (jnp.minimum(tpos + half, seq) - jnp.maximum(tpos - half, 0)).astype(F32)
        mixed = (win / cnt - xb[POOL_HALO:POOL_HALO + tm, cols]).astype(BF16)
        yg = jnp.dot(mixed, pw_ref[g], preferred_element_type=F32)
        pin[:, cols] = (yg * ps_ref[:, cols]).astype(BF16)
    pool_out = jnp.dot(pin[...], pp_ref[...], preferred_element_type=F32)

    attn_out = jnp.dot(at_ref[...], ap_ref[...], preferred_element_type=F32)
    merged = (jax.nn.sigmoid(gp_ref[...].astype(F32)) * pool_out
              + jax.nn.sigmoid(ga_ref[...].astype(F32)) * attn_out)
    h = x_ref[...] + jnp.dot(merged.astype(BF16), wo_ref[...], preferred_element_type=F32)
    h_ref[...] = h

    ms = jnp.mean(h * h, axis=-1, keepdims=True)
    hn = h * lax.rsqrt(ms + EPS) * g2_ref[...]
    hn_ref[...] = hn

    hi = hn.astype(BF16)
    lo = (hn - hi.astype(F32)).astype(BF16)
    logit = (jnp.dot(hi, rwh_ref[...], preferred_element_type=F32)
             + jnp.dot(hi, rwl_ref[...], preferred_element_type=F32)
             + jnp.dot(lo, rwh_ref[...], preferred_element_type=F32)) + rb_ref[...]

    lane_i = lax.broadcasted_iota(jnp.int32, logit.shape, 1)
    lane = lane_i.astype(F32)
    big = float(LANES)
    is_grp = (lane_i >= N_EXPERTS) & (lane_i < N_EXPERTS + N_GROUPS)
    mg = jnp.max(jnp.where(is_grp, logit, -jnp.inf), axis=-1, keepdims=True)
    grp = jnp.min(jnp.where(is_grp & (logit == mg), lane - N_EXPERTS, big), axis=-1, keepdims=True)
    p_grp = 1.0 / jnp.sum(jnp.where(is_grp, jnp.exp(logit - mg), 0.0), axis=-1, keepdims=True)

    grp_of_lane = (lane_i // EXPERTS_PER_GROUP).astype(F32)
    in_grp = (lane_i < N_EXPERTS) & (grp_of_lane == grp)
    m1 = jnp.max(jnp.where(in_grp, logit, -jnp.inf), axis=-1, keepdims=True)
    i1 = jnp.min(jnp.where(in_grp & (logit == m1), lane, big), axis=-1, keepdims=True)
    rest = in_grp & (lane != i1)
    m2 = jnp.max(jnp.where(rest, logit, -jnp.inf), axis=-1, keepdims=True)
    i2 = jnp.min(jnp.where(rest & (logit == m2), lane, big), axis=-1, keepdims=True)
    e2 = jnp.exp(m2 - m1)
    den = 1.0 + e2
    ids_ref[...] = jnp.where(lane_i == 0, i1, jnp.where(lane_i == 1, i2, 0.0)).astype(jnp.int32)
    wts_ref[...] = jnp.where(lane_i == 0, p_grp * (1.0 / den), jnp.where(lane_i == 1, p_grp * (e2 / den), 0.0))


def _mix_out(x2d, u, attn, pool_w, pool_scale, pool_proj, attn_proj, w_out, g2, rw_hi, rw_lo, rb, seq):
    T = x2d.shape[0]
    tm = min(256, seq)
    nh = tm // POOL_HALO
    last_halo = T // POOL_HALO - 1
    pcol = COL_P // POOL_WIDTH
    kern = functools.partial(_mix_out_kernel, seq=seq, tm=tm)
    row_spec = lambda w: pl.BlockSpec((tm, w), lambda t: (t, 0))
    return pl.pallas_call(
        kern,
        grid=(T // tm,),
        in_specs=[
            row_spec(D_MODEL),
            pl.BlockSpec((POOL_HALO, POOL_WIDTH), lambda t: (jnp.maximum(t * nh - 1, 0), pcol)),
            pl.BlockSpec((tm, POOL_WIDTH), lambda t: (t, pcol)),
            pl.BlockSpec((POOL_HALO, POOL_WIDTH), lambda t: (jnp.minimum((t + 1) * nh, last_halo), pcol)),
            row_spec(ATTN_WIDTH),
            pl.BlockSpec((tm, D_MODEL), lambda t: (t, COL_GP // D_MODEL)),
            pl.BlockSpec((tm, D_MODEL), lambda t: (t, COL_GA // D_MODEL)),
            _resident((POOL_GROUPS, POOL_GC, POOL_GC)),
            _resident((1, POOL_WIDTH)),
            _resident((POOL_WIDTH, D_MODEL)),
            _resident((ATTN_WIDTH, D_MODEL)),
            _resident((D_MODEL, D_MODEL)),
            _resident((1, D_MODEL)),
            _resident((D_MODEL, LANES)),
            _resident((D_MODEL, LANES)),
            _resident((1, LANES)),
        ],
        out_specs=[row_spec(D_MODEL), row_spec(D_MODEL), row_spec(LANES), row_spec(LANES)],
        out_shape=[jax.ShapeDtypeStruct((T, D_MODEL), F32),
                   jax.ShapeDtypeStruct((T, D_MODEL), F32),
                   jax.ShapeDtypeStruct((T, LANES), jnp.int32),
                   jax.ShapeDtypeStruct((T, LANES), F32)],
        scratch_shapes=[pltpu.VMEM((tm + 2 * POOL_HALO, POOL_WIDTH), F32),
                        pltpu.VMEM((tm, POOL_WIDTH), BF16)],
        compiler_params=_params(56),
        name="mix_out",
    )(x2d, u, u, u, attn, u, u, pool_w, pool_scale, pool_proj, attn_proj, w_out, g2, rw_hi, rw_lo, rb)


def _row_copy(src_ref, src_row, dst_ref, dst_row, sem):
    return pltpu.make_async_copy(src_ref.at[pl.ds(src_row, 1)], dst_ref.at[pl.ds(dst_row, 1)], sem)


def _scatter_kernel(dest_ref, hn_ref, xs_in_ref, xs_ref, sem, *, tk):
    del xs_in_ref

    def start(r, c):
        for k in range(TOP_K):
            _row_copy(hn_ref, r, xs_ref, dest_ref[0, 0, TOP_K * r + k], sem).start()
        return c

    def wait(r, c):
        for k in range(TOP_K):
            _row_copy(hn_ref, 0, xs_ref, 0, sem).wait()
        return c

    lax.fori_loop(0, tk, start, 0)
    lax.fori_loop(0, tk, wait, 0)


def _scatter_rows(hn, dest3, n_slots, tk):
    T = hn.shape[0]
    zeros = jnp.zeros((n_slots, D_MODEL), F32)
    return pl.pallas_call(
        functools.partial(_scatter_kernel, tk=tk),
        grid=(T // tk,),
        in_specs=[
            pl.BlockSpec((1, 1, TOP_K * tk), lambda t: (t, 0, 0), memory_space=pltpu.SMEM),
            pl.BlockSpec((tk, D_MODEL), lambda t: (t, 0)),
            pl.BlockSpec(memory_space=pl.ANY),
        ],
        out_specs=pl.BlockSpec(memory_space=pl.ANY),
        out_shape=jax.ShapeDtypeStruct((n_slots, D_MODEL), F32),
        scratch_shapes=[pltpu.SemaphoreType.DMA(())],
        input_output_aliases={2: 0},
        compiler_params=_params(32),
        name="scatter_rows",
    )(dest3, hn, zeros)


def _combine_kernel(dest_ref, h_ref, wts_ref, os_ref, y_ref, gbuf, sem, *, tk):
    def start(r, c):
        for k in range(TOP_K):
            _row_copy(os_ref, dest_ref[0, 0, TOP_K * r + k], gbuf.at[k], r, sem).start()
        return c

    def wait(r, c):
        for k in range(TOP_K):
            _row_copy(os_ref, 0, gbuf.at[k], 0, sem).wait()
        return c

    lax.fori_loop(0, tk, start, 0)
    lax.fori_loop(0, tk, wait, 0)
    w = wts_ref[...]
    y_ref[...] = h_ref[...] + (w[:, 0:1] * gbuf[0] + w[:, 1:2] * gbuf[1])


def _combine(h, wts, dest3, o_sorted, tk):
    T = h.shape[0]
    return pl.pallas_call(
        functools.partial(_combine_kernel, tk=tk),
        grid=(T // tk,),
        in_specs=[
            pl.BlockSpec((1, 1, TOP_K * tk), lambda t: (t, 0, 0), memory_space=pltpu.SMEM),
            pl.BlockSpec((tk, D_MODEL), lambda t: (t, 0)),
            pl.BlockSpec((tk, LANES), lambda t: (t, 0)),
            pl.BlockSpec(memory_space=pl.ANY),
        ],
        out_specs=pl.BlockSpec((tk, D_MODEL), lambda t: (t, 0)),
        out_shape=jax.ShapeDtypeStruct((T, D_MODEL), F32),
        scratch_shapes=[pltpu.VMEM((TOP_K, tk, D_MODEL), F32), pltpu.SemaphoreType.DMA(())],
        compiler_params=_params(32),
        name="combine",
    )(dest3, h, wts, o_sorted)


def _moe_ffn_kernel(be_ref, nu_ref, x_ref, wg_ref, wu_ref, wd_ref, o_ref):
    del be_ref

    used = pl.program_id(0) < nu_ref[0]

    @pl.when(used)
    def _():
        x = x_ref[...].astype(BF16)
        gate = jnp.dot(x, wg_ref[0], preferred_element_type=F32)
        up = jnp.dot(x, wu_ref[0], preferred_element_type=F32)
        mid = (jax.nn.silu(gate) * up).astype(BF16)
        o_ref[...] = jnp.dot(mid, wd_ref[0], preferred_element_type=F32)

    @pl.when(jnp.logical_not(used))
    def _():
        o_ref[...] = jnp.zeros_like(o_ref)


def _moe_ffn(x_sorted, blk_exp, n_used, w_gate, w_up, w_down, tb):
    n_slots = x_sorted.shape[0]
    nblk = n_slots // tb
    row_map = lambda i, be, nu: (jnp.minimum(i, nu[0] - 1), 0)
    exp_map = lambda i, be, nu: (be[i], 0, 0)
    grid_spec = pltpu.PrefetchScalarGridSpec(
        num_scalar_prefetch=2,
        grid=(nblk,),
        in_specs=[
            pl.BlockSpec((tb, D_MODEL), row_map),
            pl.BlockSpec((1, D_MODEL, D_FF), exp_map),
            pl.BlockSpec((1, D_MODEL, D_FF), exp_map),
            pl.BlockSpec((1, D_FF, D_MODEL), exp_map),
        ],
        out_specs=pl.BlockSpec((tb, D_MODEL), lambda i, be, nu: (i, 0)),
    )
    return pl.pallas_call(
        _moe_ffn_kernel,
        grid_spec=grid_spec,
        out_shape=jax.ShapeDtypeStruct((n_slots, D_MODEL), F32),
        compiler_params=_params(48),
        name="moe_ffn",
    )(blk_exp, n_used, x_sorted, w_gate, w_up, w_down)


def _slot_plan(ids, tb):
    T = ids.shape[0]
    M = T * TOP_K
    e = ids[:, :TOP_K].reshape(M)
    onehot = (e[:, None] == jnp.arange(N_EXPERTS, dtype=jnp.int32)[None, :]).astype(jnp.int32)
    csum = jnp.cumsum(onehot, axis=0)
    rank = jnp.take_along_axis(csum, e[:, None], axis=1)[:, 0] - 1
    counts = csum[-1]
    padded = (counts + tb - 1) // tb * tb
    pad_end = jnp.cumsum(padded)
    pad_start = pad_end - padded
    dest = pad_start[e] + rank
    nblk = M // tb + N_EXPERTS
    blk_start = jnp.arange(nblk, dtype=jnp.int32) * tb
    blk_exp = jnp.minimum(jnp.searchsorted(pad_end, blk_start, side="right"), N_EXPERTS - 1).astype(jnp.int32)
    n_used = (pad_end[-1] // tb).astype(jnp.int32)
    blk_exp = jnp.where(blk_start < pad_end[-1], blk_exp, blk_exp[jnp.maximum(n_used - 1, 0)])
    return dest.astype(jnp.int32), blk_exp, n_used.reshape(1), nblk * tb


def _rope_tables(seq):
    inv = ROPE_THETA ** (-jnp.arange(ROT_HALF, dtype=F32) / ROT_HALF)
    ang = jnp.arange(seq, dtype=jnp.int32).astype(F32)[:, None] * inv[None, :]
    cos = jnp.cos(ang)
    sin = jnp.sin(ang)
    ones = jnp.ones((seq, HEAD_DIM - ROT_DIM), F32)
    cos_t = jnp.concatenate([cos, cos, ones], axis=1)
    sin_t = jnp.concatenate([-sin, sin, 0.0 * ones], axis=1)
    return cos_t, sin_t


def _trunk(x, prm):
    B, S, D = x.shape
    T = B * S
    x2d = x.reshape(T, D)
    cos_t, sin_t = _rope_tables(S)
    u = _in_proj(x2d, prm["g1"], prm["w_in"], prm["head_gain"], cos_t, sin_t, S)
    attn = _band_attn(u, prm["sink"], S)
    h, hn, ids, wts = _mix_out(x2d, u, attn, prm["pool_w"], prm["pool_scale"], prm["pool_proj"],
                               prm["attn_proj"], prm["w_out"], prm["g2"], prm["rw_hi"], prm["rw_lo"],
                               prm["rb"], S)
    tb = 256
    tk = min(256, S)
    dest, blk_exp, n_used, n_slots = _slot_plan(ids, tb)
    dest3 = dest.reshape(T // tk, 1, TOP_K * tk)
    x_sorted = _scatter_rows(hn, dest3, n_slots, tk)
    o_sorted = _moe_ffn(x_sorted, blk_exp, n_used, prm["w_gate"], prm["w_up"], prm["w_down"], tb)
    y = _combine(h, wts, dest3, o_sorted, tk)
    return y.reshape(B, S, D)


def kernel(x_prompt, x_sample, norm1_g, w_in, pool_w, pool_scale, pool_proj, q_norm_g, k_norm_g, sink,
           attn_proj, w_out, norm2_g, router_group_w, router_group_b, router_expert_w, router_expert_b,
           w_gate, w_up, w_down):
    assert norm1_g.shape[0] == 1, "single-layer trunk"
    c0 = POOL_WIDTH
    c1 = c0 + ATTN_WIDTH
    c2 = c1 + KV_WIDTH
    c3 = c2 + KV_WIDTH
    wi = w_in[0]
    w_perm = jnp.concatenate([wi[:, c0:c1], wi[:, c1:c2], wi[:, c2:c3], wi[:, :c0], wi[:, c3:]],
                             axis=1).astype(BF16)
    head_gain = jnp.concatenate([jnp.tile(q_norm_g[0], N_HEADS), jnp.tile(k_norm_g[0], N_KV_HEADS)])[None, :]
    pad = LANES - N_EXPERTS - N_GROUPS
    rw = jnp.concatenate([router_expert_w[0], router_group_w[0], jnp.zeros((D_MODEL, pad), F32)], axis=1)
    rw_hi = rw.astype(BF16)
    rw_lo = (rw - rw_hi.astype(F32)).astype(BF16)
    rb = jnp.concatenate([router_expert_b[0], router_group_b[0], jnp.zeros((pad,), F32)])[None, :]
    prm = dict(
        g1=norm1_g[0][None, :], w_in=w_perm, head_gain=head_gain, sink=sink[0][None, :],
        pool_w=pool_w[0].astype(BF16), pool_scale=pool_scale[0][None, :], pool_proj=pool_proj[0].astype(BF16),
        attn_proj=attn_proj[0].astype(BF16), w_out=w_out[0].astype(BF16), g2=norm2_g[0][None, :],
        rw_hi=rw_hi, rw_lo=rw_lo, rb=rb,
        w_gate=w_gate[0].astype(BF16), w_up=w_up[0].astype(BF16), w_down=w_down[0].astype(BF16),
    )
    return (_trunk(x_prompt, prm), _trunk(x_sample, prm))
```

```python
import functools

import jax
import jax.numpy as jnp
from jax import lax
from jax.experimental import pallas as pl
from jax.experimental.pallas import tpu as pltpu

F32 = jnp.float32
BF16 = jnp.bfloat16

D_MODEL = 2048
POOL_GROUPS = 4
POOL_HALF = (1, 2, 4, 8)
POOL_WIDTH = D_MODEL // 2
POOL_GC = POOL_WIDTH // POOL_GROUPS
HEAD_DIM = 128
N_HEADS = D_MODEL // HEAD_DIM
N_KV_HEADS = N_HEADS // 4
Q_PER_KV = N_HEADS // N_KV_HEADS
ATTN_WIDTH = N_HEADS * HEAD_DIM
KV_WIDTH = N_KV_HEADS * HEAD_DIM
WINDOW = 128
BLOCK = 128
ROT_DIM = HEAD_DIM // 4
ROT_HALF = ROT_DIM // 2
ROPE_THETA = 500000.0
IN_WIDTH = POOL_WIDTH + ATTN_WIDTH + 2 * KV_WIDTH + 2 * D_MODEL
N_GROUPS = 4
EXPERTS_PER_GROUP = 8
N_EXPERTS = N_GROUPS * EXPERTS_PER_GROUP
TOP_K = 2
D_FF = D_MODEL // 2
EPS = 1e-6
NEG = -1e30

LANES = 128
POOL_HALO = 16
MOE_ROW_CHUNK = 128
MIX_ROW_CHUNK = 128
MIB = 1024 * 1024

COL_Q = 0
COL_K = COL_Q + ATTN_WIDTH
COL_V = COL_K + KV_WIDTH
COL_P = COL_V + KV_WIDTH
COL_GP = COL_P + POOL_WIDTH
COL_GA = COL_GP + D_MODEL
QK_WIDTH = ATTN_WIDTH + KV_WIDTH


def _params(vmem_mib, n_axes=1):
    return pltpu.CompilerParams(dimension_semantics=("arbitrary",) * n_axes,
                                vmem_limit_bytes=vmem_mib * MIB)


def _resident(shape):
    return pl.BlockSpec(shape, lambda *_: (0,) * len(shape), pipeline_mode=pl.Buffered(1))


def _in_proj_kernel(x_ref, g1_ref, w_ref, hg_ref, cos_ref, sin_ref, o_ref, xn_ref, *, n_qk_blocks, row_chunk):
    j = pl.program_id(1)

    @pl.when(j == 0)
    def _():
        x = x_ref[...]
        ms = jnp.mean(x * x, axis=-1, keepdims=True)
        xn_ref[...] = (x * lax.rsqrt(ms + EPS) * g1_ref[...]).astype(BF16)

    tm, tn = o_ref.shape

    def chunk_dot(c):
        rows = slice(c * row_chunk, (c + 1) * row_chunk)
        return rows, jnp.dot(xn_ref[rows, :], w_ref[...], preferred_element_type=F32)

    @pl.when(j < n_qk_blocks)
    def _():
        lane = lax.broadcasted_iota(jnp.int32, (1, HEAD_DIM), 1)
        for c in range(tm // row_chunk):
            rows, acc = chunk_dot(c)
            cos = cos_ref[rows, :]
            sin = sin_ref[rows, :]
            for h in range(tn // HEAD_DIM):
                cols = slice(h * HEAD_DIM, (h + 1) * HEAD_DIM)
                a = acc[:, cols]
                ms = jnp.mean(a * a, axis=-1, keepdims=True)
                y = a * lax.rsqrt(ms + EPS) * hg_ref[:, cols]
                partner = jnp.where(lane < ROT_HALF,
                                    pltpu.roll(y, HEAD_DIM - ROT_HALF, 1),
                                    pltpu.roll(y, ROT_HALF, 1))
                o_ref[rows, cols] = (y * cos + partner * sin).astype(BF16)

    @pl.when(j >= n_qk_blocks)
    def _():
        for c in range(tm // row_chunk):
            rows, acc = chunk_dot(c)
            o_ref[rows, :] = acc.astype(BF16)


def _in_proj(x2d, g1, w_perm, head_gain, cos_t, sin_t, seq):
    T = x2d.shape[0]
    tm = min(1024, seq)
    tn = 512
    n_pos_tiles = seq // tm
    kern = functools.partial(_in_proj_kernel, n_qk_blocks=QK_WIDTH // tn, row_chunk=min(256, tm))
    return pl.pallas_call(
        kern,
        grid=(T // tm, IN_WIDTH // tn),
        in_specs=[
            pl.BlockSpec((tm, D_MODEL), lambda i, j: (i, 0)),
            pl.BlockSpec((1, D_MODEL), lambda i, j: (0, 0)),
            pl.BlockSpec((D_MODEL, tn), lambda i, j: (0, j)),
            pl.BlockSpec((1, tn), lambda i, j: (0, jnp.minimum(j, QK_WIDTH // tn - 1))),
            pl.BlockSpec((tm, HEAD_DIM), lambda i, j: (i % n_pos_tiles, 0)),
            pl.BlockSpec((tm, HEAD_DIM), lambda i, j: (i % n_pos_tiles, 0)),
        ],
        out_specs=pl.BlockSpec((tm, tn), lambda i, j: (i, j)),
        out_shape=jax.ShapeDtypeStruct((T, IN_WIDTH), BF16),
        scratch_shapes=[pltpu.VMEM((tm, D_MODEL), BF16)],
        compiler_params=_params(48, 2),
        name="in_proj",
    )(x2d, g1, w_perm, head_gain, cos_t, sin_t)


def _band_attn_kernel(sink_ref, q_ref, kp_ref, km_ref, kn_ref, vp_ref, vm_ref, vn_ref, o_ref,
                      kbuf, vbuf, *, seq, tq):
    t = pl.program_id(0)
    pos0 = (t * tq) % seq
    kbuf[0:BLOCK, :] = kp_ref[...]
    kbuf[BLOCK:BLOCK + tq, :] = km_ref[...]
    kbuf[BLOCK + tq:, :] = kn_ref[...]
    vbuf[0:BLOCK, :] = vp_ref[...]
    vbuf[BLOCK:BLOCK + tq, :] = vm_ref[...]
    vbuf[BLOCK + tq:, :] = vn_ref[...]

    scale = HEAD_DIM ** -0.5
    row = lax.broadcasted_iota(jnp.int32, (BLOCK, 3 * BLOCK), 0)
    col = lax.broadcasted_iota(jnp.int32, (BLOCK, 3 * BLOCK), 1)
    gid = lax.broadcasted_iota(jnp.int32, (Q_PER_KV, 1, 1), 0)

    def q_block(qb, carry):
        r0 = pl.multiple_of(qb * BLOCK, BLOCK)
        base = pos0 + r0 - BLOCK
        lo = jnp.maximum(row, -base)
        hi = jnp.minimum(row + 2 * WINDOW, seq - 1 - base)
        valid = ((col >= lo) & (col <= hi))[None]
        for kv in range(N_KV_HEADS):
            kc = slice(kv * HEAD_DIM, (kv + 1) * HEAD_DIM)
            q4 = jnp.concatenate(
                [q_ref[pl.ds(r0, BLOCK), (kv * Q_PER_KV + g) * HEAD_DIM:(kv * Q_PER_KV + g + 1) * HEAD_DIM]
                 for g in range(Q_PER_KV)], axis=0)
            kcat = kbuf[pl.ds(r0, 3 * BLOCK), kc]
            vcat = vbuf[pl.ds(r0, 3 * BLOCK), kc]
            s = lax.dot_general(q4, kcat, (((1,), (1,)), ((), ())), preferred_element_type=F32)
            s = (s * scale).reshape(Q_PER_KV, BLOCK, 3 * BLOCK)
            s = jnp.where(valid, s, NEG)
            sink = jnp.zeros((Q_PER_KV, 1, 1), F32)
            for g in range(Q_PER_KV):
                sink = jnp.where(gid == g, sink_ref[0, kv * Q_PER_KV + g], sink)
            m = jnp.maximum(jnp.max(s, axis=-1, keepdims=True), sink)
            p = jnp.exp(s - m)
            p = p / (jnp.sum(p, axis=-1, keepdims=True) + jnp.exp(sink - m))
            p2 = p.astype(BF16).reshape(Q_PER_KV * BLOCK, 3 * BLOCK)
            o = jnp.dot(p2, vcat, preferred_element_type=F32)
            for g in range(Q_PER_KV):
                hc = (kv * Q_PER_KV + g) * HEAD_DIM
                o_ref[pl.ds(r0, BLOCK), hc:hc + HEAD_DIM] = o[g * BLOCK:(g + 1) * BLOCK].astype(BF16)
        return carry

    lax.fori_loop(0, tq // BLOCK, q_block, 0)


def _band_attn(u, sink, seq):
    T = u.shape[0]
    tq = min(512, seq)
    nb = tq // BLOCK
    last_blk = T // BLOCK - 1
    kcol = COL_K // KV_WIDTH
    vcol = COL_V // KV_WIDTH
    prev_map = lambda c: (lambda t: (jnp.maximum(t * nb - 1, 0), c))
    next_map = lambda c: (lambda t: (jnp.minimum((t + 1) * nb, last_blk), c))
    main_map = lambda c: (lambda t: (t, c))
    kern = functools.partial(_band_attn_kernel, seq=seq, tq=tq)
    return pl.pallas_call(
        kern,
        grid=(T // tq,),
        in_specs=[
            pl.BlockSpec(memory_space=pltpu.SMEM),
            pl.BlockSpec((tq, ATTN_WIDTH), lambda t: (t, COL_Q // ATTN_WIDTH)),
            pl.BlockSpec((BLOCK, KV_WIDTH), prev_map(kcol)),
            pl.BlockSpec((tq, KV_WIDTH), main_map(kcol)),
            pl.BlockSpec((BLOCK, KV_WIDTH), next_map(kcol)),
            pl.BlockSpec((BLOCK, KV_WIDTH), prev_map(vcol)),
            pl.BlockSpec((tq, KV_WIDTH), main_map(vcol)),
            pl.BlockSpec((BLOCK, KV_WIDTH), next_map(vcol)),
        ],
        out_specs=pl.BlockSpec((tq, ATTN_WIDTH), lambda t: (t, 0)),
        out_shape=jax.ShapeDtypeStruct((T, ATTN_WIDTH), BF16),
        scratch_shapes=[pltpu.VMEM((tq + 2 * BLOCK, KV_WIDTH), BF16),
                        pltpu.VMEM((tq + 2 * BLOCK, KV_WIDTH), BF16)],
        compiler_params=_params(40),
        name="band_attn",
    )(sink, u, u, u, u, u, u, u)


def _mix_out_kernel(x_ref, xpp_ref, xpm_ref, xpn_ref, at_ref, gp_ref, ga_ref,
                    pw_ref, ps_ref, pp_ref, ap_ref, wo_ref, g2_ref, rwh_ref, rwl_ref, rb_ref,
                    h_ref, hn_ref, ids_ref, wts_ref, xb, pin, *, seq, tm):
    t = pl.program_id(0)
    pos0 = (t * tm) % seq

    xb[0:POOL_HALO, :] = jnp.where(pos0 > 0, xpp_ref[...].astype(F32), 0.0)
    xb[POOL_HALO:POOL_HALO + tm, :] = xpm_ref[...].astype(F32)
    xb[POOL_HALO + tm:, :] = jnp.where(pos0 + tm < seq, xpn_ref[...].astype(F32), 0.0)
    for c in range(tm // MIX_ROW_CHUNK):
        _mix_out_rows(c * MIX_ROW_CHUNK, pos0, seq, x_ref, at_ref, gp_ref, ga_ref, pw_ref, ps_ref, pp_ref,
                      ap_ref, wo_ref, g2_ref, rwh_ref, rwl_ref, rb_ref, h_ref, hn_ref, ids_ref, wts_ref, xb, pin)


def _mix_out_rows(r0, pos0, seq, x_ref, at_ref, gp_ref, ga_ref, pw_ref, ps_ref, pp_ref, ap_ref, wo_ref,
                  g2_ref, rwh_ref, rwl_ref, rb_ref, h_ref, hn_ref, ids_ref, wts_ref, xb, pin):
    rc = MIX_ROW_CHUNK
    rows = slice(r0, r0 + rc)
    tpos = pos0 + r0 + lax.broadcasted_iota(jnp.int32, (rc, 1), 0)
    for g, half in enumerate(POOL_HALF):
        cols = slice(g * POOL_GC, (g + 1) * POOL_GC)
        c0 = POOL_HALO + r0
        win = xb[c0 - half:c0 - half + rc, cols]
        for d in range(-half + 1, half):
            win = win + xb[c0 + d:c0 + d + rc, cols]
        cnt = (jnp.minimum(tpos + half, seq) - jnp.maximum(tpos - half, 0)).astype(F32)
        mixed = (win / cnt - xb[c0:c0 + rc, cols]).astype(BF16)
        yg = jnp.dot(mixed, pw_ref[g], preferred_element_type=F32)
        pin[rows, cols] = (yg * ps_ref[:, cols]).astype(BF16)
    pool_out = jnp.dot(pin[rows, :], pp_ref[...], preferred_element_type=F32)

    attn_out = jnp.dot(at_ref[rows, :], ap_ref[...], preferred_element_type=F32)
    merged = (jax.nn.sigmoid(gp_ref[rows, :].astype(F32)) * pool_out
              + jax.nn.sigmoid(ga_ref[rows, :].astype(F32)) * attn_out)
    h = x_ref[rows, :] + jnp.dot(merged.astype(BF16), wo_ref[...], preferred_element_type=F32)
    h_ref[rows, :] = h

    ms = jnp.mean(h * h, axis=-1, keepdims=True)
    hn = h * lax.rsqrt(ms + EPS) * g2_ref[...]
    hn_ref[rows, :] = hn

    hi = hn.astype(BF16)
    lo = (hn - hi.astype(F32)).astype(BF16)
    logit = (jnp.dot(hi, rwh_ref[...], preferred_element_type=F32)
             + jnp.dot(hi, rwl_ref[...], preferred_element_type=F32)
             + jnp.dot(lo, rwh_ref[...], preferred_element_type=F32)) + rb_ref[...]

    lane_i = lax.broadcasted_iota(jnp.int32, logit.shape, 1)
    lane = lane_i.astype(F32)
    big = float(LANES)
    is_grp = (lane_i >= N_EXPERTS) & (lane_i < N_EXPERTS + N_GROUPS)
    mg = jnp.max(jnp.where(is_grp, logit, -jnp.inf), axis=-1, keepdims=True)
    grp = jnp.min(jnp.where(is_grp & (logit == mg), lane - N_EXPERTS, big), axis=-1, keepdims=True)
    p_grp = 1.0 / jnp.sum(jnp.where(is_grp, jnp.exp(logit - mg), 0.0), axis=-1, keepdims=True)

    grp_of_lane = (lane_i // EXPERTS_PER_GROUP).astype(F32)
    in_grp = (lane_i < N_EXPERTS) & (grp_of_lane == grp)
    m1 = jnp.max(jnp.where(in_grp, logit, -jnp.inf), axis=-1, keepdims=True)
    i1 = jnp.min(jnp.where(in_grp & (logit == m1), lane, big), axis=-1, keepdims=True)
    rest = in_grp & (lane != i1)
    m2 = jnp.max(jnp.where(rest, logit, -jnp.inf), axis=-1, keepdims=True)
    i2 = jnp.min(jnp.where(rest & (logit == m2), lane, big), axis=-1, keepdims=True)
    e2 = jnp.exp(m2 - m1)
    den = 1.0 + e2
    ids_ref[rows, :] = jnp.where(lane_i == 0, i1, jnp.where(lane_i == 1, i2, 0.0)).astype(jnp.int32)
    wts_ref[rows, :] = jnp.where(lane_i == 0, p_grp * (1.0 / den),
                                 jnp.where(lane_i == 1, p_grp * (e2 / den), 0.0))


def _mix_out(x2d, u, attn, pool_w, pool_scale, pool_proj, attn_proj, w_out, g2, rw_hi, rw_lo, rb, seq):
    T = x2d.shape[0]
    tm = min(256, seq)
    nh = tm // POOL_HALO
    last_halo = T // POOL_HALO - 1
    pcol = COL_P // POOL_WIDTH
    kern = functools.partial(_mix_out_kernel, seq=seq, tm=tm)
    row_spec = lambda w: pl.BlockSpec((tm, w), lambda t: (t, 0))
    return pl.pallas_call(
        kern,
        grid=(T // tm,),
        in_specs=[
            row_spec(D_MODEL),
            pl.BlockSpec((POOL_HALO, POOL_WIDTH), lambda t: (jnp.maximum(t * nh - 1, 0), pcol)),
            pl.BlockSpec((tm, POOL_WIDTH), lambda t: (t, pcol)),
            pl.BlockSpec((POOL_HALO, POOL_WIDTH), lambda t: (jnp.minimum((t + 1) * nh, last_halo), pcol)),
            row_spec(ATTN_WIDTH),
            pl.BlockSpec((tm, D_MODEL), lambda t: (t, COL_GP // D_MODEL)),
            pl.BlockSpec((tm, D_MODEL), lambda t: (t, COL_GA // D_MODEL)),
            _resident((POOL_GROUPS, POOL_GC, POOL_GC)),
            _resident((1, POOL_WIDTH)),
            _resident((POOL_WIDTH, D_MODEL)),
            _resident((ATTN_WIDTH, D_MODEL)),
            _resident((D_MODEL, D_MODEL)),
            _resident((1, D_MODEL)),
            _resident((D_MODEL, LANES)),
            _resident((D_MODEL, LANES)),
            _resident((1, LANES)),
        ],
        out_specs=[row_spec(D_MODEL), row_spec(D_MODEL), row_spec(LANES), row_spec(LANES)],
        out_shape=[jax.ShapeDtypeStruct((T, D_MODEL), F32),
                   jax.ShapeDtypeStruct((T, D_MODEL), F32),
                   jax.ShapeDtypeStruct((T, LANES), jnp.int32),
                   jax.ShapeDtypeStruct((T, LANES), F32)],
        scratch_shapes=[pltpu.VMEM((tm + 2 * POOL_HALO, POOL_WIDTH), F32),
                        pltpu.VMEM((tm, POOL_WIDTH), BF16)],
        compiler_params=_params(56),
        name="mix_out",
    )(x2d, u, u, u, attn, u, u, pool_w, pool_scale, pool_proj, attn_proj, w_out, g2, rw_hi, rw_lo, rb)


def _row_copy(src_ref, src_row, dst_ref, dst_row, sem):
    return pltpu.make_async_copy(src_ref.at[pl.ds(src_row, 1)], dst_ref.at[pl.ds(dst_row, 1)], sem)


def _scatter_kernel(dest_ref, hn_ref, xs_in_ref, xs_ref, sem, *, tk):
    del xs_in_ref

    def start(r, c):
        for k in range(TOP_K):
            _row_copy(hn_ref, r, xs_ref, dest_ref[0, 0, TOP_K * r + k], sem).start()
        return c

    def wait(r, c):
        for k in range(TOP_K):
            _row_copy(hn_ref, 0, xs_ref, 0, sem).wait()
        return c

    lax.fori_loop(0, tk, start, 0)
    lax.fori_loop(0, tk, wait, 0)


def _scatter_rows(hn, dest3, n_slots, tk):
    T = hn.shape[0]
    zeros = jnp.zeros((n_slots, D_MODEL), F32)
    return pl.pallas_call(
        functools.partial(_scatter_kernel, tk=tk),
        grid=(T // tk,),
        in_specs=[
            pl.BlockSpec((1, 1, TOP_K * tk), lambda t: (t, 0, 0), memory_space=pltpu.SMEM),
            pl.BlockSpec((tk, D_MODEL), lambda t: (t, 0)),
            pl.BlockSpec(memory_space=pl.ANY),
        ],
        out_specs=pl.BlockSpec(memory_space=pl.ANY),
        out_shape=jax.ShapeDtypeStruct((n_slots, D_MODEL), F32),
        scratch_shapes=[pltpu.SemaphoreType.DMA(())],
        input_output_aliases={2: 0},
        compiler_params=_params(32),
        name="scatter_rows",
    )(dest3, hn, zeros)


def _combine_kernel(dest_ref, h_ref, wts_ref, os_ref, y_ref, gbuf, sem, *, tk):
    def start(r, c):
        for k in range(TOP_K):
            _row_copy(os_ref, dest_ref[0, 0, TOP_K * r + k], gbuf.at[k], r, sem).start()
        return c

    def wait(r, c):
        for k in range(TOP_K):
            _row_copy(os_ref, 0, gbuf.at[k], 0, sem).wait()
        return c

    lax.fori_loop(0, tk, start, 0)
    lax.fori_loop(0, tk, wait, 0)
    w = wts_ref[...]
    y_ref[...] = h_ref[...] + (w[:, 0:1] * gbuf[0] + w[:, 1:2] * gbuf[1])


def _combine(h, wts, dest3, o_sorted, tk):
    T = h.shape[0]
    return pl.pallas_call(
        functools.partial(_combine_kernel, tk=tk),
        grid=(T // tk,),
        in_specs=[
            pl.BlockSpec((1, 1, TOP_K * tk), lambda t: (t, 0, 0), memory_space=pltpu.SMEM),
            pl.BlockSpec((tk, D_MODEL), lambda t: (t, 0)),
            pl.BlockSpec((tk, LANES), lambda t: (t, 0)),
            pl.BlockSpec(memory_space=pl.ANY),
        ],
        out_specs=pl.BlockSpec((tk, D_MODEL), lambda t: (t, 0)),
        out_shape=jax.ShapeDtypeStruct((T, D_MODEL), F32),
        scratch_shapes=[pltpu.VMEM((TOP_K, tk, D_MODEL), F32), pltpu.SemaphoreType.DMA(())],
        compiler_params=_params(32),
        name="combine",
    )(dest3, h, wts, o_sorted)


def _moe_ffn_kernel(be_ref, nu_ref, x_ref, wg_ref, wu_ref, wd_ref, o_ref):
    del be_ref

    used = pl.program_id(0) < nu_ref[0]

    @pl.when(used)
    def _():
        for c in range(o_ref.shape[0] // MOE_ROW_CHUNK):
            rows = slice(c * MOE_ROW_CHUNK, (c + 1) * MOE_ROW_CHUNK)
            x = x_ref[rows, :].astype(BF16)
            gate = jnp.dot(x, wg_ref[0], preferred_element_type=F32)
            up = jnp.dot(x, wu_ref[0], preferred_element_type=F32)
            mid = (jax.nn.silu(gate) * up).astype(BF16)
            o_ref[rows, :] = jnp.dot(mid, wd_ref[0], preferred_element_type=F32)

    @pl.when(jnp.logical_not(used))
    def _():
        o_ref[...] = jnp.zeros_like(o_ref)


def _moe_ffn(x_sorted, blk_exp, n_used, w_gate, w_up, w_down, tb):
    n_slots = x_sorted.shape[0]
    nblk = n_slots // tb
    row_map = lambda i, be, nu: (jnp.minimum(i, nu[0] - 1), 0)
    exp_map = lambda i, be, nu: (be[i], 0, 0)
    grid_spec = pltpu.PrefetchScalarGridSpec(
        num_scalar_prefetch=2,
        grid=(nblk,),
        in_specs=[
            pl.BlockSpec((tb, D_MODEL), row_map),
            pl.BlockSpec((1, D_MODEL, D_FF), exp_map),
            pl.BlockSpec((1, D_MODEL, D_FF), exp_map),
            pl.BlockSpec((1, D_FF, D_MODEL), exp_map),
        ],
        out_specs=pl.BlockSpec((tb, D_MODEL), lambda i, be, nu: (i, 0)),
    )
    return pl.pallas_call(
        _moe_ffn_kernel,
        grid_spec=grid_spec,
        out_shape=jax.ShapeDtypeStruct((n_slots, D_MODEL), F32),
        compiler_params=_params(48),
        name="moe_ffn",
    )(blk_exp, n_used, x_sorted, w_gate, w_up, w_down)


def _slot_plan(ids, tb):
    T = ids.shape[0]
    M = T * TOP_K
    e = ids[:, :TOP_K].reshape(M)
    onehot = (e[:, None] == jnp.arange(N_EXPERTS, dtype=jnp.int32)[None, :]).astype(jnp.int32)
    csum = jnp.cumsum(onehot, axis=0)
    rank = jnp.take_along_axis(csum, e[:, None], axis=1)[:, 0] - 1
    counts = csum[-1]
    padded = (counts + tb - 1) // tb * tb
    pad_end = jnp.cumsum(padded)
    pad_start = pad_end - padded
    dest = pad_start[e] + rank
    nblk = M // tb + N_EXPERTS
    blk_start = jnp.arange(nblk, dtype=jnp.int32) * tb
    blk_exp = jnp.minimum(jnp.searchsorted(pad_end, blk_start, side="right"), N_EXPERTS - 1).astype(jnp.int32)
    n_used = (pad_end[-1] // tb).astype(jnp.int32)
    blk_exp = jnp.where(blk_start < pad_end[-1], blk_exp, blk_exp[jnp.maximum(n_used - 1, 0)])
    return dest.astype(jnp.int32), blk_exp, n_used.reshape(1), nblk * tb


def _rope_tables(seq):
    inv = ROPE_THETA ** (-jnp.arange(ROT_HALF, dtype=F32) / ROT_HALF)
    ang = jnp.arange(seq, dtype=jnp.int32).astype(F32)[:, None] * inv[None, :]
    cos = jnp.cos(ang)
    sin = jnp.sin(ang)
    ones = jnp.ones((seq, HEAD_DIM - ROT_DIM), F32)
    cos_t = jnp.concatenate([cos, cos, ones], axis=1)
    sin_t = jnp.concatenate([-sin, sin, 0.0 * ones], axis=1)
    return cos_t, sin_t


def _trunk(x, prm):
    B, S, D = x.shape
    T = B * S
    x2d = x.reshape(T, D)
    cos_t, sin_t = _rope_tables(S)
    u = _in_proj(x2d, prm["g1"], prm["w_in"], prm["head_gain"], cos_t, sin_t, S)
    attn = _band_attn(u, prm["sink"], S)
    h, hn, ids, wts = _mix_out(x2d, u, attn, prm["pool_w"], prm["pool_scale"], prm["pool_proj"],
                               prm["attn_proj"], prm["w_out"], prm["g2"], prm["rw_hi"], prm["rw_lo"],
                               prm["rb"], S)
    tb = 256
    tk = min(256, S)
    dest, blk_exp, n_used, n_slots = _slot_plan(ids, tb)
    dest3 = dest.reshape(T // tk, 1, TOP_K * tk)
    x_sorted = _scatter_rows(hn, dest3, n_slots, tk)
    o_sorted = _moe_ffn(x_sorted, blk_exp, n_used, prm["w_gate"], prm["w_up"], prm["w_down"], tb)
    y = _combine(h, wts, dest3, o_sorted, tk)
    return y.reshape(B, S, D)


def kernel(x_prompt, x_sample, norm1_g, w_in, pool_w, pool_scale, pool_proj, q_norm_g, k_norm_g, sink,
           attn_proj, w_out, norm2_g, router_group_w, router_group_b, router_expert_w, router_expert_b,
           w_gate, w_up, w_down):
    assert norm1_g.shape[0] == 1, "single-layer trunk"
    c0 = POOL_WIDTH
    c1 = c0 + ATTN_WIDTH
    c2 = c1 + KV_WIDTH
    c3 = c2 + KV_WIDTH
    wi = w_in[0]
    w_perm = jnp.concatenate([wi[:, c0:c1], wi[:, c1:c2], wi[:, c2:c3], wi[:, :c0], wi[:, c3:]],
                             axis=1).astype(BF16)
    head_gain = jnp.concatenate([jnp.tile(q_norm_g[0], N_HEADS), jnp.tile(k_norm_g[0], N_KV_HEADS)])[None, :]
    pad = LANES - N_EXPERTS - N_GROUPS
    rw = jnp.concatenate([router_expert_w[0], router_group_w[0], jnp.zeros((D_MODEL, pad), F32)], axis=1)
    rw_hi = rw.astype(BF16)
    rw_lo = (rw - rw_hi.astype(F32)).astype(BF16)
    rb = jnp.concatenate([router_expert_b[0], router_group_b[0], jnp.zeros((pad,), F32)])[None, :]
    prm = dict(
        g1=norm1_g[0][None, :], w_in=w_perm, head_gain=head_gain, sink=sink[0][None, :],
        pool_w=pool_w[0].astype(BF16), pool_scale=pool_scale[0][None, :], pool_proj=pool_proj[0].astype(BF16),
        attn_proj=attn_proj[0].astype(BF16), w_out=w_out[0].astype(BF16), g2=norm2_g[0][None, :],
        rw_hi=rw_hi, rw_lo=rw_lo, rb=rb,
        w_gate=w_gate[0].astype(BF16), w_up=w_up[0].astype(BF16), w_down=w_down[0].astype(BF16),
    )
    return (_trunk(x_prompt, prm), _trunk(x_sample, prm))
```

```python
import functools

import jax
import jax.numpy as jnp
from jax import lax
from jax.experimental import pallas as pl
from jax.experimental.pallas import tpu as pltpu

F32 = jnp.float32
BF16 = jnp.bfloat16

D_MODEL = 2048
POOL_GROUPS = 4
POOL_HALF = (1, 2, 4, 8)
POOL_WIDTH = D_MODEL // 2
POOL_GC = POOL_WIDTH // POOL_GROUPS
HEAD_DIM = 128
N_HEADS = D_MODEL // HEAD_DIM
N_KV_HEADS = N_HEADS // 4
Q_PER_KV = N_HEADS // N_KV_HEADS
ATTN_WIDTH = N_HEADS * HEAD_DIM
KV_WIDTH = N_KV_HEADS * HEAD_DIM
WINDOW = 128
BLOCK = 128
ROT_DIM = HEAD_DIM // 4
ROT_HALF = ROT_DIM // 2
ROPE_THETA = 500000.0
IN_WIDTH = POOL_WIDTH + ATTN_WIDTH + 2 * KV_WIDTH + 2 * D_MODEL
N_GROUPS = 4
EXPERTS_PER_GROUP = 8
N_EXPERTS = N_GROUPS * EXPERTS_PER_GROUP
TOP_K = 2
D_FF = D_MODEL // 2
EPS = 1e-6
NEG = -1e30

LANES = 128
POOL_HALO = 16
MIX_ROW_CHUNK = 256
SLAB_ROWS = D_MODEL // LANES
SLAB_PITCH = SLAB_ROWS + 1
MIB = 1024 * 1024

COL_Q = 0
COL_K = COL_Q + ATTN_WIDTH
COL_V = COL_K + KV_WIDTH
COL_P = COL_V + KV_WIDTH
COL_GP = COL_P + POOL_WIDTH
COL_GA = COL_GP + D_MODEL
QK_WIDTH = ATTN_WIDTH + KV_WIDTH


def _params(vmem_mib, n_axes=1):
    return pltpu.CompilerParams(dimension_semantics=("arbitrary",) * n_axes,
                                vmem_limit_bytes=vmem_mib * MIB)


def _resident(shape):
    return pl.BlockSpec(shape, lambda *_: (0,) * len(shape), pipeline_mode=pl.Buffered(1))


def _store_slabs(ref, base_row, n, val):
    for j in range(SLAB_ROWS):
        ref[pl.ds(base_row + j, n, stride=SLAB_PITCH), :] = val[:, j * LANES:(j + 1) * LANES]
    ref[pl.ds(base_row + SLAB_ROWS, n, stride=SLAB_PITCH), :] = jnp.zeros((n, LANES), F32)


def _load_slabs(ref, base_row, n):
    return [ref[pl.ds(base_row + j, n, stride=SLAB_PITCH), :] for j in range(SLAB_ROWS)]


def _in_proj_kernel(x_ref, g1_ref, w_ref, hg_ref, cos_ref, sin_ref, o_ref, xn_ref, *, n_qk_blocks, row_chunk):
    j = pl.program_id(1)

    @pl.when(j == 0)
    def _():
        x = x_ref[...]
        ms = jnp.mean(x * x, axis=-1, keepdims=True)
        xn_ref[...] = (x * lax.rsqrt(ms + EPS) * g1_ref[...]).astype(BF16)

    tm, tn = o_ref.shape

    def chunk_dot(c):
        rows = slice(c * row_chunk, (c + 1) * row_chunk)
        return rows, jnp.dot(xn_ref[rows, :], w_ref[...], preferred_element_type=F32)

    @pl.when(j < n_qk_blocks)
    def _():
        lane = lax.broadcasted_iota(jnp.int32, (1, HEAD_DIM), 1)
        for c in range(tm // row_chunk):
            rows, acc = chunk_dot(c)
            cos = cos_ref[rows, :]
            sin = sin_ref[rows, :]
            for h in range(tn // HEAD_DIM):
                cols = slice(h * HEAD_DIM, (h + 1) * HEAD_DIM)
                a = acc[:, cols]
                ms = jnp.mean(a * a, axis=-1, keepdims=True)
                y = a * lax.rsqrt(ms + EPS) * hg_ref[:, cols]
                partner = jnp.where(lane < ROT_HALF,
                                    pltpu.roll(y, HEAD_DIM - ROT_HALF, 1),
                                    pltpu.roll(y, ROT_HALF, 1))
                o_ref[rows, cols] = (y * cos + partner * sin).astype(BF16)

    @pl.when(j >= n_qk_blocks)
    def _():
        for c in range(tm // row_chunk):
            rows, acc = chunk_dot(c)
            o_ref[rows, :] = acc.astype(BF16)


def _in_proj(x2d, g1, w_perm, head_gain, cos_t, sin_t, seq):
    T = x2d.shape[0]
    tm = min(1024, seq)
    tn = 512
    n_pos_tiles = seq // tm
    kern = functools.partial(_in_proj_kernel, n_qk_blocks=QK_WIDTH // tn, row_chunk=min(256, tm))
    return pl.pallas_call(
        kern,
        grid=(T // tm, IN_WIDTH // tn),
        in_specs=[
            pl.BlockSpec((tm, D_MODEL), lambda i, j: (i, 0)),
            pl.BlockSpec((1, D_MODEL), lambda i, j: (0, 0)),
            pl.BlockSpec((D_MODEL, tn), lambda i, j: (0, j)),
            pl.BlockSpec((1, tn), lambda i, j: (0, jnp.minimum(j, QK_WIDTH // tn - 1))),
            pl.BlockSpec((tm, HEAD_DIM), lambda i, j: (i % n_pos_tiles, 0)),
            pl.BlockSpec((tm, HEAD_DIM), lambda i, j: (i % n_pos_tiles, 0)),
        ],
        out_specs=pl.BlockSpec((tm, tn), lambda i, j: (i, j)),
        out_shape=jax.ShapeDtypeStruct((T, IN_WIDTH), BF16),
        scratch_shapes=[pltpu.VMEM((tm, D_MODEL), BF16)],
        compiler_params=_params(48, 2),
        name="in_proj",
    )(x2d, g1, w_perm, head_gain, cos_t, sin_t)


def _band_attn_kernel(sink_ref, q_ref, kp_ref, km_ref, kn_ref, vp_ref, vm_ref, vn_ref, o_ref,
                      kbuf, vbuf, *, seq, tq):
    t = pl.program_id(0)
    pos0 = (t * tq) % seq
    kbuf[0:BLOCK, :] = kp_ref[...]
    kbuf[BLOCK:BLOCK + tq, :] = km_ref[...]
    kbuf[BLOCK + tq:, :] = kn_ref[...]
    vbuf[0:BLOCK, :] = vp_ref[...]
    vbuf[BLOCK:BLOCK + tq, :] = vm_ref[...]
    vbuf[BLOCK + tq:, :] = vn_ref[...]

    scale = HEAD_DIM ** -0.5
    row = lax.broadcasted_iota(jnp.int32, (BLOCK, 3 * BLOCK), 0)
    col = lax.broadcasted_iota(jnp.int32, (BLOCK, 3 * BLOCK), 1)
    gid = lax.broadcasted_iota(jnp.int32, (Q_PER_KV, 1, 1), 0)

    def q_block(qb, carry):
        r0 = pl.multiple_of(qb * BLOCK, BLOCK)
        base = pos0 + r0 - BLOCK
        lo = jnp.maximum(row, -base)
        hi = jnp.minimum(row + 2 * WINDOW, seq - 1 - base)
        valid = ((col >= lo) & (col <= hi))[None]
        for kv in range(N_KV_HEADS):
            kc = slice(kv * HEAD_DIM, (kv + 1) * HEAD_DIM)
            q4 = jnp.concatenate(
                [q_ref[pl.ds(r0, BLOCK), (kv * Q_PER_KV + g) * HEAD_DIM:(kv * Q_PER_KV + g + 1) * HEAD_DIM]
                 for g in range(Q_PER_KV)], axis=0)
            kcat = kbuf[pl.ds(r0, 3 * BLOCK), kc]
            vcat = vbuf[pl.ds(r0, 3 * BLOCK), kc]
            s = lax.dot_general(q4, kcat, (((1,), (1,)), ((), ())), preferred_element_type=F32)
            s = (s * scale).reshape(Q_PER_KV, BLOCK, 3 * BLOCK)
            s = jnp.where(valid, s, NEG)
            sink = jnp.zeros((Q_PER_KV, 1, 1), F32)
            for g in range(Q_PER_KV):
                sink = jnp.where(gid == g, sink_ref[0, kv * Q_PER_KV + g], sink)
            m = jnp.maximum(jnp.max(s, axis=-1, keepdims=True), sink)
            p = jnp.exp(s - m)
            p = p / (jnp.sum(p, axis=-1, keepdims=True) + jnp.exp(sink - m))
            p2 = p.astype(BF16).reshape(Q_PER_KV * BLOCK, 3 * BLOCK)
            o = jnp.dot(p2, vcat, preferred_element_type=F32)
            for g in range(Q_PER_KV):
                hc = (kv * Q_PER_KV + g) * HEAD_DIM
                o_ref[pl.ds(r0, BLOCK), hc:hc + HEAD_DIM] = o[g * BLOCK:(g + 1) * BLOCK].astype(BF16)
        return carry

    lax.fori_loop(0, tq // BLOCK, q_block, 0)


def _band_attn(u, sink, seq):
    T = u.shape[0]
    tq = min(512, seq)
    nb = tq // BLOCK
    last_blk = T // BLOCK - 1
    kcol = COL_K // KV_WIDTH
    vcol = COL_V // KV_WIDTH
    prev_map = lambda c: (lambda t: (jnp.maximum(t * nb - 1, 0), c))
    next_map = lambda c: (lambda t: (jnp.minimum((t + 1) * nb, last_blk), c))
    main_map = lambda c: (lambda t: (t, c))
    kern = functools.partial(_band_attn_kernel, seq=seq, tq=tq)
    return pl.pallas_call(
        kern,
        grid=(T // tq,),
        in_specs=[
            pl.BlockSpec(memory_space=pltpu.SMEM),
            pl.BlockSpec((tq, ATTN_WIDTH), lambda t: (t, COL_Q // ATTN_WIDTH)),
            pl.BlockSpec((BLOCK, KV_WIDTH), prev_map(kcol)),
            pl.BlockSpec((tq, KV_WIDTH), main_map(kcol)),
            pl.BlockSpec((BLOCK, KV_WIDTH), next_map(kcol)),
            pl.BlockSpec((BLOCK, KV_WIDTH), prev_map(vcol)),
            pl.BlockSpec((tq, KV_WIDTH), main_map(vcol)),
            pl.BlockSpec((BLOCK, KV_WIDTH), next_map(vcol)),
        ],
        out_specs=pl.BlockSpec((tq, ATTN_WIDTH), lambda t: (t, 0)),
        out_shape=jax.ShapeDtypeStruct((T, ATTN_WIDTH), BF16),
        scratch_shapes=[pltpu.VMEM((tq + 2 * BLOCK, KV_WIDTH), BF16),
                        pltpu.VMEM((tq + 2 * BLOCK, KV_WIDTH), BF16)],
        compiler_params=_params(40),
        name="band_attn",
    )(sink, u, u, u, u, u, u, u)


def _mix_out_kernel(x_ref, xpp_ref, xpm_ref, xpn_ref, at_ref, gp_ref, ga_ref,
                    pw_ref, ps_ref, pp_ref, ap_ref, wo_ref, g2_ref, rwh_ref, rwl_ref, rb_ref,
                    h_ref, hn_ref, ids_ref, wts_ref, xb, pin, *, seq, tm):
    t = pl.program_id(0)
    pos0 = (t * tm) % seq

    xb[0:POOL_HALO, :] = jnp.where(pos0 > 0, xpp_ref[...].astype(F32), 0.0)
    xb[POOL_HALO:POOL_HALO + tm, :] = xpm_ref[...].astype(F32)
    xb[POOL_HALO + tm:, :] = jnp.where(pos0 + tm < seq, xpn_ref[...].astype(F32), 0.0)
    rc = min(MIX_ROW_CHUNK, tm)
    for c in range(tm // rc):
        _mix_out_rows(c * rc, rc, pos0, seq, x_ref, at_ref, gp_ref, ga_ref, pw_ref, ps_ref, pp_ref,
                      ap_ref, wo_ref, g2_ref, rwh_ref, rwl_ref, rb_ref, h_ref, hn_ref, ids_ref, wts_ref, xb, pin)


def _mix_out_rows(r0, rc, pos0, seq, x_ref, at_ref, gp_ref, ga_ref, pw_ref, ps_ref, pp_ref, ap_ref, wo_ref,
                  g2_ref, rwh_ref, rwl_ref, rb_ref, h_ref, hn_ref, ids_ref, wts_ref, xb, pin):
    rows = slice(r0, r0 + rc)
    tpos = pos0 + r0 + lax.broadcasted_iota(jnp.int32, (rc, 1), 0)
    for g, half in enumerate(POOL_HALF):
        cols = slice(g * POOL_GC, (g + 1) * POOL_GC)
        c0 = POOL_HALO + r0
        win = xb[c0 - half:c0 - half + rc, cols]
        for d in range(-half + 1, half):
            win = win + xb[c0 + d:c0 + d + rc, cols]
        cnt = (jnp.minimum(tpos + half, seq) - jnp.maximum(tpos - half, 0)).astype(F32)
        mixed = (win / cnt - xb[c0:c0 + rc, cols]).astype(BF16)
        yg = jnp.dot(mixed, pw_ref[g], preferred_element_type=F32)
        pin[rows, cols] = (yg * ps_ref[:, cols]).astype(BF16)
    pool_out = jnp.dot(pin[rows, :], pp_ref[...], preferred_element_type=F32)

    attn_out = jnp.dot(at_ref[rows, :], ap_ref[...], preferred_element_type=F32)
    merged = (jax.nn.sigmoid(gp_ref[rows, :].astype(F32)) * pool_out
              + jax.nn.sigmoid(ga_ref[rows, :].astype(F32)) * attn_out)
    h = x_ref[rows, :] + jnp.dot(merged.astype(BF16), wo_ref[...], preferred_element_type=F32)
    h_ref[rows, :] = h

    ms = jnp.mean(h * h, axis=-1, keepdims=True)
    hn = h * lax.rsqrt(ms + EPS) * g2_ref[...]
    _store_slabs(hn_ref, r0 * SLAB_PITCH, rc, hn)

    hi = hn.astype(BF16)
    lo = (hn - hi.astype(F32)).astype(BF16)
    logit = (jnp.dot(hi, rwh_ref[...], preferred_element_type=F32)
             + jnp.dot(hi, rwl_ref[...], preferred_element_type=F32)
             + jnp.dot(lo, rwh_ref[...], preferred_element_type=F32)) + rb_ref[...]

    lane_i = lax.broadcasted_iota(jnp.int32, logit.shape, 1)
    lane = lane_i.astype(F32)
    big = float(LANES)
    is_grp = (lane_i >= N_EXPERTS) & (lane_i < N_EXPERTS + N_GROUPS)
    mg = jnp.max(jnp.where(is_grp, logit, -jnp.inf), axis=-1, keepdims=True)
    grp = jnp.min(jnp.where(is_grp & (logit == mg), lane - N_EXPERTS, big), axis=-1, keepdims=True)
    p_grp = 1.0 / jnp.sum(jnp.where(is_grp, jnp.exp(logit - mg), 0.0), axis=-1, keepdims=True)

    grp_of_lane = (lane_i // EXPERTS_PER_GROUP).astype(F32)
    in_grp = (lane_i < N_EXPERTS) & (grp_of_lane == grp)
    m1 = jnp.max(jnp.where(in_grp, logit, -jnp.inf), axis=-1, keepdims=True)
    i1 = jnp.min(jnp.where(in_grp & (logit == m1), lane, big), axis=-1, keepdims=True)
    rest = in_grp & (lane != i1)
    m2 = jnp.max(jnp.where(rest, logit, -jnp.inf), axis=-1, keepdims=True)
    i2 = jnp.min(jnp.where(rest & (logit == m2), lane, big), axis=-1, keepdims=True)
    e2 = jnp.exp(m2 - m1)
    den = 1.0 + e2
    ids_ref[rows, :] = jnp.where(lane_i == 0, i1, jnp.where(lane_i == 1, i2, 0.0)).astype(jnp.int32)
    wts_ref[rows, :] = jnp.where(lane_i == 0, p_grp * (1.0 / den),
                                 jnp.where(lane_i == 1, p_grp * (e2 / den), 0.0))


def _mix_out(x2d, u, attn, pool_w, pool_scale, pool_proj, attn_proj, w_out, g2, rw_hi, rw_lo, rb, seq):
    T = x2d.shape[0]
    tm = min(256, seq)
    nh = tm // POOL_HALO
    last_halo = T // POOL_HALO - 1
    pcol = COL_P // POOL_WIDTH
    kern = functools.partial(_mix_out_kernel, seq=seq, tm=tm)
    row_spec = lambda w: pl.BlockSpec((tm, w), lambda t: (t, 0))
    return pl.pallas_call(
        kern,
        grid=(T // tm,),
        in_specs=[
            row_spec(D_MODEL),
            pl.BlockSpec((POOL_HALO, POOL_WIDTH), lambda t: (jnp.maximum(t * nh - 1, 0), pcol)),
            pl.BlockSpec((tm, POOL_WIDTH), lambda t: (t, pcol)),
            pl.BlockSpec((POOL_HALO, POOL_WIDTH), lambda t: (jnp.minimum((t + 1) * nh, last_halo), pcol)),
            row_spec(ATTN_WIDTH),
            pl.BlockSpec((tm, D_MODEL), lambda t: (t, COL_GP // D_MODEL)),
            pl.BlockSpec((tm, D_MODEL), lambda t: (t, COL_GA // D_MODEL)),
            _resident((POOL_GROUPS, POOL_GC, POOL_GC)),
            _resident((1, POOL_WIDTH)),
            _resident((POOL_WIDTH, D_MODEL)),
            _resident((ATTN_WIDTH, D_MODEL)),
            _resident((D_MODEL, D_MODEL)),
            _resident((1, D_MODEL)),
            _resident((D_MODEL, LANES)),
            _resident((D_MODEL, LANES)),
            _resident((1, LANES)),
        ],
        out_specs=[row_spec(D_MODEL), pl.BlockSpec((tm * SLAB_PITCH, LANES), lambda t: (t, 0)),
                   row_spec(LANES), row_spec(LANES)],
        out_shape=[jax.ShapeDtypeStruct((T, D_MODEL), F32),
                   jax.ShapeDtypeStruct((T * SLAB_PITCH, LANES), F32),
                   jax.ShapeDtypeStruct((T, LANES), jnp.int32),
                   jax.ShapeDtypeStruct((T, LANES), F32)],
        scratch_shapes=[pltpu.VMEM((tm + 2 * POOL_HALO, POOL_WIDTH), F32),
                        pltpu.VMEM((tm, POOL_WIDTH), BF16)],
        compiler_params=_params(56),
        name="mix_out",
    )(x2d, u, u, u, attn, u, u, pool_w, pool_scale, pool_proj, attn_proj, w_out, g2, rw_hi, rw_lo, rb)


def _combine_kernel(h_ref, wts_ref, o0_ref, o1_ref, y_ref, *, tk):
    w = wts_ref[...]
    w0 = w[:, 0:1]
    w1 = w[:, 1:2]
    o0 = _load_slabs(o0_ref, 0, tk)
    o1 = _load_slabs(o1_ref, 0, tk)
    for j in range(SLAB_ROWS):
        cols = slice(j * LANES, (j + 1) * LANES)
        y_ref[:, cols] = h_ref[:, cols] + (w0 * o0[j] + w1 * o1[j])


def _combine(h, wts, o_slabs, tk):
    T = h.shape[0]
    nt = T // tk
    return pl.pallas_call(
        functools.partial(_combine_kernel, tk=tk),
        grid=(nt,),
        in_specs=[
            pl.BlockSpec((tk, D_MODEL), lambda t: (t, 0)),
            pl.BlockSpec((tk, LANES), lambda t: (t, 0)),
            pl.BlockSpec((tk * SLAB_PITCH, LANES), lambda t: (t, 0)),
            pl.BlockSpec((tk * SLAB_PITCH, LANES), lambda t: (nt + t, 0)),
        ],
        out_specs=pl.BlockSpec((tk, D_MODEL), lambda t: (t, 0)),
        out_shape=jax.ShapeDtypeStruct((T, D_MODEL), F32),
        compiler_params=_params(40),
        name="combine",
    )(h, wts, o_slabs, o_slabs)


def _moe_ffn_kernel(be_ref, nu_ref, g0_ref, gnext_ref, sprev_ref, scur_ref, hn_ref, wg_ref, wu_ref, wd_ref,
                    o_ref, xbuf, obuf, sem_g, sem_s, *, tb, dummy0):
    del be_ref
    b = pl.program_id(0)
    n_used = nu_ref[0]
    slot = b % 2
    half = tb * SLAB_PITCH
    mine = pl.multiple_of(slot * half, 8)
    other = pl.multiple_of((1 - slot) * half, 8)

    def slab_copy(src, src_row, dst, dst_row, sem):
        return pltpu.make_async_copy(src.at[pl.ds(src_row, SLAB_PITCH)], dst.at[pl.ds(dst_row, SLAB_PITCH)], sem)

    def half_copy(src, dst, dst_row, sem):
        return pltpu.make_async_copy(src.at[pl.ds(0, half)], dst.at[pl.ds(dst_row, half)], sem)

    def start_gathers(tok_ref, base, sem):
        for r in range(tb):
            slab_copy(hn_ref, tok_ref[0, 0, r] * SLAB_PITCH, xbuf, base + r * SLAB_PITCH, sem).start()

    def start_scatters(dst_of_row, base, sem):
        for r in range(tb):
            slab_copy(obuf, base + r * SLAB_PITCH, o_ref, dst_of_row(r) * SLAB_PITCH, sem).start()

    @pl.when(b == 0)
    def _():
        obuf[...] = jnp.zeros_like(obuf)
        start_gathers(g0_ref, 0, sem_g.at[0])
        half_copy(obuf, o_ref, dummy0 * SLAB_PITCH, sem_s.at[0]).start()

    @pl.when(b < n_used)
    def _():
        half_copy(hn_ref, xbuf, mine, sem_g.at[slot]).wait()
        half_copy(obuf, o_ref, 0, sem_s.at[slot]).wait()
        start_gathers(gnext_ref, other, sem_g.at[1 - slot])
        first = b == 0
        start_scatters(lambda r: jnp.where(first, dummy0 + tb + r, sprev_ref[0, 0, r]), other, sem_s.at[1 - slot])

        x = jnp.concatenate([p.astype(BF16) for p in _load_slabs(xbuf, mine, tb)], axis=1)
        gate = jnp.dot(x, wg_ref[0], preferred_element_type=F32)
        up = jnp.dot(x, wu_ref[0], preferred_element_type=F32)
        mid = (jax.nn.silu(gate) * up).astype(BF16)
        out = jnp.dot(mid, wd_ref[0], preferred_element_type=F32)
        _store_slabs(obuf, mine, tb, out)

        @pl.when(b == n_used - 1)
        def _():
            half_copy(hn_ref, xbuf, other, sem_g.at[1 - slot]).wait()
            start_scatters(lambda r: scur_ref[0, 0, r], mine, sem_s.at[slot])
            half_copy(obuf, o_ref, 0, sem_s.at[1 - slot]).wait()
            half_copy(obuf, o_ref, 0, sem_s.at[slot]).wait()


def _moe_ffn(hn_slabs, plan, w_gate, w_up, w_down, n_tokens, tb):
    nblk = plan["gather_tok"].shape[0]
    dummy0 = TOP_K * n_tokens
    n_slabs = dummy0 + 2 * tb
    exp_map = lambda i, be, nu: (be[i], 0, 0)
    idx_spec = lambda f: pl.BlockSpec((1, 1, tb), lambda i, be, nu: (f(i), 0, 0), memory_space=pltpu.SMEM)
    grid_spec = pltpu.PrefetchScalarGridSpec(
        num_scalar_prefetch=2,
        grid=(nblk,),
        in_specs=[
            idx_spec(lambda i: 0),
            idx_spec(lambda i: jnp.minimum(i + 1, nblk - 1)),
            idx_spec(lambda i: jnp.maximum(i - 1, 0)),
            idx_spec(lambda i: i),
            pl.BlockSpec(memory_space=pl.ANY),
            pl.BlockSpec((1, D_MODEL, D_FF), exp_map),
            pl.BlockSpec((1, D_MODEL, D_FF), exp_map),
            pl.BlockSpec((1, D_FF, D_MODEL), exp_map),
        ],
        out_specs=pl.BlockSpec(memory_space=pl.ANY),
        scratch_shapes=[pltpu.VMEM((2 * tb * SLAB_PITCH, LANES), F32),
                        pltpu.VMEM((2 * tb * SLAB_PITCH, LANES), F32),
                        pltpu.SemaphoreType.DMA((2,)),
                        pltpu.SemaphoreType.DMA((2,))],
    )
    return pl.pallas_call(
        functools.partial(_moe_ffn_kernel, tb=tb, dummy0=dummy0),
        grid_spec=grid_spec,
        out_shape=jax.ShapeDtypeStruct((n_slabs * SLAB_PITCH, LANES), F32),
        compiler_params=_params(56),
        name="moe_ffn",
    )(plan["blk_exp"], plan["n_used"], plan["gather_tok"], plan["gather_tok"], plan["scatter_dst"],
      plan["scatter_dst"], hn_slabs, w_gate, w_up, w_down)


def _slot_plan(ids, tb):
    T = ids.shape[0]
    M = T * TOP_K
    i32 = jnp.int32
    e = ids[:, :TOP_K].reshape(M)
    skey = jnp.sort(e * M + jnp.arange(M, dtype=i32))
    sm = skey % M
    bounds = jnp.searchsorted(skey, jnp.arange(N_EXPERTS + 1, dtype=i32) * M).astype(i32)
    start = bounds[:-1]
    cnt = bounds[1:] - start
    nb = (cnt + tb - 1) // tb
    blk_end = jnp.cumsum(nb)
    n_used = blk_end[-1]
    nblk = M // tb + N_EXPERTS
    b = jnp.arange(nblk, dtype=i32)
    be = jnp.minimum(jnp.searchsorted(blk_end, b, side="right"), N_EXPERTS - 1).astype(i32)
    j = b - (blk_end[be] - nb[be])
    row0 = start[be] + j * tb
    n_valid = jnp.where(b < n_used, jnp.clip(cnt[be] - j * tb, 0, tb), 0)
    r = jnp.arange(tb, dtype=i32)[None, :]
    valid = r < n_valid[:, None]
    m = sm[jnp.minimum(row0[:, None] + r, M - 1)]
    tok = m // TOP_K
    k = m % TOP_K
    spare = TOP_K * T + (b % 2)[:, None] * tb + r
    blk_exp = jnp.where(b < n_used, be, be[jnp.maximum(n_used - 1, 0)])
    return dict(blk_exp=blk_exp.astype(i32), n_used=n_used.astype(i32).reshape(1),
                gather_tok=jnp.where(valid, tok, 0).astype(i32).reshape(nblk, 1, tb),
                scatter_dst=jnp.where(valid, k * T + tok, spare).astype(i32).reshape(nblk, 1, tb))


def _rope_tables(seq):
    inv = ROPE_THETA ** (-jnp.arange(ROT_HALF, dtype=F32) / ROT_HALF)
    ang = jnp.arange(seq, dtype=jnp.int32).astype(F32)[:, None] * inv[None, :]
    cos = jnp.cos(ang)
    sin = jnp.sin(ang)
    ones = jnp.ones((seq, HEAD_DIM - ROT_DIM), F32)
    cos_t = jnp.concatenate([cos, cos, ones], axis=1)
    sin_t = jnp.concatenate([-sin, sin, 0.0 * ones], axis=1)
    return cos_t, sin_t


def _trunk(x, prm):
    B, S, D = x.shape
    T = B * S
    x2d = x.reshape(T, D)
    cos_t, sin_t = _rope_tables(S)
    u = _in_proj(x2d, prm["g1"], prm["w_in"], prm["head_gain"], cos_t, sin_t, S)
    attn = _band_attn(u, prm["sink"], S)
    h, hn, ids, wts = _mix_out(x2d, u, attn, prm["pool_w"], prm["pool_scale"], prm["pool_proj"],
                               prm["attn_proj"], prm["w_out"], prm["g2"], prm["rw_hi"], prm["rw_lo"],
                               prm["rb"], S)
    tb = 256
    plan = _slot_plan(ids, tb)
    o_slabs = _moe_ffn(hn, plan, prm["w_gate"], prm["w_up"], prm["w_down"], T, tb)
    y = _combine(h, wts, o_slabs, min(256, S))
    return y.reshape(B, S, D)


def kernel(x_prompt, x_sample, norm1_g, w_in, pool_w, pool_scale, pool_proj, q_norm_g, k_norm_g, sink,
           attn_proj, w_out, norm2_g, router_group_w, router_group_b, router_expert_w, router_expert_b,
           w_gate, w_up, w_down):
    assert norm1_g.shape[0] == 1, "single-layer trunk"
    c0 = POOL_WIDTH
    c1 = c0 + ATTN_WIDTH
    c2 = c1 + KV_WIDTH
    c3 = c2 + KV_WIDTH
    wi = w_in[0]
    w_perm = jnp.concatenate([wi[:, c0:c1], wi[:, c1:c2], wi[:, c2:c3], wi[:, :c0], wi[:, c3:]],
                             axis=1).astype(BF16)
    head_gain = jnp.concatenate([jnp.tile(q_norm_g[0], N_HEADS), jnp.tile(k_norm_g[0], N_KV_HEADS)])[None, :]
    pad = LANES - N_EXPERTS - N_GROUPS
    rw = jnp.concatenate([router_expert_w[0], router_group_w[0], jnp.zeros((D_MODEL, pad), F32)], axis=1)
    rw_hi = rw.astype(BF16)
    rw_lo = (rw - rw_hi.astype(F32)).astype(BF16)
    rb = jnp.concatenate([router_expert_b[0], router_group_b[0], jnp.zeros((pad,), F32)])[None, :]
    prm = dict(
        g1=norm1_g[0][None, :], w_in=w_perm, head_gain=head_gain, sink=sink[0][None, :],
        pool_w=pool_w[0].astype(BF16), pool_scale=pool_scale[0][None, :], pool_proj=pool_proj[0].astype(BF16),
        attn_proj=attn_proj[0].astype(BF16), w_out=w_out[0].astype(BF16), g2=norm2_g[0][None, :],
        rw_hi=rw_hi, rw_lo=rw_lo, rb=rb,
        w_gate=w_gate[0].astype(BF16), w_up=w_up[0].astype(BF16), w_down=w_down[0].astype(BF16),
    )
    return (_trunk(x_prompt, prm), _trunk(x_sample, prm))
```

```python
import functools

import jax
import jax.numpy as jnp
from jax import lax
from jax.experimental import pallas as pl
from jax.experimental.pallas import tpu as pltpu

F32 = jnp.float32
BF16 = jnp.bfloat16

D_MODEL = 2048
POOL_GROUPS = 4
POOL_HALF = (1, 2, 4, 8)
POOL_WIDTH = D_MODEL // 2
POOL_GC = POOL_WIDTH // POOL_GROUPS
HEAD_DIM = 128
N_HEADS = D_MODEL // HEAD_DIM
N_KV_HEADS = N_HEADS // 4
Q_PER_KV = N_HEADS // N_KV_HEADS
ATTN_WIDTH = N_HEADS * HEAD_DIM
KV_WIDTH = N_KV_HEADS * HEAD_DIM
WINDOW = 128
BLOCK = 128
ROT_DIM = HEAD_DIM // 4
ROT_HALF = ROT_DIM // 2
ROPE_THETA = 500000.0
IN_WIDTH = POOL_WIDTH + ATTN_WIDTH + 2 * KV_WIDTH + 2 * D_MODEL
N_GROUPS = 4
EXPERTS_PER_GROUP = 8
N_EXPERTS = N_GROUPS * EXPERTS_PER_GROUP
TOP_K = 2
D_FF = D_MODEL // 2
EPS = 1e-6
NEG = -1e30

LANES = 128
POOL_HALO = 16
MIX_ROW_CHUNK = 256
MIB = 1024 * 1024
SLAB_ROWS = D_MODEL // LANES
SLAB_PITCH = SLAB_ROWS + 1

COL_Q = 0
COL_K = COL_Q + ATTN_WIDTH
COL_V = COL_K + KV_WIDTH
COL_P = COL_V + KV_WIDTH
COL_GP = COL_P + POOL_WIDTH
COL_GA = COL_GP + D_MODEL
QK_WIDTH = ATTN_WIDTH + KV_WIDTH
IN_TN = 512


def _params(vmem_mib, n_axes=1):
    return pltpu.CompilerParams(dimension_semantics=("arbitrary",) * n_axes,
                                vmem_limit_bytes=vmem_mib * MIB)


def _resident(shape):
    return pl.BlockSpec(shape, lambda *_: (0,) * len(shape), pipeline_mode=pl.Buffered(1))


def _tile_seq_pos(t, tile, segs):
    (t1, s1), (_, s2) = segs
    n1 = t1 // tile
    first = t < n1
    seq = jnp.where(first, s1, s2)
    pos0 = jnp.where(first, (t * tile) % s1, ((t - n1) * tile) % s2)
    return seq, pos0


def _split_specs(tile, width, n1):
    return [pl.BlockSpec((tile, width), lambda t, *_: (jnp.minimum(t, n1 - 1), 0)),
            pl.BlockSpec((tile, width), lambda t, *_: (jnp.maximum(t - n1, 0), 0))]


def _store_slabs(ref, base_row, n, val):
    for j in range(SLAB_ROWS):
        ref[pl.ds(base_row + j, n, stride=SLAB_PITCH), :] = val[:, j * LANES:(j + 1) * LANES]
    ref[pl.ds(base_row + SLAB_ROWS, n, stride=SLAB_PITCH), :] = jnp.zeros((n, LANES), F32)


def _load_slabs(ref, base_row, n):
    return [ref[pl.ds(base_row + j, n, stride=SLAB_PITCH), :] for j in range(SLAB_ROWS)]


def _in_proj_kernel(x1_ref, x2_ref, g1_ref, w_ref, hg_ref, cos_ref, sin_ref, o_ref, xn_ref, *,
                    n1, n_qk_blocks, row_chunk):
    i = pl.program_id(0)
    j = pl.program_id(1)

    @pl.when(j == 0)
    def _():
        x = jnp.where(i < n1, x1_ref[...], x2_ref[...])
        ms = jnp.mean(x * x, axis=-1, keepdims=True)
        xn_ref[...] = (x * lax.rsqrt(ms + EPS) * g1_ref[...]).astype(BF16)

    tm, tn = o_ref.shape

    def chunk_dot(c):
        rows = slice(c * row_chunk, (c + 1) * row_chunk)
        return rows, jnp.dot(xn_ref[rows, :], w_ref[...], preferred_element_type=F32)

    @pl.when(j < n_qk_blocks)
    def _():
        lane = lax.broadcasted_iota(jnp.int32, (1, HEAD_DIM), 1)
        for c in range(tm // row_chunk):
            rows, acc = chunk_dot(c)
            cos = cos_ref[rows, :]
            sin = sin_ref[rows, :]
            for h in range(tn // HEAD_DIM):
                cols = slice(h * HEAD_DIM, (h + 1) * HEAD_DIM)
                a = acc[:, cols]
                ms = jnp.mean(a * a, axis=-1, keepdims=True)
                y = a * lax.rsqrt(ms + EPS) * hg_ref[:, cols]
                partner = jnp.where(lane < ROT_HALF,
                                    pltpu.roll(y, HEAD_DIM - ROT_HALF, 1),
                                    pltpu.roll(y, ROT_HALF, 1))
                o_ref[rows, cols] = (y * cos + partner * sin).astype(BF16)

    @pl.when(j >= n_qk_blocks)
    def _():
        for c in range(tm // row_chunk):
            rows, acc = chunk_dot(c)
            o_ref[rows, :] = acc.astype(BF16)


def _w_in_block(j):
    n_pool = POOL_WIDTH // IN_TN
    n_qkv = (ATTN_WIDTH + 2 * KV_WIDTH) // IN_TN
    return jnp.where(j < n_qkv, j + n_pool, jnp.where(j < n_qkv + n_pool, j - n_qkv, j))


def _in_proj(x1, x2, g1, w_in, head_gain, cos_t, sin_t, segs, tm):
    (t1, s1), (t2, s2) = segs
    n1 = t1 // tm
    p1 = s1 // tm
    p2 = s2 // tm
    pos_tile = lambda i, j: (jnp.where(i < n1, i % p1, (i - n1) % p2), 0)
    kern = functools.partial(_in_proj_kernel, n1=n1, n_qk_blocks=QK_WIDTH // IN_TN, row_chunk=min(256, tm))
    return pl.pallas_call(
        kern,
        grid=((t1 + t2) // tm, IN_WIDTH // IN_TN),
        in_specs=_split_specs(tm, D_MODEL, n1) + [
            pl.BlockSpec((1, D_MODEL), lambda i, j: (0, 0)),
            pl.BlockSpec((D_MODEL, IN_TN), lambda i, j: (0, _w_in_block(j))),
            pl.BlockSpec((1, IN_TN), lambda i, j: (0, jnp.minimum(j, QK_WIDTH // IN_TN - 1))),
            pl.BlockSpec((tm, HEAD_DIM), pos_tile),
            pl.BlockSpec((tm, HEAD_DIM), pos_tile),
        ],
        out_specs=pl.BlockSpec((tm, IN_TN), lambda i, j: (i, j)),
        out_shape=jax.ShapeDtypeStruct((t1 + t2, IN_WIDTH), BF16),
        scratch_shapes=[pltpu.VMEM((tm, D_MODEL), BF16)],
        compiler_params=_params(56, 2),
        name="in_proj",
    )(x1, x2, g1, w_in, head_gain, cos_t, sin_t)


def _band_attn_kernel(sink_ref, q_ref, kp_ref, km_ref, kn_ref, vp_ref, vm_ref, vn_ref, o_ref,
                      kbuf, vbuf, *, segs, tq):
    seq, pos0 = _tile_seq_pos(pl.program_id(0), tq, segs)
    kbuf[0:BLOCK, :] = kp_ref[...]
    kbuf[BLOCK:BLOCK + tq, :] = km_ref[...]
    kbuf[BLOCK + tq:, :] = kn_ref[...]
    vbuf[0:BLOCK, :] = vp_ref[...]
    vbuf[BLOCK:BLOCK + tq, :] = vm_ref[...]
    vbuf[BLOCK + tq:, :] = vn_ref[...]

    scale = HEAD_DIM ** -0.5
    row = lax.broadcasted_iota(jnp.int32, (BLOCK, 3 * BLOCK), 0)
    col = lax.broadcasted_iota(jnp.int32, (BLOCK, 3 * BLOCK), 1)
    gid = lax.broadcasted_iota(jnp.int32, (Q_PER_KV, 1, 1), 0)

    def q_block(qb, carry):
        r0 = pl.multiple_of(qb * BLOCK, BLOCK)
        base = pos0 + r0 - BLOCK
        lo = jnp.maximum(row, -base)
        hi = jnp.minimum(row + 2 * WINDOW, seq - 1 - base)
        valid = ((col >= lo) & (col <= hi))[None]
        for kv in range(N_KV_HEADS):
            kc = slice(kv * HEAD_DIM, (kv + 1) * HEAD_DIM)
            q4 = jnp.concatenate(
                [q_ref[pl.ds(r0, BLOCK), (kv * Q_PER_KV + g) * HEAD_DIM:(kv * Q_PER_KV + g + 1) * HEAD_DIM]
                 for g in range(Q_PER_KV)], axis=0)
            kcat = kbuf[pl.ds(r0, 3 * BLOCK), kc]
            vcat = vbuf[pl.ds(r0, 3 * BLOCK), kc]
            s = lax.dot_general(q4, kcat, (((1,), (1,)), ((), ())), preferred_element_type=F32)
            s = (s * scale).reshape(Q_PER_KV, BLOCK, 3 * BLOCK)
            s = jnp.where(valid, s, NEG)
            sink = jnp.zeros((Q_PER_KV, 1, 1), F32)
            for g in range(Q_PER_KV):
                sink = jnp.where(gid == g, sink_ref[0, kv * Q_PER_KV + g], sink)
            m = jnp.maximum(jnp.max(s, axis=-1, keepdims=True), sink)
            p = jnp.exp(s - m)
            p = p / (jnp.sum(p, axis=-1, keepdims=True) + jnp.exp(sink - m))
            p2 = p.astype(BF16).reshape(Q_PER_KV * BLOCK, 3 * BLOCK)
            o = jnp.dot(p2, vcat, preferred_element_type=F32)
            for g in range(Q_PER_KV):
                hc = (kv * Q_PER_KV + g) * HEAD_DIM
                o_ref[pl.ds(r0, BLOCK), hc:hc + HEAD_DIM] = o[g * BLOCK:(g + 1) * BLOCK].astype(BF16)
        return carry

    lax.fori_loop(0, tq // BLOCK, q_block, 0)


def _band_attn(u, sink, segs, tq):
    T = u.shape[0]
    nb = tq // BLOCK
    last_blk = T // BLOCK - 1
    kcol = COL_K // KV_WIDTH
    vcol = COL_V // KV_WIDTH
    prev_map = lambda c: (lambda t: (jnp.maximum(t * nb - 1, 0), c))
    next_map = lambda c: (lambda t: (jnp.minimum((t + 1) * nb, last_blk), c))
    main_map = lambda c: (lambda t: (t, c))
    kern = functools.partial(_band_attn_kernel, segs=segs, tq=tq)
    return pl.pallas_call(
        kern,
        grid=(T // tq,),
        in_specs=[
            pl.BlockSpec(memory_space=pltpu.SMEM),
            pl.BlockSpec((tq, ATTN_WIDTH), lambda t: (t, COL_Q // ATTN_WIDTH)),
            pl.BlockSpec((BLOCK, KV_WIDTH), prev_map(kcol)),
            pl.BlockSpec((tq, KV_WIDTH), main_map(kcol)),
            pl.BlockSpec((BLOCK, KV_WIDTH), next_map(kcol)),
            pl.BlockSpec((BLOCK, KV_WIDTH), prev_map(vcol)),
            pl.BlockSpec((tq, KV_WIDTH), main_map(vcol)),
            pl.BlockSpec((BLOCK, KV_WIDTH), next_map(vcol)),
        ],
        out_specs=pl.BlockSpec((tq, ATTN_WIDTH), lambda t: (t, 0)),
        out_shape=jax.ShapeDtypeStruct((T, ATTN_WIDTH), BF16),
        scratch_shapes=[pltpu.VMEM((tq + 2 * BLOCK, KV_WIDTH), BF16),
                        pltpu.VMEM((tq + 2 * BLOCK, KV_WIDTH), BF16)],
        compiler_params=_params(40),
        name="band_attn",
    )(sink, u, u, u, u, u, u, u)


def _mix_out_kernel(x1_ref, x2_ref, xpp_ref, xpm_ref, xpn_ref, at_ref, gp_ref, ga_ref,
                    pw_ref, ps_ref, pp_ref, ap_ref, wo_ref, g2_ref, rwc_ref, rwh_ref, rb_ref,
                    h_ref, hn_ref, ids_ref, wts_ref, xb, pin, *, segs, tm):
    t = pl.program_id(0)
    seq, pos0 = _tile_seq_pos(t, tm, segs)
    in_first = t < segs[0][0] // tm

    xb[0:POOL_HALO, :] = jnp.where(pos0 > 0, xpp_ref[...].astype(F32), 0.0)
    xb[POOL_HALO:POOL_HALO + tm, :] = xpm_ref[...].astype(F32)
    xb[POOL_HALO + tm:, :] = jnp.where(pos0 + tm < seq, xpn_ref[...].astype(F32), 0.0)
    rc = min(MIX_ROW_CHUNK, tm)
    for c in range(tm // rc):
        r0 = c * rc
        rows = slice(r0, r0 + rc)
        x = jnp.where(in_first, x1_ref[rows, :], x2_ref[rows, :])
        _mix_out_rows(r0, rc, pos0, seq, x, at_ref, gp_ref, ga_ref, pw_ref, ps_ref, pp_ref,
                      ap_ref, wo_ref, g2_ref, rwc_ref, rwh_ref, rb_ref, h_ref, hn_ref, ids_ref, wts_ref, xb, pin)


def _mix_out_rows(r0, rc, pos0, seq, x, at_ref, gp_ref, ga_ref, pw_ref, ps_ref, pp_ref, ap_ref, wo_ref,
                  g2_ref, rwc_ref, rwh_ref, rb_ref, h_ref, hn_ref, ids_ref, wts_ref, xb, pin):
    rows = slice(r0, r0 + rc)
    tpos = pos0 + r0 + lax.broadcasted_iota(jnp.int32, (rc, 1), 0)
    for g, half in enumerate(POOL_HALF):
        cols = slice(g * POOL_GC, (g + 1) * POOL_GC)
        c0 = POOL_HALO + r0
        win = xb[c0 - half:c0 - half + rc, cols]
        for d in range(-half + 1, half):
            win = win + xb[c0 + d:c0 + d + rc, cols]
        cnt = (jnp.minimum(tpos + half, seq) - jnp.maximum(tpos - half, 0)).astype(F32)
        mixed = (win / cnt - xb[c0:c0 + rc, cols]).astype(BF16)
        yg = jnp.dot(mixed, pw_ref[g], preferred_element_type=F32)
        pin[rows, cols] = (yg * ps_ref[:, cols]).astype(BF16)
    pool_out = jnp.dot(pin[rows, :], pp_ref[...], preferred_element_type=F32)

    attn_out = jnp.dot(at_ref[rows, :], ap_ref[...], preferred_element_type=F32)
    merged = (jax.nn.sigmoid(gp_ref[rows, :].astype(F32)) * pool_out
              + jax.nn.sigmoid(ga_ref[rows, :].astype(F32)) * attn_out)
    h = x + jnp.dot(merged.astype(BF16), wo_ref[...], preferred_element_type=F32)
    h_ref[rows, :] = h

    ms = jnp.mean(h * h, axis=-1, keepdims=True)
    hn = h * lax.rsqrt(ms + EPS) * g2_ref[...]
    _store_slabs(hn_ref, r0 * SLAB_PITCH, rc, hn)

    hi = hn.astype(BF16)
    lo = (hn - hi.astype(F32)).astype(BF16)
    both = jnp.dot(hi, rwc_ref[...], preferred_element_type=F32)
    logit = (both[:, :LANES] + both[:, LANES:]
             + jnp.dot(lo, rwh_ref[...], preferred_element_type=F32)) + rb_ref[...]

    lane_i = lax.broadcasted_iota(jnp.int32, logit.shape, 1)
    lane = lane_i.astype(F32)
    big = float(LANES)
    is_grp = (lane_i >= N_EXPERTS) & (lane_i < N_EXPERTS + N_GROUPS)
    mg = jnp.max(jnp.where(is_grp, logit, -jnp.inf), axis=-1, keepdims=True)
    grp = jnp.min(jnp.where(is_grp & (logit == mg), lane - N_EXPERTS, big), axis=-1, keepdims=True)
    p_grp = 1.0 / jnp.sum(jnp.where(is_grp, jnp.exp(logit - mg), 0.0), axis=-1, keepdims=True)

    grp_of_lane = (lane_i // EXPERTS_PER_GROUP).astype(F32)
    in_grp = (lane_i < N_EXPERTS) & (grp_of_lane == grp)
    m1 = jnp.max(jnp.where(in_grp, logit, -jnp.inf), axis=-1, keepdims=True)
    i1 = jnp.min(jnp.where(in_grp & (logit == m1), lane, big), axis=-1, keepdims=True)
    rest = in_grp & (lane != i1)
    m2 = jnp.max(jnp.where(rest, logit, -jnp.inf), axis=-1, keepdims=True)
    i2 = jnp.min(jnp.where(rest & (logit == m2), lane, big), axis=-1, keepdims=True)
    e2 = jnp.exp(m2 - m1)
    den = 1.0 + e2
    ids_ref[rows, :] = jnp.where(lane_i == 0, i1, jnp.where(lane_i == 1, i2, 0.0)).astype(jnp.int32)
    wts_ref[rows, :] = jnp.where(lane_i == 0, p_grp * (1.0 / den),
                                 jnp.where(lane_i == 1, p_grp * (e2 / den), 0.0))


def _mix_out(x1, x2, u, attn, pool_w, pool_scale, pool_proj, attn_proj, w_out, g2, rw_cat, rw_hi, rb, segs, tm):
    T = u.shape[0]
    n1 = segs[0][0] // tm
    nh = tm // POOL_HALO
    last_halo = T // POOL_HALO - 1
    pcol = COL_P // POOL_WIDTH
    kern = functools.partial(_mix_out_kernel, segs=segs, tm=tm)
    row_spec = lambda w: pl.BlockSpec((tm, w), lambda t: (t, 0))
    return pl.pallas_call(
        kern,
        grid=(T // tm,),
        in_specs=_split_specs(tm, D_MODEL, n1) + [
            pl.BlockSpec((POOL_HALO, POOL_WIDTH), lambda t: (jnp.maximum(t * nh - 1, 0), pcol)),
            pl.BlockSpec((tm, POOL_WIDTH), lambda t: (t, pcol)),
            pl.BlockSpec((POOL_HALO, POOL_WIDTH), lambda t: (jnp.minimum((t + 1) * nh, last_halo), pcol)),
            row_spec(ATTN_WIDTH),
            pl.BlockSpec((tm, D_MODEL), lambda t: (t, COL_GP // D_MODEL)),
            pl.BlockSpec((tm, D_MODEL), lambda t: (t, COL_GA // D_MODEL)),
            _resident((POOL_GROUPS, POOL_GC, POOL_GC)),
            _resident((1, POOL_WIDTH)),
            _resident((POOL_WIDTH, D_MODEL)),
            _resident((ATTN_WIDTH, D_MODEL)),
            _resident((D_MODEL, D_MODEL)),
            _resident((1, D_MODEL)),
            _resident((D_MODEL, 2 * LANES)),
            _resident((D_MODEL, LANES)),
            _resident((1, LANES)),
        ],
        out_specs=[row_spec(D_MODEL), pl.BlockSpec((tm * SLAB_PITCH, LANES), lambda t: (t, 0)),
                   row_spec(LANES), row_spec(LANES)],
        out_shape=[jax.ShapeDtypeStruct((T, D_MODEL), F32),
                   jax.ShapeDtypeStruct((T * SLAB_PITCH, LANES), F32),
                   jax.ShapeDtypeStruct((T, LANES), jnp.int32),
                   jax.ShapeDtypeStruct((T, LANES), F32)],
        scratch_shapes=[pltpu.VMEM((tm + 2 * POOL_HALO, POOL_WIDTH), F32),
                        pltpu.VMEM((tm, POOL_WIDTH), BF16)],
        compiler_params=_params(56),
        name="mix_out",
    )(x1, x2, u, u, u, attn, u, u, pool_w, pool_scale, pool_proj, attn_proj, w_out, g2, rw_cat, rw_hi, rb)


def _combine_kernel(h_ref, wts_ref, o0_ref, o1_ref, y1_ref, y2_ref, *, tk, n1):
    t = pl.program_id(0)
    w = wts_ref[...]
    w0 = w[:, 0:1]
    w1 = w[:, 1:2]
    o0 = _load_slabs(o0_ref, 0, tk)
    o1 = _load_slabs(o1_ref, 0, tk)

    def write(y_ref):
        for j in range(SLAB_ROWS):
            cols = slice(j * LANES, (j + 1) * LANES)
            y_ref[:, cols] = h_ref[:, cols] + (w0 * o0[j] + w1 * o1[j])

    pl.when(t < n1)(lambda: write(y1_ref))
    pl.when(t >= n1)(lambda: write(y2_ref))


def _combine(h, wts, o_slabs, segs, tk):
    T = h.shape[0]
    nt = T // tk
    n1 = segs[0][0] // tk
    return pl.pallas_call(
        functools.partial(_combine_kernel, tk=tk, n1=n1),
        grid=(nt,),
        in_specs=[
            pl.BlockSpec((tk, D_MODEL), lambda t: (t, 0)),
            pl.BlockSpec((tk, LANES), lambda t: (t, 0)),
            pl.BlockSpec((tk * SLAB_PITCH, LANES), lambda t: (t, 0)),
            pl.BlockSpec((tk * SLAB_PITCH, LANES), lambda t: (nt + t, 0)),
        ],
        out_specs=_split_specs(tk, D_MODEL, n1),
        out_shape=[jax.ShapeDtypeStruct((segs[0][0], D_MODEL), F32),
                   jax.ShapeDtypeStruct((segs[1][0], D_MODEL), F32)],
        compiler_params=_params(40),
        name="combine",
    )(h, wts, o_slabs, o_slabs)


def _moe_ffn_kernel(be_ref, nu_ref, g0_ref, gnext_ref, sprev_ref, scur_ref, hn_ref, wg_ref, wu_ref, wd_ref,
                    o_ref, xbuf, obuf, sem_g, sem_s, *, tb, dummy0):
    del be_ref
    b = pl.program_id(0)
    n_used = nu_ref[0]
    slot = b % 2
    half = tb * SLAB_PITCH
    mine = pl.multiple_of(slot * half, 8)
    other = pl.multiple_of((1 - slot) * half, 8)

    def slab_copy(src, src_row, dst, dst_row, sem):
        return pltpu.make_async_copy(src.at[pl.ds(src_row, SLAB_PITCH)], dst.at[pl.ds(dst_row, SLAB_PITCH)], sem)

    def half_copy(src, dst, dst_row, sem):
        return pltpu.make_async_copy(src.at[pl.ds(0, half)], dst.at[pl.ds(dst_row, half)], sem)

    def start_gathers(tok_ref, base, sem):
        for r in range(tb):
            slab_copy(hn_ref, tok_ref[0, 0, r] * SLAB_PITCH, xbuf, base + r * SLAB_PITCH, sem).start(priority=r % 2)

    def start_scatters(dst_of_row, base, sem):
        for r in range(tb):
            slab_copy(obuf, base + r * SLAB_PITCH, o_ref, dst_of_row(r) * SLAB_PITCH, sem).start(priority=r % 2)

    @pl.when(b == 0)
    def _():
        obuf[...] = jnp.zeros_like(obuf)
        start_gathers(g0_ref, 0, sem_g.at[0])
        half_copy(obuf, o_ref, dummy0 * SLAB_PITCH, sem_s.at[0]).start()

    @pl.when(b < n_used)
    def _():
        half_copy(hn_ref, xbuf, mine, sem_g.at[slot]).wait()
        half_copy(obuf, o_ref, 0, sem_s.at[slot]).wait()
        start_gathers(gnext_ref, other, sem_g.at[1 - slot])
        first = b == 0
        start_scatters(lambda r: jnp.where(first, dummy0 + tb + r, sprev_ref[0, 0, r]), other, sem_s.at[1 - slot])

        x = jnp.concatenate([p.astype(BF16) for p in _load_slabs(xbuf, mine, tb)], axis=1)
        gate = jnp.dot(x, wg_ref[0], preferred_element_type=F32)
        up = jnp.dot(x, wu_ref[0], preferred_element_type=F32)
        mid = (jax.nn.silu(gate) * up).astype(BF16)
        out = jnp.dot(mid, wd_ref[0], preferred_element_type=F32)
        _store_slabs(obuf, mine, tb, out)

        @pl.when(b == n_used - 1)
        def _():
            half_copy(hn_ref, xbuf, other, sem_g.at[1 - slot]).wait()
            start_scatters(lambda r: scur_ref[0, 0, r], mine, sem_s.at[slot])
            half_copy(obuf, o_ref, 0, sem_s.at[1 - slot]).wait()
            half_copy(obuf, o_ref, 0, sem_s.at[slot]).wait()


def _moe_ffn(hn_slabs, plan, w_gate, w_up, w_down, n_tokens, tb):
    nblk = plan["gather_tok"].shape[0]
    dummy0 = TOP_K * n_tokens
    n_slabs = dummy0 + 2 * tb
    exp_map = lambda i, be, nu: (be[i], 0, 0)
    idx_spec = lambda f: pl.BlockSpec((1, 1, tb), lambda i, be, nu: (f(i), 0, 0), memory_space=pltpu.SMEM)
    grid_spec = pltpu.PrefetchScalarGridSpec(
        num_scalar_prefetch=2,
        grid=(nblk,),
        in_specs=[
            idx_spec(lambda i: 0),
            idx_spec(lambda i: jnp.minimum(i + 1, nblk - 1)),
            idx_spec(lambda i: jnp.maximum(i - 1, 0)),
            idx_spec(lambda i: i),
            pl.BlockSpec(memory_space=pl.ANY),
            pl.BlockSpec((1, D_MODEL, D_FF), exp_map),
            pl.BlockSpec((1, D_MODEL, D_FF), exp_map),
            pl.BlockSpec((1, D_FF, D_MODEL), exp_map),
        ],
        out_specs=pl.BlockSpec(memory_space=pl.ANY),
        scratch_shapes=[pltpu.VMEM((2 * tb * SLAB_PITCH, LANES), F32),
                        pltpu.VMEM((2 * tb * SLAB_PITCH, LANES), F32),
                        pltpu.SemaphoreType.DMA((2,)),
                        pltpu.SemaphoreType.DMA((2,))],
    )
    return pl.pallas_call(
        functools.partial(_moe_ffn_kernel, tb=tb, dummy0=dummy0),
        grid_spec=grid_spec,
        out_shape=jax.ShapeDtypeStruct((n_slabs * SLAB_PITCH, LANES), F32),
        compiler_params=_params(56),
        name="moe_ffn",
    )(plan["blk_exp"], plan["n_used"], plan["gather_tok"], plan["gather_tok"], plan["scatter_dst"],
      plan["scatter_dst"], hn_slabs, w_gate, w_up, w_down)


def _slot_plan(ids, tb):
    T = ids.shape[0]
    M = T * TOP_K
    i32 = jnp.int32
    e = ids[:, :TOP_K].reshape(M)
    skey = jnp.sort(e * M + jnp.arange(M, dtype=i32))
    sm = skey % M
    edges = jnp.arange(N_EXPERTS + 1, dtype=i32) * M
    bounds = jnp.sum((skey[None, :] < edges[:, None]).astype(i32), axis=1)
    start = bounds[:-1]
    cnt = bounds[1:] - start
    nb = (cnt + tb - 1) // tb
    blk_end = jnp.cumsum(nb)
    n_used = blk_end[-1]
    nblk = M // tb + N_EXPERTS
    b = jnp.arange(nblk, dtype=i32)
    be = jnp.minimum(jnp.sum((blk_end[None, :] <= b[:, None]).astype(i32), axis=1), N_EXPERTS - 1)
    j = b - (blk_end[be] - nb[be])
    row0 = start[be] + j * tb
    n_valid = jnp.where(b < n_used, jnp.clip(cnt[be] - j * tb, 0, tb), 0)
    r = jnp.arange(tb, dtype=i32)[None, :]
    valid = r < n_valid[:, None]
    m = sm[jnp.minimum(row0[:, None] + r, M - 1)]
    tok = m // TOP_K
    k = m % TOP_K
    spare = TOP_K * T + (b % 2)[:, None] * tb + r
    blk_exp = jnp.where(b < n_used, be, be[jnp.maximum(n_used - 1, 0)])
    return dict(blk_exp=blk_exp.astype(i32), n_used=n_used.astype(i32).reshape(1),
                gather_tok=jnp.where(valid, tok, 0).astype(i32).reshape(nblk, 1, tb),
                scatter_dst=jnp.where(valid, k * T + tok, spare).astype(i32).reshape(nblk, 1, tb))


def _rope_tables(seq):
    inv = ROPE_THETA ** (-jnp.arange(ROT_HALF, dtype=F32) / ROT_HALF)
    ang = jnp.arange(seq, dtype=jnp.int32).astype(F32)[:, None] * inv[None, :]
    cos = jnp.cos(ang)
    sin = jnp.sin(ang)
    ones = jnp.ones((seq, HEAD_DIM - ROT_DIM), F32)
    cos_t = jnp.concatenate([cos, cos, ones], axis=1)
    sin_t = jnp.concatenate([-sin, sin, 0.0 * ones], axis=1)
    return cos_t, sin_t


def kernel(x_prompt, x_sample, norm1_g, w_in, pool_w, pool_scale, pool_proj, q_norm_g, k_norm_g, sink,
           attn_proj, w_out, norm2_g, router_group_w, router_group_b, router_expert_w, router_expert_b,
           w_gate, w_up, w_down):
    assert norm1_g.shape[0] == 1, "single-layer trunk"
    (b1, s1, d), (b2, s2, _) = x_prompt.shape, x_sample.shape
    t1, t2 = b1 * s1, b2 * s2
    segs = ((t1, s1), (t2, s2))
    s_min = min(s1, s2)
    x1 = x_prompt.reshape(t1, d)
    x2 = x_sample.reshape(t2, d)

    head_gain = jnp.concatenate([jnp.tile(q_norm_g[0], N_HEADS), jnp.tile(k_norm_g[0], N_KV_HEADS)])[None, :]
    pad = LANES - N_EXPERTS - N_GROUPS
    rw = jnp.concatenate([router_expert_w[0], router_group_w[0], jnp.zeros((D_MODEL, pad), F32)], axis=1)
    rw_hi = rw.astype(BF16)
    rw_lo = (rw - rw_hi.astype(F32)).astype(BF16)
    rb = jnp.concatenate([router_expert_b[0], router_group_b[0], jnp.zeros((pad,), F32)])[None, :]
    cos_t, sin_t = _rope_tables(max(s1, s2))

    u = _in_proj(x1, x2, norm1_g[0][None, :], w_in[0].astype(BF16), head_gain, cos_t, sin_t, segs,
                 tm=min(1024, s_min))
    attn = _band_attn(u, sink[0][None, :], segs, tq=min(512, s_min))
    h, hn, ids, wts = _mix_out(x1, x2, u, attn, pool_w[0].astype(BF16), pool_scale[0][None, :],
                               pool_proj[0].astype(BF16), attn_proj[0].astype(BF16), w_out[0].astype(BF16),
                               norm2_g[0][None, :], jnp.concatenate([rw_hi, rw_lo], axis=1), rw_hi, rb,
                               segs, tm=min(256, s_min))
    tb = 256
    plan = _slot_plan(ids, tb)
    o_slabs = _moe_ffn(hn, plan, w_gate[0].astype(BF16), w_up[0].astype(BF16), w_down[0].astype(BF16),
                       t1 + t2, tb)
    y1, y2 = _combine(h, wts, o_slabs, segs, tk=min(256, s_min))
    return (y1.reshape(b1, s1, d), y2.reshape(b2, s2, d))
```

```python
import functools

import jax
import jax.numpy as jnp
from jax import lax
from jax.experimental import pallas as pl
from jax.experimental.pallas import tpu as pltpu

F32 = jnp.float32
BF16 = jnp.bfloat16

D_MODEL = 2048
POOL_GROUPS = 4
POOL_HALF = (1, 2, 4, 8)
POOL_WIDTH = D_MODEL // 2
POOL_GC = POOL_WIDTH // POOL_GROUPS
HEAD_DIM = 128
N_HEADS = D_MODEL // HEAD_DIM
N_KV_HEADS = N_HEADS // 4
Q_PER_KV = N_HEADS // N_KV_HEADS
ATTN_WIDTH = N_HEADS * HEAD_DIM
KV_WIDTH = N_KV_HEADS * HEAD_DIM
WINDOW = 128
BLOCK = 128
ROT_DIM = HEAD_DIM // 4
ROT_HALF = ROT_DIM // 2
ROPE_THETA = 500000.0
IN_WIDTH = POOL_WIDTH + ATTN_WIDTH + 2 * KV_WIDTH + 2 * D_MODEL
N_GROUPS = 4
EXPERTS_PER_GROUP = 8
N_EXPERTS = N_GROUPS * EXPERTS_PER_GROUP
TOP_K = 2
D_FF = D_MODEL // 2
EPS = 1e-6
NEG = -1e30
LOG2_E = 1.4426950408889634

LANES = 128
POOL_HALO = 16
MIX_ROW_CHUNK = 256
MIB = 1024 * 1024
SLAB_ROWS = D_MODEL // LANES
SLAB_PITCH = SLAB_ROWS + 1

COL_Q = 0
COL_K = COL_Q + ATTN_WIDTH
COL_V = COL_K + KV_WIDTH
COL_P = COL_V + KV_WIDTH
COL_GP = COL_P + POOL_WIDTH
COL_GA = COL_GP + D_MODEL
QK_WIDTH = ATTN_WIDTH + KV_WIDTH
IN_TN = 512


def _params(vmem_mib, n_axes=1):
    return pltpu.CompilerParams(dimension_semantics=("arbitrary",) * n_axes,
                                vmem_limit_bytes=vmem_mib * MIB)


def _resident(shape):
    return pl.BlockSpec(shape, lambda *_: (0,) * len(shape), pipeline_mode=pl.Buffered(1))


def _tile_seq_pos(t, tile, segs):
    (t1, s1), (_, s2) = segs
    n1 = t1 // tile
    first = t < n1
    seq = jnp.where(first, s1, s2)
    pos0 = jnp.where(first, (t * tile) % s1, ((t - n1) * tile) % s2)
    return seq, pos0


def _split_specs(tile, width, n1):
    return [pl.BlockSpec((tile, width), lambda t, *_: (jnp.minimum(t, n1 - 1), 0)),
            pl.BlockSpec((tile, width), lambda t, *_: (jnp.maximum(t - n1, 0), 0))]


def _store_slabs(ref, base_row, n, val):
    for j in range(SLAB_ROWS):
        ref[pl.ds(base_row + j, n, stride=SLAB_PITCH), :] = val[:, j * LANES:(j + 1) * LANES]
    ref[pl.ds(base_row + SLAB_ROWS, n, stride=SLAB_PITCH), :] = jnp.zeros((n, LANES), F32)


def _load_slabs(ref, base_row, n):
    return [ref[pl.ds(base_row + j, n, stride=SLAB_PITCH), :] for j in range(SLAB_ROWS)]


def _in_proj_kernel(x1_ref, x2_ref, g1_ref, w_ref, hg_ref, cos_ref, sin_ref, o_ref, xn_ref, *,
                    n1, n_qk_blocks, row_chunk):
    i = pl.program_id(0)
    j = pl.program_id(1)

    def norm_rows(x_ref):
        x = x_ref[...]
        ms = jnp.mean(x * x, axis=-1, keepdims=True)
        xn_ref[...] = (x * lax.rsqrt(ms + EPS) * g1_ref[...]).astype(BF16)

    pl.when((j == 0) & (i < n1))(lambda: norm_rows(x1_ref))
    pl.when((j == 0) & (i >= n1))(lambda: norm_rows(x2_ref))

    tm, tn = o_ref.shape

    def chunk_dot(c):
        rows = slice(c * row_chunk, (c + 1) * row_chunk)
        return rows, jnp.dot(xn_ref[rows, :], w_ref[...], preferred_element_type=F32)

    @pl.when(j < n_qk_blocks)
    def _():
        lane = lax.broadcasted_iota(jnp.int32, (1, HEAD_DIM), 1)
        for c in range(tm // row_chunk):
            rows, acc = chunk_dot(c)
            cos = cos_ref[rows, :]
            sin = sin_ref[rows, :]
            for h in range(tn // HEAD_DIM):
                cols = slice(h * HEAD_DIM, (h + 1) * HEAD_DIM)
                a = acc[:, cols]
                ms = jnp.mean(a * a, axis=-1, keepdims=True)
                y = a * lax.rsqrt(ms + EPS) * hg_ref[:, cols]
                partner = jnp.where(lane < ROT_HALF,
                                    pltpu.roll(y, HEAD_DIM - ROT_HALF, 1),
                                    pltpu.roll(y, ROT_HALF, 1))
                o_ref[rows, cols] = (y * cos + partner * sin).astype(BF16)

    @pl.when(j >= n_qk_blocks)
    def _():
        for c in range(tm // row_chunk):
            rows, acc = chunk_dot(c)
            o_ref[rows, :] = acc.astype(BF16)


def _w_in_block(j):
    n_pool = POOL_WIDTH // IN_TN
    n_qkv = (ATTN_WIDTH + 2 * KV_WIDTH) // IN_TN
    return jnp.where(j < n_qkv, j + n_pool, jnp.where(j < n_qkv + n_pool, j - n_qkv, j))


def _in_proj(x1, x2, g1, w_in, head_gain, cos_t, sin_t, segs, tm):
    (t1, s1), (t2, s2) = segs
    n1 = t1 // tm
    p1 = s1 // tm
    p2 = s2 // tm
    pos_tile = lambda i, j: (jnp.where(i < n1, i % p1, (i - n1) % p2), 0)
    kern = functools.partial(_in_proj_kernel, n1=n1, n_qk_blocks=QK_WIDTH // IN_TN, row_chunk=min(256, tm))
    return pl.pallas_call(
        kern,
        grid=((t1 + t2) // tm, IN_WIDTH // IN_TN),
        in_specs=_split_specs(tm, D_MODEL, n1) + [
            pl.BlockSpec((1, D_MODEL), lambda i, j: (0, 0)),
            pl.BlockSpec((D_MODEL, IN_TN), lambda i, j: (0, _w_in_block(j))),
            pl.BlockSpec((1, IN_TN), lambda i, j: (0, jnp.minimum(j, QK_WIDTH // IN_TN - 1))),
            pl.BlockSpec((tm, HEAD_DIM), pos_tile),
            pl.BlockSpec((tm, HEAD_DIM), pos_tile),
        ],
        out_specs=pl.BlockSpec((tm, IN_TN), lambda i, j: (i, j)),
        out_shape=jax.ShapeDtypeStruct((t1 + t2, IN_WIDTH), BF16),
        scratch_shapes=[pltpu.VMEM((tm, D_MODEL), BF16)],
        compiler_params=_params(56, 2),
        name="in_proj",
    )(x1, x2, g1, w_in, head_gain, cos_t, sin_t)


def _band_attn_kernel(sink_ref, q_ref, kp_ref, km_ref, kn_ref, vp_ref, vm_ref, vn_ref, o_ref,
                      kbuf, vbuf, *, segs, tq):
    seq, pos0 = _tile_seq_pos(pl.program_id(0), tq, segs)
    kbuf[0:BLOCK, :] = kp_ref[...]
    kbuf[BLOCK:BLOCK + tq, :] = km_ref[...]
    kbuf[BLOCK + tq:, :] = kn_ref[...]
    vbuf[0:BLOCK, :] = vp_ref[...]
    vbuf[BLOCK:BLOCK + tq, :] = vm_ref[...]
    vbuf[BLOCK + tq:, :] = vn_ref[...]

    scale = HEAD_DIM ** -0.5
    c = scale * LOG2_E
    row = lax.broadcasted_iota(jnp.int32, (BLOCK, 3 * BLOCK), 0)
    col = lax.broadcasted_iota(jnp.int32, (BLOCK, 3 * BLOCK), 1)

    def q_block(qb, carry):
        r0 = pl.multiple_of(qb * BLOCK, BLOCK)
        base = pos0 + r0 - BLOCK
        lo = jnp.maximum(row, -base)
        hi = jnp.minimum(row + 2 * WINDOW, seq - 1 - base)
        valid = (col >= lo) & (col <= hi)
        for kv in range(N_KV_HEADS):
            kc = slice(kv * HEAD_DIM, (kv + 1) * HEAD_DIM)
            q4 = jnp.concatenate(
                [q_ref[pl.ds(r0, BLOCK), (kv * Q_PER_KV + g) * HEAD_DIM:(kv * Q_PER_KV + g + 1) * HEAD_DIM]
                 for g in range(Q_PER_KV)], axis=0)
            kcat = kbuf[pl.ds(r0, 3 * BLOCK), kc]
            vcat = vbuf[pl.ds(r0, 3 * BLOCK), kc]
            s4 = lax.dot_general(q4, kcat, (((1,), (1,)), ((), ())), preferred_element_type=F32)
            ps, inv = [], []
            for g in range(Q_PER_KV):
                s = jnp.where(valid, s4[g * BLOCK:(g + 1) * BLOCK], NEG)
                sink = sink_ref[0, kv * Q_PER_KV + g] * (1.0 / scale)
                m = jnp.maximum(jnp.max(s, axis=-1, keepdims=True), sink)
                p = jnp.exp2((s - m) * c)
                inv.append(1.0 / (jnp.sum(p, axis=-1, keepdims=True) + jnp.exp2((sink - m) * c)))
                ps.append(p.astype(BF16))
            o = jnp.dot(jnp.concatenate(ps, axis=0), vcat, preferred_element_type=F32)
            for g in range(Q_PER_KV):
                hc = (kv * Q_PER_KV + g) * HEAD_DIM
                o_ref[pl.ds(r0, BLOCK), hc:hc + HEAD_DIM] = (o[g * BLOCK:(g + 1) * BLOCK] * inv[g]).astype(BF16)
        return carry

    lax.fori_loop(0, tq // BLOCK, q_block, 0)


def _band_attn(u, sink, segs, tq):
    T = u.shape[0]
    nb = tq // BLOCK
    last_blk = T // BLOCK - 1
    kcol = COL_K // KV_WIDTH
    vcol = COL_V // KV_WIDTH
    prev_map = lambda c: (lambda t: (jnp.maximum(t * nb - 1, 0), c))
    next_map = lambda c: (lambda t: (jnp.minimum((t + 1) * nb, last_blk), c))
    main_map = lambda c: (lambda t: (t, c))
    kern = functools.partial(_band_attn_kernel, segs=segs, tq=tq)
    return pl.pallas_call(
        kern,
        grid=(T // tq,),
        in_specs=[
            pl.BlockSpec(memory_space=pltpu.SMEM),
            pl.BlockSpec((tq, ATTN_WIDTH), lambda t: (t, COL_Q // ATTN_WIDTH)),
            pl.BlockSpec((BLOCK, KV_WIDTH), prev_map(kcol)),
            pl.BlockSpec((tq, KV_WIDTH), main_map(kcol)),
            pl.BlockSpec((BLOCK, KV_WIDTH), next_map(kcol)),
            pl.BlockSpec((BLOCK, KV_WIDTH), prev_map(vcol)),
            pl.BlockSpec((tq, KV_WIDTH), main_map(vcol)),
            pl.BlockSpec((BLOCK, KV_WIDTH), next_map(vcol)),
        ],
        out_specs=pl.BlockSpec((tq, ATTN_WIDTH), lambda t: (t, 0)),
        out_shape=jax.ShapeDtypeStruct((T, ATTN_WIDTH), BF16),
        scratch_shapes=[pltpu.VMEM((tq + 2 * BLOCK, KV_WIDTH), BF16),
                        pltpu.VMEM((tq + 2 * BLOCK, KV_WIDTH), BF16)],
        compiler_params=_params(40),
        name="band_attn",
    )(sink, u, u, u, u, u, u, u)


def _mix_out_kernel(x1_ref, x2_ref, xpp_ref, xpm_ref, xpn_ref, at_ref, gp_ref, ga_ref,
                    pw_ref, ps_ref, pp_ref, ap_ref, wo_ref, g2_ref, rwc_ref, rwh_ref, rb_ref,
                    h_ref, hn_ref, ids_ref, wts_ref, xb, pin, *, segs, tm):
    t = pl.program_id(0)
    seq, pos0 = _tile_seq_pos(t, tm, segs)
    in_first = t < segs[0][0] // tm

    xb[0:POOL_HALO, :] = jnp.where(pos0 > 0, xpp_ref[...].astype(F32), 0.0)
    xb[POOL_HALO:POOL_HALO + tm, :] = xpm_ref[...].astype(F32)
    xb[POOL_HALO + tm:, :] = jnp.where(pos0 + tm < seq, xpn_ref[...].astype(F32), 0.0)
    rc = min(MIX_ROW_CHUNK, tm)
    for c in range(tm // rc):
        r0 = c * rc
        rows = slice(r0, r0 + rc)
        x = jnp.where(in_first, x1_ref[rows, :], x2_ref[rows, :])
        _mix_out_rows(r0, rc, pos0, seq, x, at_ref, gp_ref, ga_ref, pw_ref, ps_ref, pp_ref,
                      ap_ref, wo_ref, g2_ref, rwc_ref, rwh_ref, rb_ref, h_ref, hn_ref, ids_ref, wts_ref, xb, pin)


def _mix_out_rows(r0, rc, pos0, seq, x, at_ref, gp_ref, ga_ref, pw_ref, ps_ref, pp_ref, ap_ref, wo_ref,
                  g2_ref, rwc_ref, rwh_ref, rb_ref, h_ref, hn_ref, ids_ref, wts_ref, xb, pin):
    rows = slice(r0, r0 + rc)
    tpos = pos0 + r0 + lax.broadcasted_iota(jnp.int32, (rc, 1), 0)
    for g, half in enumerate(POOL_HALF):
        cols = slice(g * POOL_GC, (g + 1) * POOL_GC)
        c0 = POOL_HALO + r0
        win = xb[c0 - half:c0 - half + rc, cols]
        for d in range(-half + 1, half):
            win = win + xb[c0 + d:c0 + d + rc, cols]
        cnt = (jnp.minimum(tpos + half, seq) - jnp.maximum(tpos - half, 0)).astype(F32)
        mixed = (win / cnt - xb[c0:c0 + rc, cols]).astype(BF16)
        yg = jnp.dot(mixed, pw_ref[g], preferred_element_type=F32)
        pin[rows, cols] = (yg * ps_ref[:, cols]).astype(BF16)
    pool_out = jnp.dot(pin[rows, :], pp_ref[...], preferred_element_type=F32)

    attn_out = jnp.dot(at_ref[rows, :], ap_ref[...], preferred_element_type=F32)
    merged = (jax.nn.sigmoid(gp_ref[rows, :].astype(F32)) * pool_out
              + jax.nn.sigmoid(ga_ref[rows, :].astype(F32)) * attn_out)
    h = x + jnp.dot(merged.astype(BF16), wo_ref[...], preferred_element_type=F32)
    h_ref[rows, :] = h

    ms = jnp.mean(h * h, axis=-1, keepdims=True)
    hn = h * lax.rsqrt(ms + EPS) * g2_ref[...]
    _store_slabs(hn_ref, r0 * SLAB_PITCH, rc, hn)

    hi = hn.astype(BF16)
    lo = (hn - hi.astype(F32)).astype(BF16)
    both = jnp.dot(hi, rwc_ref[...], preferred_element_type=F32)
    logit = (both[:, :LANES] + both[:, LANES:]
             + jnp.dot(lo, rwh_ref[...], preferred_element_type=F32)) + rb_ref[...]

    lane_i = lax.broadcasted_iota(jnp.int32, logit.shape, 1)
    lane = lane_i.astype(F32)
    big = float(LANES)
    is_grp = (lane_i >= N_EXPERTS) & (lane_i < N_EXPERTS + N_GROUPS)
    mg = jnp.max(jnp.where(is_grp, logit, -jnp.inf), axis=-1, keepdims=True)
    grp = jnp.min(jnp.where(is_grp & (logit == mg), lane - N_EXPERTS, big), axis=-1, keepdims=True)
    p_grp = 1.0 / jnp.sum(jnp.where(is_grp, jnp.exp(logit - mg), 0.0), axis=-1, keepdims=True)

    grp_of_lane = (lane_i // EXPERTS_PER_GROUP).astype(F32)
    in_grp = (lane_i < N_EXPERTS) & (grp_of_lane == grp)
    m1 = jnp.max(jnp.where(in_grp, logit, -jnp.inf), axis=-1, keepdims=True)
    i1 = jnp.min(jnp.where(in_grp & (logit == m1), lane, big), axis=-1, keepdims=True)
    rest = in_grp & (lane != i1)
    m2 = jnp.max(jnp.where(rest, logit, -jnp.inf), axis=-1, keepdims=True)
    i2 = jnp.min(jnp.where(rest & (logit == m2), lane, big), axis=-1, keepdims=True)
    e2 = jnp.exp(m2 - m1)
    den = 1.0 + e2
    ids_ref[rows, :] = jnp.where(lane_i == 0, i1, jnp.where(lane_i == 1, i2, 0.0)).astype(jnp.int32)
    wts_ref[rows, :] = jnp.where(lane_i == 0, p_grp * (1.0 / den),
                                 jnp.where(lane_i == 1, p_grp * (e2 / den), 0.0))


def _mix_out(x1, x2, u, attn, pool_w, pool_scale, pool_proj, attn_proj, w_out, g2, rw_cat, rw_hi, rb, segs, tm):
    T = u.shape[0]
    n1 = segs[0][0] // tm
    nh = tm // POOL_HALO
    last_halo = T // POOL_HALO - 1
    pcol = COL_P // POOL_WIDTH
    kern = functools.partial(_mix_out_kernel, segs=segs, tm=tm)
    row_spec = lambda w: pl.BlockSpec((tm, w), lambda t: (t, 0))
    return pl.pallas_call(
        kern,
        grid=(T // tm,),
        in_specs=_split_specs(tm, D_MODEL, n1) + [
            pl.BlockSpec((POOL_HALO, POOL_WIDTH), lambda t: (jnp.maximum(t * nh - 1, 0), pcol)),
            pl.BlockSpec((tm, POOL_WIDTH), lambda t: (t, pcol)),
            pl.BlockSpec((POOL_HALO, POOL_WIDTH), lambda t: (jnp.minimum((t + 1) * nh, last_halo), pcol)),
            row_spec(ATTN_WIDTH),
            pl.BlockSpec((tm, D_MODEL), lambda t: (t, COL_GP // D_MODEL)),
            pl.BlockSpec((tm, D_MODEL), lambda t: (t, COL_GA // D_MODEL)),
            _resident((POOL_GROUPS, POOL_GC, POOL_GC)),
            _resident((1, POOL_WIDTH)),
            _resident((POOL_WIDTH, D_MODEL)),
            _resident((ATTN_WIDTH, D_MODEL)),
            _resident((D_MODEL, D_MODEL)),
            _resident((1, D_MODEL)),
            _resident((D_MODEL, 2 * LANES)),
            _resident((D_MODEL, LANES)),
            _resident((1, LANES)),
        ],
        out_specs=[row_spec(D_MODEL), pl.BlockSpec((tm * SLAB_PITCH, LANES), lambda t: (t, 0)),
                   row_spec(LANES), row_spec(LANES)],
        out_shape=[jax.ShapeDtypeStruct((T, D_MODEL), F32),
                   jax.ShapeDtypeStruct((T * SLAB_PITCH, LANES), F32),
                   jax.ShapeDtypeStruct((T, LANES), jnp.int32),
                   jax.ShapeDtypeStruct((T, LANES), F32)],
        scratch_shapes=[pltpu.VMEM((tm + 2 * POOL_HALO, POOL_WIDTH), F32),
                        pltpu.VMEM((tm, POOL_WIDTH), BF16)],
        compiler_params=_params(56),
        name="mix_out",
    )(x1, x2, u, u, u, attn, u, u, pool_w, pool_scale, pool_proj, attn_proj, w_out, g2, rw_cat, rw_hi, rb)


def _combine_kernel(h_ref, wts_ref, o0_ref, o1_ref, y1_ref, y2_ref, *, tk, n1):
    t = pl.program_id(0)
    w = wts_ref[...]
    w0 = w[:, 0:1]
    w1 = w[:, 1:2]
    o0 = _load_slabs(o0_ref, 0, tk)
    o1 = _load_slabs(o1_ref, 0, tk)

    def write(y_ref):
        for j in range(SLAB_ROWS):
            cols = slice(j * LANES, (j + 1) * LANES)
            y_ref[:, cols] = h_ref[:, cols] + (w0 * o0[j] + w1 * o1[j])

    pl.when(t < n1)(lambda: write(y1_ref))
    pl.when(t >= n1)(lambda: write(y2_ref))


def _combine(h, wts, o_slabs, segs, tk):
    T = h.shape[0]
    nt = T // tk
    n1 = segs[0][0] // tk
    return pl.pallas_call(
        functools.partial(_combine_kernel, tk=tk, n1=n1),
        grid=(nt,),
        in_specs=[
            pl.BlockSpec((tk, D_MODEL), lambda t: (t, 0)),
            pl.BlockSpec((tk, LANES), lambda t: (t, 0)),
            pl.BlockSpec((tk * SLAB_PITCH, LANES), lambda t: (t, 0)),
            pl.BlockSpec((tk * SLAB_PITCH, LANES), lambda t: (nt + t, 0)),
        ],
        out_specs=_split_specs(tk, D_MODEL, n1),
        out_shape=[jax.ShapeDtypeStruct((segs[0][0], D_MODEL), F32),
                   jax.ShapeDtypeStruct((segs[1][0], D_MODEL), F32)],
        compiler_params=_params(40),
        name="combine",
    )(h, wts, o_slabs, o_slabs)


def _moe_ffn_kernel(be_ref, nu_ref, g0_ref, gnext_ref, sprev_ref, scur_ref, hn_ref, wg_ref, wu_ref, wd_ref,
                    o_ref, xbuf, obuf, sem_g, sem_s, *, tb, dummy0):
    del be_ref
    b = pl.program_id(0)
    n_used = nu_ref[0]
    slot = b % 2
    half = tb * SLAB_PITCH
    mine = pl.multiple_of(slot * half, 8)
    other = pl.multiple_of((1 - slot) * half, 8)

    def slab_copy(src, src_row, dst, dst_row, sem):
        return pltpu.make_async_copy(src.at[pl.ds(src_row, SLAB_PITCH)], dst.at[pl.ds(dst_row, SLAB_PITCH)], sem)

    def half_copy(src, dst, dst_row, sem):
        return pltpu.make_async_copy(src.at[pl.ds(0, half)], dst.at[pl.ds(dst_row, half)], sem)

    def start_gathers(tok_ref, base, sem):
        for r in range(tb):
            slab_copy(hn_ref, tok_ref[0, 0, r] * SLAB_PITCH, xbuf, base + r * SLAB_PITCH, sem).start(priority=r % 2)

    def start_scatters(dst_of_row, base, sem):
        for r in range(tb):
            slab_copy(obuf, base + r * SLAB_PITCH, o_ref, dst_of_row(r) * SLAB_PITCH, sem).start(priority=r % 2)

    @pl.when(b == 0)
    def _():
        obuf[...] = jnp.zeros_like(obuf)
        start_gathers(g0_ref, 0, sem_g.at[0])
        half_copy(obuf, o_ref, dummy0 * SLAB_PITCH, sem_s.at[0]).start()

    @pl.when(b < n_used)
    def _():
        half_copy(hn_ref, xbuf, mine, sem_g.at[slot]).wait()
        half_copy(obuf, o_ref, 0, sem_s.at[slot]).wait()
        start_gathers(gnext_ref, other, sem_g.at[1 - slot])
        first = b == 0
        start_scatters(lambda r: jnp.where(first, dummy0 + tb + r, sprev_ref[0, 0, r]), other, sem_s.at[1 - slot])

        x = jnp.concatenate([p.astype(BF16) for p in _load_slabs(xbuf, mine, tb)], axis=1)
        gate = jnp.dot(x, wg_ref[0], preferred_element_type=F32)
        up = jnp.dot(x, wu_ref[0], preferred_element_type=F32)
        mid = (jax.nn.silu(gate) * up).astype(BF16)
        out = jnp.dot(mid, wd_ref[0], preferred_element_type=F32)
        _store_slabs(obuf, mine, tb, out)

        @pl.when(b == n_used - 1)
        def _():
            half_copy(hn_ref, xbuf, other, sem_g.at[1 - slot]).wait()
            start_scatters(lambda r: scur_ref[0, 0, r], mine, sem_s.at[slot])
            half_copy(obuf, o_ref, 0, sem_s.at[1 - slot]).wait()
            half_copy(obuf, o_ref, 0, sem_s.at[slot]).wait()


def _moe_ffn(hn_slabs, plan, w_gate, w_up, w_down, n_tokens, tb):
    nblk = plan["gather_tok"].shape[0]
    dummy0 = TOP_K * n_tokens
    n_slabs = dummy0 + 2 * tb
    exp_map = lambda i, be, nu: (be[i], 0, 0)
    idx_spec = lambda f: pl.BlockSpec((1, 1, tb), lambda i, be, nu: (f(i), 0, 0), memory_space=pltpu.SMEM)
    grid_spec = pltpu.PrefetchScalarGridSpec(
        num_scalar_prefetch=2,
        grid=(nblk,),
        in_specs=[
            idx_spec(lambda i: 0),
            idx_spec(lambda i: jnp.minimum(i + 1, nblk - 1)),
            idx_spec(lambda i: jnp.maximum(i - 1, 0)),
            idx_spec(lambda i: i),
            pl.BlockSpec(memory_space=pl.ANY),
            pl.BlockSpec((1, D_MODEL, D_FF), exp_map),
            pl.BlockSpec((1, D_MODEL, D_FF), exp_map),
            pl.BlockSpec((1, D_FF, D_MODEL), exp_map),
        ],
        out_specs=pl.BlockSpec(memory_space=pl.ANY),
        scratch_shapes=[pltpu.VMEM((2 * tb * SLAB_PITCH, LANES), F32),
                        pltpu.VMEM((2 * tb * SLAB_PITCH, LANES), F32),
                        pltpu.SemaphoreType.DMA((2,)),
                        pltpu.SemaphoreType.DMA((2,))],
    )
    return pl.pallas_call(
        functools.partial(_moe_ffn_kernel, tb=tb, dummy0=dummy0),
        grid_spec=grid_spec,
        out_shape=jax.ShapeDtypeStruct((n_slabs * SLAB_PITCH, LANES), F32),
        compiler_params=_params(56),
        name="moe_ffn",
    )(plan["blk_exp"], plan["n_used"], plan["gather_tok"], plan["gather_tok"], plan["scatter_dst"],
      plan["scatter_dst"], hn_slabs, w_gate, w_up, w_down)


def _slot_plan(ids, tb):
    T = ids.shape[0]
    M = T * TOP_K
    i32 = jnp.int32
    e = ids[:, :TOP_K].reshape(M)
    skey = jnp.sort(e * M + jnp.arange(M, dtype=i32))
    sm = skey % M
    edges = jnp.arange(N_EXPERTS + 1, dtype=i32) * M
    bounds = jnp.sum((skey[None, :] < edges[:, None]).astype(i32), axis=1)
    start = bounds[:-1]
    cnt = bounds[1:] - start
    nb = (cnt + tb - 1) // tb
    blk_end = jnp.cumsum(nb)
    n_used = blk_end[-1]
    nblk = M // tb + N_EXPERTS
    b = jnp.arange(nblk, dtype=i32)
    be = jnp.minimum(jnp.sum((blk_end[None, :] <= b[:, None]).astype(i32), axis=1), N_EXPERTS - 1)
    j = b - (blk_end[be] - nb[be])
    row0 = start[be] + j * tb
    n_valid = jnp.where(b < n_used, jnp.clip(cnt[be] - j * tb, 0, tb), 0)
    r = jnp.arange(tb, dtype=i32)[None, :]
    valid = r < n_valid[:, None]
    m = sm[jnp.minimum(row0[:, None] + r, M - 1)]
    tok = m // TOP_K
    k = m % TOP_K
    spare = TOP_K * T + (b % 2)[:, None] * tb + r
    blk_exp = jnp.where(b < n_used, be, be[jnp.maximum(n_used - 1, 0)])
    return dict(blk_exp=blk_exp.astype(i32), n_used=n_used.astype(i32).reshape(1),
                gather_tok=jnp.where(valid, tok, 0).astype(i32).reshape(nblk, 1, tb),
                scatter_dst=jnp.where(valid, k * T + tok, spare).astype(i32).reshape(nblk, 1, tb))


def _rope_tables(seq):
    inv = ROPE_THETA ** (-jnp.arange(ROT_HALF, dtype=F32) / ROT_HALF)
    ang = jnp.arange(seq, dtype=jnp.int32).astype(F32)[:, None] * inv[None, :]
    cos = jnp.cos(ang)
    sin = jnp.sin(ang)
    ones = jnp.ones((seq, HEAD_DIM - ROT_DIM), F32)
    cos_t = jnp.concatenate([cos, cos, ones], axis=1)
    sin_t = jnp.concatenate([-sin, sin, 0.0 * ones], axis=1)
    return cos_t, sin_t


def kernel(x_prompt, x_sample, norm1_g, w_in, pool_w, pool_scale, pool_proj, q_norm_g, k_norm_g, sink,
           attn_proj, w_out, norm2_g, router_group_w, router_group_b, router_expert_w, router_expert_b,
           w_gate, w_up, w_down):
    assert norm1_g.shape[0] == 1, "single-layer trunk"
    (b1, s1, d), (b2, s2, _) = x_prompt.shape, x_sample.shape
    t1, t2 = b1 * s1, b2 * s2
    segs = ((t1, s1), (t2, s2))
    s_min = min(s1, s2)
    x1 = x_prompt.reshape(t1, d)
    x2 = x_sample.reshape(t2, d)

    head_gain = jnp.concatenate([jnp.tile(q_norm_g[0], N_HEADS), jnp.tile(k_norm_g[0], N_KV_HEADS)])[None, :]
    pad = LANES - N_EXPERTS - N_GROUPS
    rw = jnp.concatenate([router_expert_w[0], router_group_w[0], jnp.zeros((D_MODEL, pad), F32)], axis=1)
    rw_hi = rw.astype(BF16)
    rw_lo = (rw - rw_hi.astype(F32)).astype(BF16)
    rb = jnp.concatenate([router_expert_b[0], router_group_b[0], jnp.zeros((pad,), F32)])[None, :]
    cos_t, sin_t = _rope_tables(max(s1, s2))

    u = _in_proj(x1, x2, norm1_g[0][None, :], w_in[0].astype(BF16), head_gain, cos_t, sin_t, segs,
                 tm=min(1024, s_min))
    attn = _band_attn(u, sink[0][None, :], segs, tq=min(512, s_min))
    h, hn, ids, wts = _mix_out(x1, x2, u, attn, pool_w[0].astype(BF16), pool_scale[0][None, :],
                               pool_proj[0].astype(BF16), attn_proj[0].astype(BF16), w_out[0].astype(BF16),
                               norm2_g[0][None, :], jnp.concatenate([rw_hi, rw_lo], axis=1), rw_hi, rb,
                               segs, tm=min(256, s_min))
    tb = 256
    plan = _slot_plan(ids, tb)
    o_slabs = _moe_ffn(hn, plan, w_gate[0].astype(BF16), w_up[0].astype(BF16), w_down[0].astype(BF16),
                       t1 + t2, tb)
    y1, y2 = _combine(h, wts, o_slabs, segs, tk=min(256, s_min))
    return (y1.reshape(b1, s1, d), y2.reshape(b2, s2, d))
```

```python
import functools

import jax
import jax.numpy as jnp
from jax import lax
from jax.experimental import pallas as pl
from jax.experimental.pallas import tpu as pltpu

F32 = jnp.float32
BF16 = jnp.bfloat16

D_MODEL = 2048
POOL_GROUPS = 4
POOL_HALF = (1, 2, 4, 8)
POOL_WIDTH = D_MODEL // 2
POOL_GC = POOL_WIDTH // POOL_GROUPS
HEAD_DIM = 128
N_HEADS = D_MODEL // HEAD_DIM
N_KV_HEADS = N_HEADS // 4
Q_PER_KV = N_HEADS // N_KV_HEADS
ATTN_WIDTH = N_HEADS * HEAD_DIM
KV_WIDTH = N_KV_HEADS * HEAD_DIM
WINDOW = 128
BLOCK = 128
ROT_DIM = HEAD_DIM // 4
ROT_HALF = ROT_DIM // 2
ROPE_THETA = 500000.0
IN_WIDTH = POOL_WIDTH + ATTN_WIDTH + 2 * KV_WIDTH + 2 * D_MODEL
N_GROUPS = 4
EXPERTS_PER_GROUP = 8
N_EXPERTS = N_GROUPS * EXPERTS_PER_GROUP
TOP_K = 2
D_FF = D_MODEL // 2
EPS = 1e-6
NEG = -1e30
LOG2_E = 1.4426950408889634

LANES = 128
POOL_HALO = 16
MIX_ROW_CHUNK = 256
WEIGHT_CAST_ROWS = 128
MIB = 1024 * 1024
SLAB_ROWS = D_MODEL // LANES
SLAB_PITCH = SLAB_ROWS + 1

COL_Q = 0
COL_K = COL_Q + ATTN_WIDTH
COL_V = COL_K + KV_WIDTH
COL_P = COL_V + KV_WIDTH
COL_GP = COL_P + POOL_WIDTH
COL_GA = COL_GP + D_MODEL
QK_WIDTH = ATTN_WIDTH + KV_WIDTH
IN_TN = 512


def _params(vmem_mib, n_axes=1):
    return pltpu.CompilerParams(dimension_semantics=("arbitrary",) * n_axes,
                                vmem_limit_bytes=vmem_mib * MIB)


def _resident(shape):
    return pl.BlockSpec(shape, lambda *_: (0,) * len(shape), pipeline_mode=pl.Buffered(1))


def _tile_seq_pos(t, tile, segs):
    (t1, s1), (_, s2) = segs
    n1 = t1 // tile
    first = t < n1
    seq = jnp.where(first, s1, s2)
    pos0 = jnp.where(first, (t * tile) % s1, ((t - n1) * tile) % s2)
    return seq, pos0


def _split_specs(tile, width, n1):
    return [pl.BlockSpec((tile, width), lambda t, *_: (jnp.minimum(t, n1 - 1), 0)),
            pl.BlockSpec((tile, width), lambda t, *_: (jnp.maximum(t - n1, 0), 0))]


def _store_slabs(ref, base_row, n, val):
    for j in range(SLAB_ROWS):
        ref[pl.ds(base_row + j, n, stride=SLAB_PITCH), :] = val[:, j * LANES:(j + 1) * LANES]
    ref[pl.ds(base_row + SLAB_ROWS, n, stride=SLAB_PITCH), :] = jnp.zeros((n, LANES), F32)


def _load_slabs(ref, base_row, n):
    return [ref[pl.ds(base_row + j, n, stride=SLAB_PITCH), :] for j in range(SLAB_ROWS)]


def _in_proj_kernel(x1_ref, x2_ref, g1_ref, w_ref, hg_ref, cos_ref, sin_ref, o_ref, xn_ref, *,
                    n1, n_qk_blocks, row_chunk):
    i = pl.program_id(0)
    j = pl.program_id(1)

    def norm_rows(x_ref):
        x = x_ref[...]
        ms = jnp.mean(x * x, axis=-1, keepdims=True)
        xn_ref[...] = (x * lax.rsqrt(ms + EPS) * g1_ref[...]).astype(BF16)

    pl.when((j == 0) & (i < n1))(lambda: norm_rows(x1_ref))
    pl.when((j == 0) & (i >= n1))(lambda: norm_rows(x2_ref))

    tm, tn = o_ref.shape

    def chunk_dot(c):
        rows = slice(c * row_chunk, (c + 1) * row_chunk)
        return rows, jnp.dot(xn_ref[rows, :], w_ref[...], preferred_element_type=F32)

    @pl.when(j < n_qk_blocks)
    def _():
        lane = lax.broadcasted_iota(jnp.int32, (1, HEAD_DIM), 1)
        for c in range(tm // row_chunk):
            rows, acc = chunk_dot(c)
            cos = cos_ref[rows, :]
            sin = sin_ref[rows, :]
            for h in range(tn // HEAD_DIM):
                cols = slice(h * HEAD_DIM, (h + 1) * HEAD_DIM)
                a = acc[:, cols]
                ms = jnp.mean(a * a, axis=-1, keepdims=True)
                y = a * lax.rsqrt(ms + EPS) * hg_ref[:, cols]
                partner = jnp.where(lane < ROT_HALF,
                                    pltpu.roll(y, HEAD_DIM - ROT_HALF, 1),
                                    pltpu.roll(y, ROT_HALF, 1))
                o_ref[rows, cols] = (y * cos + partner * sin).astype(BF16)

    @pl.when(j >= n_qk_blocks)
    def _():
        for c in range(tm // row_chunk):
            rows, acc = chunk_dot(c)
            o_ref[rows, :] = acc.astype(BF16)


def _w_in_block(j):
    n_pool = POOL_WIDTH // IN_TN
    n_qkv = (ATTN_WIDTH + 2 * KV_WIDTH) // IN_TN
    return jnp.where(j < n_qkv, j + n_pool, jnp.where(j < n_qkv + n_pool, j - n_qkv, j))


def _in_proj(x1, x2, g1, w_in, head_gain, cos_t, sin_t, segs, tm):
    (t1, s1), (t2, s2) = segs
    n1 = t1 // tm
    p1 = s1 // tm
    p2 = s2 // tm
    pos_tile = lambda i, j: (jnp.where(i < n1, i % p1, (i - n1) % p2), 0)
    kern = functools.partial(_in_proj_kernel, n1=n1, n_qk_blocks=QK_WIDTH // IN_TN, row_chunk=min(256, tm))
    return pl.pallas_call(
        kern,
        grid=((t1 + t2) // tm, IN_WIDTH // IN_TN),
        in_specs=_split_specs(tm, D_MODEL, n1) + [
            pl.BlockSpec((1, D_MODEL), lambda i, j: (0, 0)),
            pl.BlockSpec((D_MODEL, IN_TN), lambda i, j: (0, _w_in_block(j))),
            pl.BlockSpec((1, IN_TN), lambda i, j: (0, jnp.minimum(j, QK_WIDTH // IN_TN - 1))),
            pl.BlockSpec((tm, HEAD_DIM), pos_tile),
            pl.BlockSpec((tm, HEAD_DIM), pos_tile),
        ],
        out_specs=pl.BlockSpec((tm, IN_TN), lambda i, j: (i, j)),
        out_shape=jax.ShapeDtypeStruct((t1 + t2, IN_WIDTH), BF16),
        scratch_shapes=[pltpu.VMEM((tm, D_MODEL), BF16)],
        compiler_params=_params(56, 2),
        name="in_proj",
    )(x1, x2, g1, w_in, head_gain, cos_t, sin_t)


def _band_attn_kernel(sink_ref, q_ref, kp_ref, km_ref, kn_ref, vp_ref, vm_ref, vn_ref, o_ref,
                      kbuf, vbuf, *, segs, tq):
    seq, pos0 = _tile_seq_pos(pl.program_id(0), tq, segs)
    kbuf[0:BLOCK, :] = kp_ref[...]
    kbuf[BLOCK:BLOCK + tq, :] = km_ref[...]
    kbuf[BLOCK + tq:, :] = kn_ref[...]
    vbuf[0:BLOCK, :] = vp_ref[...]
    vbuf[BLOCK:BLOCK + tq, :] = vm_ref[...]
    vbuf[BLOCK + tq:, :] = vn_ref[...]

    scale = HEAD_DIM ** -0.5
    c = scale * LOG2_E
    row = lax.broadcasted_iota(jnp.int32, (BLOCK, 3 * BLOCK), 0)
    col = lax.broadcasted_iota(jnp.int32, (BLOCK, 3 * BLOCK), 1)

    def q_block(qb, carry):
        r0 = pl.multiple_of(qb * BLOCK, BLOCK)
        base = pos0 + r0 - BLOCK
        lo = jnp.maximum(row, -base)
        hi = jnp.minimum(row + 2 * WINDOW, seq - 1 - base)
        valid = (col >= lo) & (col <= hi)
        for kv in range(N_KV_HEADS):
            kc = slice(kv * HEAD_DIM, (kv + 1) * HEAD_DIM)
            q4 = jnp.concatenate(
                [q_ref[pl.ds(r0, BLOCK), (kv * Q_PER_KV + g) * HEAD_DIM:(kv * Q_PER_KV + g + 1) * HEAD_DIM]
                 for g in range(Q_PER_KV)], axis=0)
            kcat = kbuf[pl.ds(r0, 3 * BLOCK), kc]
            vcat = vbuf[pl.ds(r0, 3 * BLOCK), kc]
            s4 = lax.dot_general(q4, kcat, (((1,), (1,)), ((), ())), preferred_element_type=F32)
            ps, inv = [], []
            for g in range(Q_PER_KV):
                s = jnp.where(valid, s4[g * BLOCK:(g + 1) * BLOCK], NEG)
                sink = sink_ref[0, kv * Q_PER_KV + g] * (1.0 / scale)
                m = jnp.maximum(jnp.max(s, axis=-1, keepdims=True), sink)
                p = jnp.exp2((s - m) * c)
                inv.append(1.0 / (jnp.sum(p, axis=-1, keepdims=True) + jnp.exp2((sink - m) * c)))
                ps.append(p.astype(BF16))
            o = jnp.dot(jnp.concatenate(ps, axis=0), vcat, preferred_element_type=F32)
            for g in range(Q_PER_KV):
                hc = (kv * Q_PER_KV + g) * HEAD_DIM
                o_ref[pl.ds(r0, BLOCK), hc:hc + HEAD_DIM] = (o[g * BLOCK:(g + 1) * BLOCK] * inv[g]).astype(BF16)
        return carry

    lax.fori_loop(0, tq // BLOCK, q_block, 0)


def _band_attn(u, sink, segs, tq):
    T = u.shape[0]
    nb = tq // BLOCK
    last_blk = T // BLOCK - 1
    kcol = COL_K // KV_WIDTH
    vcol = COL_V // KV_WIDTH
    prev_map = lambda c: (lambda t: (jnp.maximum(t * nb - 1, 0), c))
    next_map = lambda c: (lambda t: (jnp.minimum((t + 1) * nb, last_blk), c))
    main_map = lambda c: (lambda t: (t, c))
    kern = functools.partial(_band_attn_kernel, segs=segs, tq=tq)
    return pl.pallas_call(
        kern,
        grid=(T // tq,),
        in_specs=[
            pl.BlockSpec(memory_space=pltpu.SMEM),
            pl.BlockSpec((tq, ATTN_WIDTH), lambda t: (t, COL_Q // ATTN_WIDTH)),
            pl.BlockSpec((BLOCK, KV_WIDTH), prev_map(kcol)),
            pl.BlockSpec((tq, KV_WIDTH), main_map(kcol)),
            pl.BlockSpec((BLOCK, KV_WIDTH), next_map(kcol)),
            pl.BlockSpec((BLOCK, KV_WIDTH), prev_map(vcol)),
            pl.BlockSpec((tq, KV_WIDTH), main_map(vcol)),
            pl.BlockSpec((BLOCK, KV_WIDTH), next_map(vcol)),
        ],
        out_specs=pl.BlockSpec((tq, ATTN_WIDTH), lambda t: (t, 0)),
        out_shape=jax.ShapeDtypeStruct((T, ATTN_WIDTH), BF16),
        scratch_shapes=[pltpu.VMEM((tq + 2 * BLOCK, KV_WIDTH), BF16),
                        pltpu.VMEM((tq + 2 * BLOCK, KV_WIDTH), BF16)],
        compiler_params=_params(40),
        name="band_attn",
    )(sink, u, u, u, u, u, u, u)


def _mix_out_kernel(x1_ref, x2_ref, xpp_ref, xpm_ref, xpn_ref, at_ref, gp_ref, ga_ref,
                    pw_ref, ps_ref, pp_ref, ap_ref, wo_ref, g2_ref, rwc_ref, rwh_ref, rb_ref,
                    h_ref, hn_ref, ids_ref, wts_ref, xb, pin, *, segs, tm):
    t = pl.program_id(0)
    seq, pos0 = _tile_seq_pos(t, tm, segs)
    in_first = t < segs[0][0] // tm

    xb[0:POOL_HALO, :] = jnp.where(pos0 > 0, xpp_ref[...].astype(F32), 0.0)
    xb[POOL_HALO:POOL_HALO + tm, :] = xpm_ref[...].astype(F32)
    xb[POOL_HALO + tm:, :] = jnp.where(pos0 + tm < seq, xpn_ref[...].astype(F32), 0.0)
    rc = min(MIX_ROW_CHUNK, tm)
    for c in range(tm // rc):
        r0 = c * rc
        rows = slice(r0, r0 + rc)
        x = jnp.where(in_first, x1_ref[rows, :], x2_ref[rows, :])
        _mix_out_rows(r0, rc, pos0, seq, x, at_ref, gp_ref, ga_ref, pw_ref, ps_ref, pp_ref,
                      ap_ref, wo_ref, g2_ref, rwc_ref, rwh_ref, rb_ref, h_ref, hn_ref, ids_ref, wts_ref, xb, pin)


def _mix_out_rows(r0, rc, pos0, seq, x, at_ref, gp_ref, ga_ref, pw_ref, ps_ref, pp_ref, ap_ref, wo_ref,
                  g2_ref, rwc_ref, rwh_ref, rb_ref, h_ref, hn_ref, ids_ref, wts_ref, xb, pin):
    rows = slice(r0, r0 + rc)
    tpos = pos0 + r0 + lax.broadcasted_iota(jnp.int32, (rc, 1), 0)
    for g, half in enumerate(POOL_HALF):
        cols = slice(g * POOL_GC, (g + 1) * POOL_GC)
        c0 = POOL_HALO + r0
        win = xb[c0 - half:c0 - half + rc, cols]
        for d in range(-half + 1, half):
            win = win + xb[c0 + d:c0 + d + rc, cols]
        cnt = (jnp.minimum(tpos + half, seq) - jnp.maximum(tpos - half, 0)).astype(F32)
        mixed = (win / cnt - xb[c0:c0 + rc, cols]).astype(BF16)
        yg = jnp.dot(mixed, pw_ref[g], preferred_element_type=F32)
        pin[rows, cols] = (yg * ps_ref[:, cols]).astype(BF16)
    pool_out = jnp.dot(pin[rows, :], pp_ref[...], preferred_element_type=F32)

    attn_out = jnp.dot(at_ref[rows, :], ap_ref[...], preferred_element_type=F32)
    merged = (jax.nn.sigmoid(gp_ref[rows, :].astype(F32)) * pool_out
              + jax.nn.sigmoid(ga_ref[rows, :].astype(F32)) * attn_out)
    h = x + jnp.dot(merged.astype(BF16), wo_ref[...], preferred_element_type=F32)
    h_ref[rows, :] = h

    ms = jnp.mean(h * h, axis=-1, keepdims=True)
    hn = h * lax.rsqrt(ms + EPS) * g2_ref[...]
    _store_slabs(hn_ref, r0 * SLAB_PITCH, rc, hn)

    hi = hn.astype(BF16)
    lo = (hn - hi.astype(F32)).astype(BF16)
    both = jnp.dot(hi, rwc_ref[...], preferred_element_type=F32)
    logit = (both[:, :LANES] + both[:, LANES:]
             + jnp.dot(lo, rwh_ref[...], preferred_element_type=F32)) + rb_ref[...]

    lane_i = lax.broadcasted_iota(jnp.int32, logit.shape, 1)
    lane = lane_i.astype(F32)
    big = float(LANES)
    is_grp = (lane_i >= N_EXPERTS) & (lane_i < N_EXPERTS + N_GROUPS)
    mg = jnp.max(jnp.where(is_grp, logit, -jnp.inf), axis=-1, keepdims=True)
    grp = jnp.min(jnp.where(is_grp & (logit == mg), lane - N_EXPERTS, big), axis=-1, keepdims=True)
    p_grp = 1.0 / jnp.sum(jnp.where(is_grp, jnp.exp(logit - mg), 0.0), axis=-1, keepdims=True)

    grp_of_lane = (lane_i // EXPERTS_PER_GROUP).astype(F32)
    in_grp = (lane_i < N_EXPERTS) & (grp_of_lane == grp)
    m1 = jnp.max(jnp.where(in_grp, logit, -jnp.inf), axis=-1, keepdims=True)
    i1 = jnp.min(jnp.where(in_grp & (logit == m1), lane, big), axis=-1, keepdims=True)
    rest = in_grp & (lane != i1)
    m2 = jnp.max(jnp.where(rest, logit, -jnp.inf), axis=-1, keepdims=True)
    i2 = jnp.min(jnp.where(rest & (logit == m2), lane, big), axis=-1, keepdims=True)
    e2 = jnp.exp(m2 - m1)
    den = 1.0 + e2
    ids_ref[rows, :] = jnp.where(lane_i == 0, i1, jnp.where(lane_i == 1, i2, 0.0)).astype(jnp.int32)
    wts_ref[rows, :] = jnp.where(lane_i == 0, p_grp * (1.0 / den),
                                 jnp.where(lane_i == 1, p_grp * (e2 / den), 0.0))


def _mix_out(x1, x2, u, attn, pool_w, pool_scale, pool_proj, attn_proj, w_out, g2, rw_cat, rw_hi, rb, segs, tm):
    T = u.shape[0]
    n1 = segs[0][0] // tm
    nh = tm // POOL_HALO
    last_halo = T // POOL_HALO - 1
    pcol = COL_P // POOL_WIDTH
    kern = functools.partial(_mix_out_kernel, segs=segs, tm=tm)
    row_spec = lambda w: pl.BlockSpec((tm, w), lambda t: (t, 0))
    return pl.pallas_call(
        kern,
        grid=(T // tm,),
        in_specs=_split_specs(tm, D_MODEL, n1) + [
            pl.BlockSpec((POOL_HALO, POOL_WIDTH), lambda t: (jnp.maximum(t * nh - 1, 0), pcol)),
            pl.BlockSpec((tm, POOL_WIDTH), lambda t: (t, pcol)),
            pl.BlockSpec((POOL_HALO, POOL_WIDTH), lambda t: (jnp.minimum((t + 1) * nh, last_halo), pcol)),
            row_spec(ATTN_WIDTH),
            pl.BlockSpec((tm, D_MODEL), lambda t: (t, COL_GP // D_MODEL)),
            pl.BlockSpec((tm, D_MODEL), lambda t: (t, COL_GA // D_MODEL)),
            _resident((POOL_GROUPS, POOL_GC, POOL_GC)),
            _resident((1, POOL_WIDTH)),
            _resident((POOL_WIDTH, D_MODEL)),
            _resident((ATTN_WIDTH, D_MODEL)),
            _resident((D_MODEL, D_MODEL)),
            _resident((1, D_MODEL)),
            _resident((D_MODEL, 2 * LANES)),
            _resident((D_MODEL, LANES)),
            _resident((1, LANES)),
        ],
        out_specs=[row_spec(D_MODEL), pl.BlockSpec((tm * SLAB_PITCH, LANES), lambda t: (t, 0)),
                   row_spec(LANES), row_spec(LANES)],
        out_shape=[jax.ShapeDtypeStruct((T, D_MODEL), F32),
                   jax.ShapeDtypeStruct((T * SLAB_PITCH, LANES), F32),
                   jax.ShapeDtypeStruct((T, LANES), jnp.int32),
                   jax.ShapeDtypeStruct((T, LANES), F32)],
        scratch_shapes=[pltpu.VMEM((tm + 2 * POOL_HALO, POOL_WIDTH), F32),
                        pltpu.VMEM((tm, POOL_WIDTH), BF16)],
        compiler_params=_params(56),
        name="mix_out",
    )(x1, x2, u, u, u, attn, u, u, pool_w, pool_scale, pool_proj, attn_proj, w_out, g2, rw_cat, rw_hi, rb)


def _combine_kernel(h_ref, wts_ref, o0_ref, o1_ref, y1_ref, y2_ref, *, tk, n1):
    t = pl.program_id(0)
    w = wts_ref[...]
    w0 = w[:, 0:1]
    w1 = w[:, 1:2]
    o0 = _load_slabs(o0_ref, 0, tk)
    o1 = _load_slabs(o1_ref, 0, tk)

    def write(y_ref):
        for j in range(SLAB_ROWS):
            cols = slice(j * LANES, (j + 1) * LANES)
            y_ref[:, cols] = h_ref[:, cols] + (w0 * o0[j] + w1 * o1[j])

    pl.when(t < n1)(lambda: write(y1_ref))
    pl.when(t >= n1)(lambda: write(y2_ref))


def _combine(h, wts, o_slabs, segs, tk):
    T = h.shape[0]
    nt = T // tk
    n1 = segs[0][0] // tk
    return pl.pallas_call(
        functools.partial(_combine_kernel, tk=tk, n1=n1),
        grid=(nt,),
        in_specs=[
            pl.BlockSpec((tk, D_MODEL), lambda t: (t, 0)),
            pl.BlockSpec((tk, LANES), lambda t: (t, 0)),
            pl.BlockSpec((tk * SLAB_PITCH, LANES), lambda t: (t, 0)),
            pl.BlockSpec((tk * SLAB_PITCH, LANES), lambda t: (nt + t, 0)),
        ],
        out_specs=_split_specs(tk, D_MODEL, n1),
        out_shape=[jax.ShapeDtypeStruct((segs[0][0], D_MODEL), F32),
                   jax.ShapeDtypeStruct((segs[1][0], D_MODEL), F32)],
        compiler_params=_params(40),
        name="combine",
    )(h, wts, o_slabs, o_slabs)


def _moe_ffn_kernel(be_ref, nu_ref, nx_ref, g0_ref, gnext_ref, sprev_ref, scur_ref, hn_ref,
                    wg_hbm, wu_hbm, wd_hbm, o_ref, xbuf, obuf, wg_f32, wu_f32, wd_f32, wg_ref, wu_ref, wd_ref,
                    sem_g, sem_s, sem_w, *, tb, dummy0):
    b = pl.program_id(0)
    n_used = nu_ref[0]
    expert = be_ref[b]

    def weight_copies(e):
        return [pltpu.make_async_copy(src.at[e], dst, sem_w)
                for src, dst in ((wg_hbm, wg_f32), (wu_hbm, wu_f32), (wd_hbm, wd_f32))]

    @pl.when(b == 0)
    def _():
        for cp in weight_copies(expert):
            cp.start()

    @pl.when((b < n_used) & ((b == 0) | (expert != be_ref[jnp.maximum(b - 1, 0)])))
    def _():
        for cp in weight_copies(expert):
            cp.wait()
        for f32_ref, bf16_ref in ((wg_f32, wg_ref), (wu_f32, wu_ref), (wd_f32, wd_ref)):
            n_chunks = f32_ref.shape[0] // WEIGHT_CAST_ROWS

            def cast_chunk(ci, carry, f32_ref=f32_ref, bf16_ref=bf16_ref):
                rows = pl.ds(pl.multiple_of(ci * WEIGHT_CAST_ROWS, WEIGHT_CAST_ROWS), WEIGHT_CAST_ROWS)
                bf16_ref[rows, :] = f32_ref[rows, :].astype(BF16)
                return carry

            lax.fori_loop(0, n_chunks, cast_chunk, 0)
        nxt = nx_ref[b]

        @pl.when(nxt >= 0)
        def _():
            for cp in weight_copies(nxt):
                cp.start()
    slot = b % 2
    half = tb * SLAB_PITCH
    mine = pl.multiple_of(slot * half, 8)
    other = pl.multiple_of((1 - slot) * half, 8)

    def slab_copy(src, src_row, dst, dst_row, sem):
        return pltpu.make_async_copy(src.at[pl.ds(src_row, SLAB_PITCH)], dst.at[pl.ds(dst_row, SLAB_PITCH)], sem)

    def half_copy(src, dst, dst_row, sem):
        return pltpu.make_async_copy(src.at[pl.ds(0, half)], dst.at[pl.ds(dst_row, half)], sem)

    def start_gathers(tok_ref, base, sem):
        for r in range(tb):
            slab_copy(hn_ref, tok_ref[0, 0, r] * SLAB_PITCH, xbuf, base + r * SLAB_PITCH, sem).start(priority=r % 2)

    def start_scatters(dst_of_row, base, sem):
        for r in range(tb):
            slab_copy(obuf, base + r * SLAB_PITCH, o_ref, dst_of_row(r) * SLAB_PITCH, sem).start(priority=r % 2)

    @pl.when(b == 0)
    def _():
        obuf[...] = jnp.zeros_like(obuf)
        start_gathers(g0_ref, 0, sem_g.at[0])
        half_copy(obuf, o_ref, dummy0 * SLAB_PITCH, sem_s.at[0]).start()

    @pl.when(b < n_used)
    def _():
        half_copy(hn_ref, xbuf, mine, sem_g.at[slot]).wait()
        half_copy(obuf, o_ref, 0, sem_s.at[slot]).wait()
        start_gathers(gnext_ref, other, sem_g.at[1 - slot])
        first = b == 0
        start_scatters(lambda r: jnp.where(first, dummy0 + tb + r, sprev_ref[0, 0, r]), other, sem_s.at[1 - slot])

        x = jnp.concatenate([p.astype(BF16) for p in _load_slabs(xbuf, mine, tb)], axis=1)
        gate = jnp.dot(x, wg_ref[...], preferred_element_type=F32)
        up = jnp.dot(x, wu_ref[...], preferred_element_type=F32)
        mid = (jax.nn.silu(gate) * up).astype(BF16)
        out = jnp.dot(mid, wd_ref[...], preferred_element_type=F32)
        _store_slabs(obuf, mine, tb, out)

        @pl.when(b == n_used - 1)
        def _():
            half_copy(hn_ref, xbuf, other, sem_g.at[1 - slot]).wait()
            start_scatters(lambda r: scur_ref[0, 0, r], mine, sem_s.at[slot])
            half_copy(obuf, o_ref, 0, sem_s.at[1 - slot]).wait()
            half_copy(obuf, o_ref, 0, sem_s.at[slot]).wait()


def _moe_ffn(hn_slabs, plan, w_gate, w_up, w_down, n_tokens, tb):
    nblk = plan["gather_tok"].shape[0]
    dummy0 = TOP_K * n_tokens
    n_slabs = dummy0 + 2 * tb
    idx_spec = lambda f: pl.BlockSpec((1, 1, tb), lambda i, *_: (f(i), 0, 0), memory_space=pltpu.SMEM)
    hbm = pl.BlockSpec(memory_space=pl.ANY)
    grid_spec = pltpu.PrefetchScalarGridSpec(
        num_scalar_prefetch=3,
        grid=(nblk,),
        in_specs=[
            idx_spec(lambda i: 0),
            idx_spec(lambda i: jnp.minimum(i + 1, nblk - 1)),
            idx_spec(lambda i: jnp.maximum(i - 1, 0)),
            idx_spec(lambda i: i),
            hbm, hbm, hbm, hbm,
        ],
        out_specs=hbm,
        scratch_shapes=[pltpu.VMEM((2 * tb * SLAB_PITCH, LANES), F32),
                        pltpu.VMEM((2 * tb * SLAB_PITCH, LANES), F32),
                        pltpu.VMEM((D_MODEL, D_FF), F32),
                        pltpu.VMEM((D_MODEL, D_FF), F32),
                        pltpu.VMEM((D_FF, D_MODEL), F32),
                        pltpu.VMEM((D_MODEL, D_FF), BF16),
                        pltpu.VMEM((D_MODEL, D_FF), BF16),
                        pltpu.VMEM((D_FF, D_MODEL), BF16),
                        pltpu.SemaphoreType.DMA((2,)),
                        pltpu.SemaphoreType.DMA((2,)),
                        pltpu.SemaphoreType.DMA(())],
    )
    return pl.pallas_call(
        functools.partial(_moe_ffn_kernel, tb=tb, dummy0=dummy0),
        grid_spec=grid_spec,
        out_shape=jax.ShapeDtypeStruct((n_slabs * SLAB_PITCH, LANES), F32),
        compiler_params=_params(58),
        name="moe_ffn",
    )(plan["blk_exp"], plan["n_used"], plan["next_exp"], plan["gather_tok"], plan["gather_tok"], plan["scatter_dst"],
      plan["scatter_dst"], hn_slabs, w_gate, w_up, w_down)


def _slot_plan(ids, tb):
    T = ids.shape[0]
    M = T * TOP_K
    i32 = jnp.int32
    e = ids[:, :TOP_K].reshape(M)
    skey = jnp.sort(e * M + jnp.arange(M, dtype=i32))
    sm = skey % M
    edges = jnp.arange(N_EXPERTS + 1, dtype=i32) * M
    bounds = jnp.sum((skey[None, :] < edges[:, None]).astype(i32), axis=1)
    start = bounds[:-1]
    cnt = bounds[1:] - start
    nb = (cnt + tb - 1) // tb
    blk_end = jnp.cumsum(nb)
    n_used = blk_end[-1]
    nblk = M // tb + N_EXPERTS
    b = jnp.arange(nblk, dtype=i32)
    be = jnp.minimum(jnp.sum((blk_end[None, :] <= b[:, None]).astype(i32), axis=1), N_EXPERTS - 1)
    j = b - (blk_end[be] - nb[be])
    row0 = start[be] + j * tb
    n_valid = jnp.where(b < n_used, jnp.clip(cnt[be] - j * tb, 0, tb), 0)
    r = jnp.arange(tb, dtype=i32)[None, :]
    valid = r < n_valid[:, None]
    m = sm[jnp.minimum(row0[:, None] + r, M - 1)]
    tok = m // TOP_K
    k = m % TOP_K
    spare = TOP_K * T + (b % 2)[:, None] * tb + r
    blk_exp = jnp.where(b < n_used, be, be[jnp.maximum(n_used - 1, 0)])
    ex = jnp.arange(N_EXPERTS, dtype=i32)
    later = (ex[None, :] > ex[:, None]) & (cnt[None, :] > 0)
    next_of = jnp.min(jnp.where(later, ex[None, :], N_EXPERTS), axis=1)
    next_of = jnp.where(next_of < N_EXPERTS, next_of, -1)
    return dict(blk_exp=blk_exp.astype(i32), n_used=n_used.astype(i32).reshape(1),
                next_exp=next_of[blk_exp].astype(i32),
                gather_tok=jnp.where(valid, tok, 0).astype(i32).reshape(nblk, 1, tb),
                scatter_dst=jnp.where(valid, k * T + tok, spare).astype(i32).reshape(nblk, 1, tb))


def _rope_tables(seq):
    inv = ROPE_THETA ** (-jnp.arange(ROT_HALF, dtype=F32) / ROT_HALF)
    ang = jnp.arange(seq, dtype=jnp.int32).astype(F32)[:, None] * inv[None, :]
    cos = jnp.cos(ang)
    sin = jnp.sin(ang)
    ones = jnp.ones((seq, HEAD_DIM - ROT_DIM), F32)
    cos_t = jnp.concatenate([cos, cos, ones], axis=1)
    sin_t = jnp.concatenate([-sin, sin, 0.0 * ones], axis=1)
    return cos_t, sin_t


def kernel(x_prompt, x_sample, norm1_g, w_in, pool_w, pool_scale, pool_proj, q_norm_g, k_norm_g, sink,
           attn_proj, w_out, norm2_g, router_group_w, router_group_b, router_expert_w, router_expert_b,
           w_gate, w_up, w_down):
    assert norm1_g.shape[0] == 1, "single-layer trunk"
    (b1, s1, d), (b2, s2, _) = x_prompt.shape, x_sample.shape
    t1, t2 = b1 * s1, b2 * s2
    segs = ((t1, s1), (t2, s2))
    s_min = min(s1, s2)
    x1 = x_prompt.reshape(t1, d)
    x2 = x_sample.reshape(t2, d)

    head_gain = jnp.concatenate([jnp.tile(q_norm_g[0], N_HEADS), jnp.tile(k_norm_g[0], N_KV_HEADS)])[None, :]
    pad = LANES - N_EXPERTS - N_GROUPS
    rw = jnp.concatenate([router_expert_w[0], router_group_w[0], jnp.zeros((D_MODEL, pad), F32)], axis=1)
    rw_hi = rw.astype(BF16)
    rw_lo = (rw - rw_hi.astype(F32)).astype(BF16)
    rb = jnp.concatenate([router_expert_b[0], router_group_b[0], jnp.zeros((pad,), F32)])[None, :]
    cos_t, sin_t = _rope_tables(max(s1, s2))

    u = _in_proj(x1, x2, norm1_g[0][None, :], w_in[0].astype(BF16), head_gain, cos_t, sin_t, segs,
                 tm=min(1024, s_min))
    attn = _band_attn(u, sink[0][None, :], segs, tq=min(512, s_min))
    h, hn, ids, wts = _mix_out(x1, x2, u, attn, pool_w[0].astype(BF16), pool_scale[0][None, :],
                               pool_proj[0].astype(BF16), attn_proj[0].astype(BF16), w_out[0].astype(BF16),
                               norm2_g[0][None, :], jnp.concatenate([rw_hi, rw_lo], axis=1), rw_hi, rb,
                               segs, tm=min(256, s_min))
    tb = 256
    plan = _slot_plan(ids, tb)
    o_slabs = _moe_ffn(hn, plan, w_gate[0], w_up[0], w_down[0], t1 + t2, tb)
    y1, y2 = _combine(h, wts, o_slabs, segs, tk=min(256, s_min))
    return (y1.reshape(b1, s1, d), y2.reshape(b2, s2, d))
```

```python
import functools

import jax
import jax.numpy as jnp
from jax import lax
from jax.experimental import pallas as pl
from jax.experimental.pallas import tpu as pltpu

F32 = jnp.float32
BF16 = jnp.bfloat16

D_MODEL = 2048
POOL_GROUPS = 4
POOL_HALF = (1, 2, 4, 8)
POOL_WIDTH = D_MODEL // 2
POOL_GC = POOL_WIDTH // POOL_GROUPS
HEAD_DIM = 128
N_HEADS = D_MODEL // HEAD_DIM
N_KV_HEADS = N_HEADS // 4
Q_PER_KV = N_HEADS // N_KV_HEADS
ATTN_WIDTH = N_HEADS * HEAD_DIM
KV_WIDTH = N_KV_HEADS * HEAD_DIM
WINDOW = 128
BLOCK = 128
ROT_DIM = HEAD_DIM // 4
ROT_HALF = ROT_DIM // 2
ROPE_THETA = 500000.0
IN_WIDTH = POOL_WIDTH + ATTN_WIDTH + 2 * KV_WIDTH + 2 * D_MODEL
N_GROUPS = 4
EXPERTS_PER_GROUP = 8
N_EXPERTS = N_GROUPS * EXPERTS_PER_GROUP
TOP_K = 2
D_FF = D_MODEL // 2
EPS = 1e-6
NEG = -1e30
LOG2_E = 1.4426950408889634

LANES = 128
POOL_HALO = 16
MIX_ROW_CHUNK = 256
WEIGHT_CAST_ROWS = 128
MIB = 1024 * 1024
COL_PIECES = D_MODEL // LANES
SLAB_ROWS = COL_PIECES // 2
SLAB_PITCH = SLAB_ROWS + 1
U32 = jnp.uint32

COL_Q = 0
COL_K = COL_Q + ATTN_WIDTH
COL_V = COL_K + KV_WIDTH
COL_P = COL_V + KV_WIDTH
COL_GP = COL_P + POOL_WIDTH
COL_GA = COL_GP + D_MODEL
QK_WIDTH = ATTN_WIDTH + KV_WIDTH
IN_TN = 512


def _params(vmem_mib, n_axes=1):
    return pltpu.CompilerParams(dimension_semantics=("arbitrary",) * n_axes,
                                vmem_limit_bytes=vmem_mib * MIB)


def _resident(shape):
    return pl.BlockSpec(shape, lambda *_: (0,) * len(shape), pipeline_mode=pl.Buffered(1))


def _tile_seq_pos(t, tile, segs):
    (t1, s1), (_, s2) = segs
    n1 = t1 // tile
    first = t < n1
    seq = jnp.where(first, s1, s2)
    pos0 = jnp.where(first, (t * tile) % s1, ((t - n1) * tile) % s2)
    return seq, pos0


def _split_specs(tile, width, n1):
    return [pl.BlockSpec((tile, width), lambda t, *_: (jnp.minimum(t, n1 - 1), 0)),
            pl.BlockSpec((tile, width), lambda t, *_: (jnp.maximum(t - n1, 0), 0))]


def _bf16_bits(x):
    return lax.bitcast_convert_type(x.astype(BF16).astype(F32), U32)


def _store_slabs(ref, base_row, n, val):
    for j in range(SLAB_ROWS):
        first = val[:, j * LANES:(j + 1) * LANES]
        second = val[:, (SLAB_ROWS + j) * LANES:(SLAB_ROWS + j + 1) * LANES]
        ref[pl.ds(base_row + j, n, stride=SLAB_PITCH), :] = _bf16_bits(first) | (_bf16_bits(second) >> 16)
    ref[pl.ds(base_row + SLAB_ROWS, n, stride=SLAB_PITCH), :] = jnp.zeros((n, LANES), U32)


def _load_slabs(ref, base_row, n):
    words = [ref[pl.ds(base_row + j, n, stride=SLAB_PITCH), :] for j in range(SLAB_ROWS)]
    first = [lax.bitcast_convert_type(w & jnp.uint32(0xFFFF0000), F32) for w in words]
    second = [lax.bitcast_convert_type(w << 16, F32) for w in words]
    return first + second


def _in_proj_kernel(x1_ref, x2_ref, g1_ref, w_ref, hg_ref, cos_ref, sin_ref, o_ref, xn_ref, *,
                    n1, n_qk_blocks, row_chunk):
    i = pl.program_id(0)
    j = pl.program_id(1)

    def norm_rows(x_ref):
        x = x_ref[...]
        ms = jnp.mean(x * x, axis=-1, keepdims=True)
        xn_ref[...] = (x * lax.rsqrt(ms + EPS) * g1_ref[...]).astype(BF16)

    pl.when((j == 0) & (i < n1))(lambda: norm_rows(x1_ref))
    pl.when((j == 0) & (i >= n1))(lambda: norm_rows(x2_ref))

    tm, tn = o_ref.shape

    def chunk_dot(c):
        rows = slice(c * row_chunk, (c + 1) * row_chunk)
        return rows, jnp.dot(xn_ref[rows, :], w_ref[...], preferred_element_type=F32)

    @pl.when(j < n_qk_blocks)
    def _():
        lane = lax.broadcasted_iota(jnp.int32, (1, HEAD_DIM), 1)
        for c in range(tm // row_chunk):
            rows, acc = chunk_dot(c)
            cos = cos_ref[rows, :]
            sin = sin_ref[rows, :]
            for h in range(tn // HEAD_DIM):
                cols = slice(h * HEAD_DIM, (h + 1) * HEAD_DIM)
                a = acc[:, cols]
                ms = jnp.mean(a * a, axis=-1, keepdims=True)
                y = a * lax.rsqrt(ms + EPS) * hg_ref[:, cols]
                partner = jnp.where(lane < ROT_HALF,
                                    pltpu.roll(y, HEAD_DIM - ROT_HALF, 1),
                                    pltpu.roll(y, ROT_HALF, 1))
                o_ref[rows, cols] = (y * cos + partner * sin).astype(BF16)

    @pl.when(j >= n_qk_blocks)
    def _():
        for c in range(tm // row_chunk):
            rows, acc = chunk_dot(c)
            o_ref[rows, :] = acc.astype(BF16)


def _w_in_block(j):
    n_pool = POOL_WIDTH // IN_TN
    n_qkv = (ATTN_WIDTH + 2 * KV_WIDTH) // IN_TN
    return jnp.where(j < n_qkv, j + n_pool, jnp.where(j < n_qkv + n_pool, j - n_qkv, j))


def _in_proj(x1, x2, g1, w_in, head_gain, cos_t, sin_t, segs, tm):
    (t1, s1), (t2, s2) = segs
    n1 = t1 // tm
    p1 = s1 // tm
    p2 = s2 // tm
    pos_tile = lambda i, j: (jnp.where(i < n1, i % p1, (i - n1) % p2), 0)
    kern = functools.partial(_in_proj_kernel, n1=n1, n_qk_blocks=QK_WIDTH // IN_TN, row_chunk=min(256, tm))
    return pl.pallas_call(
        kern,
        grid=((t1 + t2) // tm, IN_WIDTH // IN_TN),
        in_specs=_split_specs(tm, D_MODEL, n1) + [
            pl.BlockSpec((1, D_MODEL), lambda i, j: (0, 0)),
            pl.BlockSpec((D_MODEL, IN_TN), lambda i, j: (0, _w_in_block(j))),
            pl.BlockSpec((1, IN_TN), lambda i, j: (0, jnp.minimum(j, QK_WIDTH // IN_TN - 1))),
            pl.BlockSpec((tm, HEAD_DIM), pos_tile),
            pl.BlockSpec((tm, HEAD_DIM), pos_tile),
        ],
        out_specs=pl.BlockSpec((tm, IN_TN), lambda i, j: (i, j)),
        out_shape=jax.ShapeDtypeStruct((t1 + t2, IN_WIDTH), BF16),
        scratch_shapes=[pltpu.VMEM((tm, D_MODEL), BF16)],
        compiler_params=_params(56, 2),
        name="in_proj",
    )(x1, x2, g1, w_in, head_gain, cos_t, sin_t)


def _band_attn_kernel(sink_ref, q_ref, kp_ref, km_ref, kn_ref, vp_ref, vm_ref, vn_ref, o_ref,
                      kbuf, vbuf, *, segs, tq):
    seq, pos0 = _tile_seq_pos(pl.program_id(0), tq, segs)
    kbuf[0:BLOCK, :] = kp_ref[...]
    kbuf[BLOCK:BLOCK + tq, :] = km_ref[...]
    kbuf[BLOCK + tq:, :] = kn_ref[...]
    vbuf[0:BLOCK, :] = vp_ref[...]
    vbuf[BLOCK:BLOCK + tq, :] = vm_ref[...]
    vbuf[BLOCK + tq:, :] = vn_ref[...]

    scale = HEAD_DIM ** -0.5
    c = scale * LOG2_E
    row = lax.broadcasted_iota(jnp.int32, (BLOCK, 3 * BLOCK), 0)
    col = lax.broadcasted_iota(jnp.int32, (BLOCK, 3 * BLOCK), 1)

    def q_block(qb, carry):
        r0 = pl.multiple_of(qb * BLOCK, BLOCK)
        base = pos0 + r0 - BLOCK
        lo = jnp.maximum(row, -base)
        hi = jnp.minimum(row + 2 * WINDOW, seq - 1 - base)
        valid = (col >= lo) & (col <= hi)
        for kv in range(N_KV_HEADS):
            kc = slice(kv * HEAD_DIM, (kv + 1) * HEAD_DIM)
            q4 = jnp.concatenate(
                [q_ref[pl.ds(r0, BLOCK), (kv * Q_PER_KV + g) * HEAD_DIM:(kv * Q_PER_KV + g + 1) * HEAD_DIM]
                 for g in range(Q_PER_KV)], axis=0)
            kcat = kbuf[pl.ds(r0, 3 * BLOCK), kc]
            vcat = vbuf[pl.ds(r0, 3 * BLOCK), kc]
            s4 = lax.dot_general(q4, kcat, (((1,), (1,)), ((), ())), preferred_element_type=F32)
            ps, inv = [], []
            for g in range(Q_PER_KV):
                s = jnp.where(valid, s4[g * BLOCK:(g + 1) * BLOCK], NEG)
                sink = sink_ref[0, kv * Q_PER_KV + g] * (1.0 / scale)
                m = jnp.maximum(jnp.max(s, axis=-1, keepdims=True), sink)
                p = jnp.exp2((s - m) * c)
                inv.append(1.0 / (jnp.sum(p, axis=-1, keepdims=True) + jnp.exp2((sink - m) * c)))
                ps.append(p.astype(BF16))
            o = jnp.dot(jnp.concatenate(ps, axis=0), vcat, preferred_element_type=F32)
            for g in range(Q_PER_KV):
                hc = (kv * Q_PER_KV + g) * HEAD_DIM
                o_ref[pl.ds(r0, BLOCK), hc:hc + HEAD_DIM] = (o[g * BLOCK:(g + 1) * BLOCK] * inv[g]).astype(BF16)
        return carry

    lax.fori_loop(0, tq // BLOCK, q_block, 0)


def _band_attn(u, sink, segs, tq):
    T = u.shape[0]
    nb = tq // BLOCK
    last_blk = T // BLOCK - 1
    kcol = COL_K // KV_WIDTH
    vcol = COL_V // KV_WIDTH
    prev_map = lambda c: (lambda t: (jnp.maximum(t * nb - 1, 0), c))
    next_map = lambda c: (lambda t: (jnp.minimum((t + 1) * nb, last_blk), c))
    main_map = lambda c: (lambda t: (t, c))
    kern = functools.partial(_band_attn_kernel, segs=segs, tq=tq)
    return pl.pallas_call(
        kern,
        grid=(T // tq,),
        in_specs=[
            pl.BlockSpec(memory_space=pltpu.SMEM),
            pl.BlockSpec((tq, ATTN_WIDTH), lambda t: (t, COL_Q // ATTN_WIDTH)),
            pl.BlockSpec((BLOCK, KV_WIDTH), prev_map(kcol)),
            pl.BlockSpec((tq, KV_WIDTH), main_map(kcol)),
            pl.BlockSpec((BLOCK, KV_WIDTH), next_map(kcol)),
            pl.BlockSpec((BLOCK, KV_WIDTH), prev_map(vcol)),
            pl.BlockSpec((tq, KV_WIDTH), main_map(vcol)),
            pl.BlockSpec((BLOCK, KV_WIDTH), next_map(vcol)),
        ],
        out_specs=pl.BlockSpec((tq, ATTN_WIDTH), lambda t: (t, 0)),
        out_shape=jax.ShapeDtypeStruct((T, ATTN_WIDTH), BF16),
        scratch_shapes=[pltpu.VMEM((tq + 2 * BLOCK, KV_WIDTH), BF16),
                        pltpu.VMEM((tq + 2 * BLOCK, KV_WIDTH), BF16)],
        compiler_params=_params(40),
        name="band_attn",
    )(sink, u, u, u, u, u, u, u)


def _mix_out_kernel(x1_ref, x2_ref, xpp_ref, xpm_ref, xpn_ref, at_ref, gp_ref, ga_ref,
                    pw_ref, ps_ref, pp_ref, ap_ref, wo_ref, g2_ref, rwc_ref, rwh_ref, rb_ref,
                    h_ref, hn_ref, ids_ref, wts_ref, xb, pin, *, segs, tm):
    t = pl.program_id(0)
    seq, pos0 = _tile_seq_pos(t, tm, segs)
    in_first = t < segs[0][0] // tm

    xb[0:POOL_HALO, :] = jnp.where(pos0 > 0, xpp_ref[...].astype(F32), 0.0)
    xb[POOL_HALO:POOL_HALO + tm, :] = xpm_ref[...].astype(F32)
    xb[POOL_HALO + tm:, :] = jnp.where(pos0 + tm < seq, xpn_ref[...].astype(F32), 0.0)
    rc = min(MIX_ROW_CHUNK, tm)
    for c in range(tm // rc):
        r0 = c * rc
        rows = slice(r0, r0 + rc)
        x = jnp.where(in_first, x1_ref[rows, :], x2_ref[rows, :])
        _mix_out_rows(r0, rc, pos0, seq, x, at_ref, gp_ref, ga_ref, pw_ref, ps_ref, pp_ref,
                      ap_ref, wo_ref, g2_ref, rwc_ref, rwh_ref, rb_ref, h_ref, hn_ref, ids_ref, wts_ref, xb, pin)


def _mix_out_rows(r0, rc, pos0, seq, x, at_ref, gp_ref, ga_ref, pw_ref, ps_ref, pp_ref, ap_ref, wo_ref,
                  g2_ref, rwc_ref, rwh_ref, rb_ref, h_ref, hn_ref, ids_ref, wts_ref, xb, pin):
    rows = slice(r0, r0 + rc)
    tpos = pos0 + r0 + lax.broadcasted_iota(jnp.int32, (rc, 1), 0)
    for g, half in enumerate(POOL_HALF):
        cols = slice(g * POOL_GC, (g + 1) * POOL_GC)
        c0 = POOL_HALO + r0
        win = xb[c0 - half:c0 - half + rc, cols]
        for d in range(-half + 1, half):
            win = win + xb[c0 + d:c0 + d + rc, cols]
        cnt = (jnp.minimum(tpos + half, seq) - jnp.maximum(tpos - half, 0)).astype(F32)
        mixed = (win / cnt - xb[c0:c0 + rc, cols]).astype(BF16)
        yg = jnp.dot(mixed, pw_ref[g], preferred_element_type=F32)
        pin[rows, cols] = (yg * ps_ref[:, cols]).astype(BF16)
    pool_out = jnp.dot(pin[rows, :], pp_ref[...], preferred_element_type=F32)

    attn_out = jnp.dot(at_ref[rows, :], ap_ref[...], preferred_element_type=F32)
    merged = (jax.nn.sigmoid(gp_ref[rows, :].astype(F32)) * pool_out
              + jax.nn.sigmoid(ga_ref[rows, :].astype(F32)) * attn_out)
    h = x + jnp.dot(merged.astype(BF16), wo_ref[...], preferred_element_type=F32)
    h_ref[rows, :] = h

    ms = jnp.mean(h * h, axis=-1, keepdims=True)
    hn = h * lax.rsqrt(ms + EPS) * g2_ref[...]
    _store_slabs(hn_ref, r0 * SLAB_PITCH, rc, hn)

    hi = hn.astype(BF16)
    lo = (hn - hi.astype(F32)).astype(BF16)
    both = jnp.dot(hi, rwc_ref[...], preferred_element_type=F32)
    logit = (both[:, :LANES] + both[:, LANES:]
             + jnp.dot(lo, rwh_ref[...], preferred_element_type=F32)) + rb_ref[...]

    lane_i = lax.broadcasted_iota(jnp.int32, logit.shape, 1)
    lane = lane_i.astype(F32)
    big = float(LANES)
    is_grp = (lane_i >= N_EXPERTS) & (lane_i < N_EXPERTS + N_GROUPS)
    mg = jnp.max(jnp.where(is_grp, logit, -jnp.inf), axis=-1, keepdims=True)
    grp = jnp.min(jnp.where(is_grp & (logit == mg), lane - N_EXPERTS, big), axis=-1, keepdims=True)
    p_grp = 1.0 / jnp.sum(jnp.where(is_grp, jnp.exp(logit - mg), 0.0), axis=-1, keepdims=True)

    grp_of_lane = (lane_i // EXPERTS_PER_GROUP).astype(F32)
    in_grp = (lane_i < N_EXPERTS) & (grp_of_lane == grp)
    m1 = jnp.max(jnp.where(in_grp, logit, -jnp.inf), axis=-1, keepdims=True)
    i1 = jnp.min(jnp.where(in_grp & (logit == m1), lane, big), axis=-1, keepdims=True)
    rest = in_grp & (lane != i1)
    m2 = jnp.max(jnp.where(rest, logit, -jnp.inf), axis=-1, keepdims=True)
    i2 = jnp.min(jnp.where(rest & (logit == m2), lane, big), axis=-1, keepdims=True)
    e2 = jnp.exp(m2 - m1)
    den = 1.0 + e2
    ids_ref[rows, :] = jnp.where(lane_i == 0, i1, jnp.where(lane_i == 1, i2, 0.0)).astype(jnp.int32)
    wts_ref[rows, :] = jnp.where(lane_i == 0, p_grp * (1.0 / den),
                                 jnp.where(lane_i == 1, p_grp * (e2 / den), 0.0))


def _mix_out(x1, x2, u, attn, pool_w, pool_scale, pool_proj, attn_proj, w_out, g2, rw_cat, rw_hi, rb, segs, tm):
    T = u.shape[0]
    n1 = segs[0][0] // tm
    nh = tm // POOL_HALO
    last_halo = T // POOL_HALO - 1
    pcol = COL_P // POOL_WIDTH
    kern = functools.partial(_mix_out_kernel, segs=segs, tm=tm)
    row_spec = lambda w: pl.BlockSpec((tm, w), lambda t: (t, 0))
    return pl.pallas_call(
        kern,
        grid=(T // tm,),
        in_specs=_split_specs(tm, D_MODEL, n1) + [
            pl.BlockSpec((POOL_HALO, POOL_WIDTH), lambda t: (jnp.maximum(t * nh - 1, 0), pcol)),
            pl.BlockSpec((tm, POOL_WIDTH), lambda t: (t, pcol)),
            pl.BlockSpec((POOL_HALO, POOL_WIDTH), lambda t: (jnp.minimum((t + 1) * nh, last_halo), pcol)),
            row_spec(ATTN_WIDTH),
            pl.BlockSpec((tm, D_MODEL), lambda t: (t, COL_GP // D_MODEL)),
            pl.BlockSpec((tm, D_MODEL), lambda t: (t, COL_GA // D_MODEL)),
            _resident((POOL_GROUPS, POOL_GC, POOL_GC)),
            _resident((1, POOL_WIDTH)),
            _resident((POOL_WIDTH, D_MODEL)),
            _resident((ATTN_WIDTH, D_MODEL)),
            _resident((D_MODEL, D_MODEL)),
            _resident((1, D_MODEL)),
            _resident((D_MODEL, 2 * LANES)),
            _resident((D_MODEL, LANES)),
            _resident((1, LANES)),
        ],
        out_specs=[row_spec(D_MODEL), pl.BlockSpec((tm * SLAB_PITCH, LANES), lambda t: (t, 0)),
                   row_spec(LANES), row_spec(LANES)],
        out_shape=[jax.ShapeDtypeStruct((T, D_MODEL), F32),
                   jax.ShapeDtypeStruct((T * SLAB_PITCH, LANES), U32),
                   jax.ShapeDtypeStruct((T, LANES), jnp.int32),
                   jax.ShapeDtypeStruct((T, LANES), F32)],
        scratch_shapes=[pltpu.VMEM((tm + 2 * POOL_HALO, POOL_WIDTH), F32),
                        pltpu.VMEM((tm, POOL_WIDTH), BF16)],
        compiler_params=_params(56),
        name="mix_out",
    )(x1, x2, u, u, u, attn, u, u, pool_w, pool_scale, pool_proj, attn_proj, w_out, g2, rw_cat, rw_hi, rb)


def _combine_kernel(h_ref, wts_ref, o0_ref, o1_ref, y1_ref, y2_ref, *, tk, n1):
    t = pl.program_id(0)
    w = wts_ref[...]
    w0 = w[:, 0:1]
    w1 = w[:, 1:2]
    o0 = _load_slabs(o0_ref, 0, tk)
    o1 = _load_slabs(o1_ref, 0, tk)

    def write(y_ref):
        for j in range(COL_PIECES):
            cols = slice(j * LANES, (j + 1) * LANES)
            y_ref[:, cols] = h_ref[:, cols] + (w0 * o0[j] + w1 * o1[j])

    pl.when(t < n1)(lambda: write(y1_ref))
    pl.when(t >= n1)(lambda: write(y2_ref))


def _combine(h, wts, o_slabs, segs, tk):
    T = h.shape[0]
    nt = T // tk
    n1 = segs[0][0] // tk
    return pl.pallas_call(
        functools.partial(_combine_kernel, tk=tk, n1=n1),
        grid=(nt,),
        in_specs=[
            pl.BlockSpec((tk, D_MODEL), lambda t: (t, 0)),
            pl.BlockSpec((tk, LANES), lambda t: (t, 0)),
            pl.BlockSpec((tk * SLAB_PITCH, LANES), lambda t: (t, 0)),
            pl.BlockSpec((tk * SLAB_PITCH, LANES), lambda t: (nt + t, 0)),
        ],
        out_specs=_split_specs(tk, D_MODEL, n1),
        out_shape=[jax.ShapeDtypeStruct((segs[0][0], D_MODEL), F32),
                   jax.ShapeDtypeStruct((segs[1][0], D_MODEL), F32)],
        compiler_params=_params(40),
        name="combine",
    )(h, wts, o_slabs, o_slabs)


def _moe_ffn_kernel(be_ref, nu_ref, nx_ref, g0_ref, gnext_ref, sprev_ref, scur_ref, hn_ref,
                    wg_hbm, wu_hbm, wd_hbm, o_ref, xbuf, obuf, wg_f32, wu_f32, wd_f32, wg_ref, wu_ref, wd_ref,
                    sem_g, sem_s, sem_w, *, tb, dummy0):
    b = pl.program_id(0)
    n_used = nu_ref[0]
    expert = be_ref[b]

    def weight_copies(e):
        return [pltpu.make_async_copy(src.at[e], dst, sem_w)
                for src, dst in ((wg_hbm, wg_f32), (wu_hbm, wu_f32), (wd_hbm, wd_f32))]

    @pl.when(b == 0)
    def _():
        for cp in weight_copies(expert):
            cp.start()

    @pl.when((b < n_used) & ((b == 0) | (expert != be_ref[jnp.maximum(b - 1, 0)])))
    def _():
        for cp in weight_copies(expert):
            cp.wait()
        for f32_ref, bf16_ref in ((wg_f32, wg_ref), (wu_f32, wu_ref), (wd_f32, wd_ref)):
            n_chunks = f32_ref.shape[0] // WEIGHT_CAST_ROWS

            def cast_chunk(ci, carry, f32_ref=f32_ref, bf16_ref=bf16_ref):
                rows = pl.ds(pl.multiple_of(ci * WEIGHT_CAST_ROWS, WEIGHT_CAST_ROWS), WEIGHT_CAST_ROWS)
                bf16_ref[rows, :] = f32_ref[rows, :].astype(BF16)
                return carry

            lax.fori_loop(0, n_chunks, cast_chunk, 0)
        nxt = nx_ref[b]

        @pl.when(nxt >= 0)
        def _():
            for cp in weight_copies(nxt):
                cp.start()
    slot = b % 2
    half = tb * SLAB_PITCH
    mine = pl.multiple_of(slot * half, 8)
    other = pl.multiple_of((1 - slot) * half, 8)

    def slab_copy(src, src_row, dst, dst_row, sem):
        return pltpu.make_async_copy(src.at[pl.ds(src_row, SLAB_PITCH)], dst.at[pl.ds(dst_row, SLAB_PITCH)], sem)

    def half_copy(src, dst, dst_row, sem):
        return pltpu.make_async_copy(src.at[pl.ds(0, half)], dst.at[pl.ds(dst_row, half)], sem)

    def start_gathers(tok_ref, base, sem):
        for r in range(tb):
            slab_copy(hn_ref, tok_ref[0, 0, r] * SLAB_PITCH, xbuf, base + r * SLAB_PITCH, sem).start(priority=r % 2)

    def start_scatters(dst_of_row, base, sem):
        for r in range(tb):
            slab_copy(obuf, base + r * SLAB_PITCH, o_ref, dst_of_row(r) * SLAB_PITCH, sem).start(priority=r % 2)

    @pl.when(b == 0)
    def _():
        obuf[...] = jnp.zeros_like(obuf)
        start_gathers(g0_ref, 0, sem_g.at[0])
        half_copy(obuf, o_ref, dummy0 * SLAB_PITCH, sem_s.at[0]).start()

    @pl.when(b < n_used)
    def _():
        half_copy(hn_ref, xbuf, mine, sem_g.at[slot]).wait()
        half_copy(obuf, o_ref, 0, sem_s.at[slot]).wait()
        start_gathers(gnext_ref, other, sem_g.at[1 - slot])
        first = b == 0
        start_scatters(lambda r: jnp.where(first, dummy0 + tb + r, sprev_ref[0, 0, r]), other, sem_s.at[1 - slot])

        x = jnp.concatenate([p.astype(BF16) for p in _load_slabs(xbuf, mine, tb)], axis=1)
        gate = jnp.dot(x, wg_ref[...], preferred_element_type=F32)
        up = jnp.dot(x, wu_ref[...], preferred_element_type=F32)
        mid = (jax.nn.silu(gate) * up).astype(BF16)
        out = jnp.dot(mid, wd_ref[...], preferred_element_type=F32)
        _store_slabs(obuf, mine, tb, out)

        @pl.when(b == n_used - 1)
        def _():
            half_copy(hn_ref, xbuf, other, sem_g.at[1 - slot]).wait()
            start_scatters(lambda r: scur_ref[0, 0, r], mine, sem_s.at[slot])
            half_copy(obuf, o_ref, 0, sem_s.at[1 - slot]).wait()
            half_copy(obuf, o_ref, 0, sem_s.at[slot]).wait()


def _moe_ffn(hn_slabs, plan, w_gate, w_up, w_down, n_tokens, tb):
    nblk = plan["gather_tok"].shape[0]
    dummy0 = TOP_K * n_tokens
    n_slabs = dummy0 + 2 * tb
    idx_spec = lambda f: pl.BlockSpec((1, 1, tb), lambda i, *_: (f(i), 0, 0), memory_space=pltpu.SMEM)
    hbm = pl.BlockSpec(memory_space=pl.ANY)
    grid_spec = pltpu.PrefetchScalarGridSpec(
        num_scalar_prefetch=3,
        grid=(nblk,),
        in_specs=[
            idx_spec(lambda i: 0),
            idx_spec(lambda i: jnp.minimum(i + 1, nblk - 1)),
            idx_spec(lambda i: jnp.maximum(i - 1, 0)),
            idx_spec(lambda i: i),
            hbm, hbm, hbm, hbm,
        ],
        out_specs=hbm,
        scratch_shapes=[pltpu.VMEM((2 * tb * SLAB_PITCH, LANES), U32),
                        pltpu.VMEM((2 * tb * SLAB_PITCH, LANES), U32),
                        pltpu.VMEM((D_MODEL, D_FF), F32),
                        pltpu.VMEM((D_MODEL, D_FF), F32),
                        pltpu.VMEM((D_FF, D_MODEL), F32),
                        pltpu.VMEM((D_MODEL, D_FF), BF16),
                        pltpu.VMEM((D_MODEL, D_FF), BF16),
                        pltpu.VMEM((D_FF, D_MODEL), BF16),
                        pltpu.SemaphoreType.DMA((2,)),
                        pltpu.SemaphoreType.DMA((2,)),
                        pltpu.SemaphoreType.DMA(())],
    )
    return pl.pallas_call(
        functools.partial(_moe_ffn_kernel, tb=tb, dummy0=dummy0),
        grid_spec=grid_spec,
        out_shape=jax.ShapeDtypeStruct((n_slabs * SLAB_PITCH, LANES), U32),
        compiler_params=_params(58),
        name="moe_ffn",
    )(plan["blk_exp"], plan["n_used"], plan["next_exp"], plan["gather_tok"], plan["gather_tok"], plan["scatter_dst"],
      plan["scatter_dst"], hn_slabs, w_gate, w_up, w_down)


def _slot_plan(ids, tb):
    T = ids.shape[0]
    M = T * TOP_K
    i32 = jnp.int32
    e = ids[:, :TOP_K].reshape(M)
    skey = jnp.sort(e * M + jnp.arange(M, dtype=i32))
    sm = skey % M
    edges = jnp.arange(N_EXPERTS + 1, dtype=i32) * M
    bounds = jnp.sum((skey[None, :] < edges[:, None]).astype(i32), axis=1)
    start = bounds[:-1]
    cnt = bounds[1:] - start
    nb = (cnt + tb - 1) // tb
    blk_end = jnp.cumsum(nb)
    n_used = blk_end[-1]
    nblk = M // tb + N_EXPERTS
    b = jnp.arange(nblk, dtype=i32)
    be = jnp.minimum(jnp.sum((blk_end[None, :] <= b[:, None]).astype(i32), axis=1), N_EXPERTS - 1)
    j = b - (blk_end[be] - nb[be])
    row0 = start[be] + j * tb
    n_valid = jnp.where(b < n_used, jnp.clip(cnt[be] - j * tb, 0, tb), 0)
    r = jnp.arange(tb, dtype=i32)[None, :]
    valid = r < n_valid[:, None]
    m = sm[jnp.minimum(row0[:, None] + r, M - 1)]
    tok = m // TOP_K
    k = m % TOP_K
    spare = TOP_K * T + (b % 2)[:, None] * tb + r
    blk_exp = jnp.where(b < n_used, be, be[jnp.maximum(n_used - 1, 0)])
    ex = jnp.arange(N_EXPERTS, dtype=i32)
    later = (ex[None, :] > ex[:, None]) & (cnt[None, :] > 0)
    next_of = jnp.min(jnp.where(later, ex[None, :], N_EXPERTS), axis=1)
    next_of = jnp.where(next_of < N_EXPERTS, next_of, -1)
    return dict(blk_exp=blk_exp.astype(i32), n_used=n_used.astype(i32).reshape(1),
                next_exp=next_of[blk_exp].astype(i32),
                gather_tok=jnp.where(valid, tok, 0).astype(i32).reshape(nblk, 1, tb),
                scatter_dst=jnp.where(valid, k * T + tok, spare).astype(i32).reshape(nblk, 1, tb))


def _rope_tables(seq):
    inv = ROPE_THETA ** (-jnp.arange(ROT_HALF, dtype=F32) / ROT_HALF)
    ang = jnp.arange(seq, dtype=jnp.int32).astype(F32)[:, None] * inv[None, :]
    cos = jnp.cos(ang)
    sin = jnp.sin(ang)
    ones = jnp.ones((seq, HEAD_DIM - ROT_DIM), F32)
    cos_t = jnp.concatenate([cos, cos, ones], axis=1)
    sin_t = jnp.concatenate([-sin, sin, 0.0 * ones], axis=1)
    return cos_t, sin_t


def kernel(x_prompt, x_sample, norm1_g, w_in, pool_w, pool_scale, pool_proj, q_norm_g, k_norm_g, sink,
           attn_proj, w_out, norm2_g, router_group_w, router_group_b, router_expert_w, router_expert_b,
           w_gate, w_up, w_down):
    assert norm1_g.shape[0] == 1, "single-layer trunk"
    (b1, s1, d), (b2, s2, _) = x_prompt.shape, x_sample.shape
    t1, t2 = b1 * s1, b2 * s2
    segs = ((t1, s1), (t2, s2))
    s_min = min(s1, s2)
    x1 = x_prompt.reshape(t1, d)
    x2 = x_sample.reshape(t2, d)

    head_gain = jnp.concatenate([jnp.tile(q_norm_g[0], N_HEADS), jnp.tile(k_norm_g[0], N_KV_HEADS)])[None, :]
    pad = LANES - N_EXPERTS - N_GROUPS
    rw = jnp.concatenate([router_expert_w[0], router_group_w[0], jnp.zeros((D_MODEL, pad), F32)], axis=1)
    rw_hi = rw.astype(BF16)
    rw_lo = (rw - rw_hi.astype(F32)).astype(BF16)
    rb = jnp.concatenate([router_expert_b[0], router_group_b[0], jnp.zeros((pad,), F32)])[None, :]
    cos_t, sin_t = _rope_tables(max(s1, s2))

    u = _in_proj(x1, x2, norm1_g[0][None, :], w_in[0].astype(BF16), head_gain, cos_t, sin_t, segs,
                 tm=min(1024, s_min))
    attn = _band_attn(u, sink[0][None, :], segs, tq=min(512, s_min))
    h, hn, ids, wts = _mix_out(x1, x2, u, attn, pool_w[0].astype(BF16), pool_scale[0][None, :],
                               pool_proj[0].astype(BF16), attn_proj[0].astype(BF16), w_out[0].astype(BF16),
                               norm2_g[0][None, :], jnp.concatenate([rw_hi, rw_lo], axis=1), rw_hi, rb,
                               segs, tm=min(256, s_min))
    tb = 256
    plan = _slot_plan(ids, tb)
    o_slabs = _moe_ffn(hn, plan, w_gate[0], w_up[0], w_down[0], t1 + t2, tb)
    y1, y2 = _combine(h, wts, o_slabs, segs, tk=min(256, s_min))
    return (y1.reshape(b1, s1, d), y2.reshape(b2, s2, d))
```

```python
import functools

import jax
import jax.numpy as jnp
from jax import lax
from jax.experimental import pallas as pl
from jax.experimental.pallas import tpu as pltpu

F32 = jnp.float32
BF16 = jnp.bfloat16

D_MODEL = 2048
POOL_GROUPS = 4
POOL_HALF = (1, 2, 4, 8)
POOL_WIDTH = D_MODEL // 2
POOL_GC = POOL_WIDTH // POOL_GROUPS
HEAD_DIM = 128
N_HEADS = D_MODEL // HEAD_DIM
N_KV_HEADS = N_HEADS // 4
Q_PER_KV = N_HEADS // N_KV_HEADS
ATTN_WIDTH = N_HEADS * HEAD_DIM
KV_WIDTH = N_KV_HEADS * HEAD_DIM
WINDOW = 128
BLOCK = 128
ROT_DIM = HEAD_DIM // 4
ROT_HALF = ROT_DIM // 2
ROPE_THETA = 500000.0
IN_WIDTH = POOL_WIDTH + ATTN_WIDTH + 2 * KV_WIDTH + 2 * D_MODEL
N_GROUPS = 4
EXPERTS_PER_GROUP = 8
N_EXPERTS = N_GROUPS * EXPERTS_PER_GROUP
TOP_K = 2
D_FF = D_MODEL // 2
EPS = 1e-6
NEG = -1e30
LOG2_E = 1.4426950408889634

LANES = 128
POOL_HALO = 16
MIX_ROW_CHUNK = 256
WEIGHT_CAST_ROWS = 128
MIB = 1024 * 1024
COL_PIECES = D_MODEL // LANES
SLAB_ROWS = COL_PIECES // 2
SLAB_PITCH = SLAB_ROWS + 1
U32 = jnp.uint32

COL_Q = 0
COL_K = COL_Q + ATTN_WIDTH
COL_V = COL_K + KV_WIDTH
COL_P = COL_V + KV_WIDTH
COL_GP = COL_P + POOL_WIDTH
COL_GA = COL_GP + D_MODEL
QK_WIDTH = ATTN_WIDTH + KV_WIDTH
IN_TN = 1024


def _params(vmem_mib, n_axes=1):
    return pltpu.CompilerParams(dimension_semantics=("arbitrary",) * n_axes,
                                vmem_limit_bytes=vmem_mib * MIB)


def _resident(shape):
    return pl.BlockSpec(shape, lambda *_: (0,) * len(shape), pipeline_mode=pl.Buffered(1))


def _tile_seq_pos(t, tile, segs):
    (t1, s1), (_, s2) = segs
    n1 = t1 // tile
    first = t < n1
    seq = jnp.where(first, s1, s2)
    pos0 = jnp.where(first, (t * tile) % s1, ((t - n1) * tile) % s2)
    return seq, pos0


def _split_specs(tile, width, n1):
    return [pl.BlockSpec((tile, width), lambda t, *_: (jnp.minimum(t, n1 - 1), 0)),
            pl.BlockSpec((tile, width), lambda t, *_: (jnp.maximum(t - n1, 0), 0))]


def _bf16_bits(x):
    return lax.bitcast_convert_type(x.astype(BF16).astype(F32), U32)


def _store_slabs(ref, base_row, n, val):
    for j in range(SLAB_ROWS):
        first = val[:, j * LANES:(j + 1) * LANES]
        second = val[:, (SLAB_ROWS + j) * LANES:(SLAB_ROWS + j + 1) * LANES]
        ref[pl.ds(base_row + j, n, stride=SLAB_PITCH), :] = _bf16_bits(first) | (_bf16_bits(second) >> 16)
    ref[pl.ds(base_row + SLAB_ROWS, n, stride=SLAB_PITCH), :] = jnp.zeros((n, LANES), U32)


def _load_slabs(ref, base_row, n):
    words = [ref[pl.ds(base_row + j, n, stride=SLAB_PITCH), :] for j in range(SLAB_ROWS)]
    first = [lax.bitcast_convert_type(w & jnp.uint32(0xFFFF0000), F32) for w in words]
    second = [lax.bitcast_convert_type(w << 16, F32) for w in words]
    return first + second


def _in_proj_kernel(x1_ref, x2_ref, g1_ref, w_ref, hg_ref, cos_ref, sin_ref, o_ref, xn_ref, *,
                    n1, row_chunk):
    i = pl.program_id(0)
    j = pl.program_id(1)

    def norm_rows(x_ref):
        x = x_ref[...]
        ms = jnp.mean(x * x, axis=-1, keepdims=True)
        xn_ref[...] = (x * lax.rsqrt(ms + EPS) * g1_ref[...]).astype(BF16)

    pl.when((j == 0) & (i < n1))(lambda: norm_rows(x1_ref))
    pl.when((j == 0) & (i >= n1))(lambda: norm_rows(x2_ref))

    tm, tn = o_ref.shape

    def chunk_dot(c):
        rows = slice(c * row_chunk, (c + 1) * row_chunk)
        return rows, jnp.dot(xn_ref[rows, :], w_ref[...], preferred_element_type=F32)

    def epilogue(n_heads):
        lane = lax.broadcasted_iota(jnp.int32, (1, HEAD_DIM), 1)
        for c in range(tm // row_chunk):
            rows, acc = chunk_dot(c)
            if n_heads:
                cos = cos_ref[rows, :]
                sin = sin_ref[rows, :]
            for h in range(n_heads):
                cols = slice(h * HEAD_DIM, (h + 1) * HEAD_DIM)
                a = acc[:, cols]
                ms = jnp.mean(a * a, axis=-1, keepdims=True)
                y = a * lax.rsqrt(ms + EPS) * hg_ref[:, cols]
                partner = jnp.where(lane < ROT_HALF,
                                    pltpu.roll(y, HEAD_DIM - ROT_HALF, 1),
                                    pltpu.roll(y, ROT_HALF, 1))
                o_ref[rows, cols] = (y * cos + partner * sin).astype(BF16)
            if n_heads * HEAD_DIM < tn:
                o_ref[rows, n_heads * HEAD_DIM:] = acc[:, n_heads * HEAD_DIM:].astype(BF16)

    n_q_blocks = ATTN_WIDTH // tn
    k_heads_in_last = (QK_WIDTH - n_q_blocks * tn) // HEAD_DIM
    pl.when(j < n_q_blocks)(lambda: epilogue(tn // HEAD_DIM))
    pl.when(j == n_q_blocks)(lambda: epilogue(k_heads_in_last))
    pl.when(j > n_q_blocks)(lambda: epilogue(0))


def _w_in_block(j):
    n_pool = POOL_WIDTH // IN_TN
    n_qkv = (ATTN_WIDTH + 2 * KV_WIDTH) // IN_TN
    return jnp.where(j < n_qkv, j + n_pool, jnp.where(j < n_qkv + n_pool, j - n_qkv, j))


def _in_proj(x1, x2, g1, w_in, head_gain, cos_t, sin_t, segs, tm):
    (t1, s1), (t2, s2) = segs
    n1 = t1 // tm
    p1 = s1 // tm
    p2 = s2 // tm
    pos_tile = lambda i, j: (jnp.where(i < n1, i % p1, (i - n1) % p2), 0)
    assert ATTN_WIDTH % IN_TN == 0 and QK_WIDTH - ATTN_WIDTH <= IN_TN
    kern = functools.partial(_in_proj_kernel, n1=n1, row_chunk=min(256, tm))
    return pl.pallas_call(
        kern,
        grid=((t1 + t2) // tm, IN_WIDTH // IN_TN),
        in_specs=_split_specs(tm, D_MODEL, n1) + [
            pl.BlockSpec((1, D_MODEL), lambda i, j: (0, 0)),
            pl.BlockSpec((D_MODEL, IN_TN), lambda i, j: (0, _w_in_block(j))),
            pl.BlockSpec((1, IN_TN), lambda i, j: (0, jnp.minimum(j, ATTN_WIDTH // IN_TN))),
            pl.BlockSpec((tm, HEAD_DIM), pos_tile),
            pl.BlockSpec((tm, HEAD_DIM), pos_tile),
        ],
        out_specs=pl.BlockSpec((tm, IN_TN), lambda i, j: (i, j)),
        out_shape=jax.ShapeDtypeStruct((t1 + t2, IN_WIDTH), BF16),
        scratch_shapes=[pltpu.VMEM((tm, D_MODEL), BF16)],
        compiler_params=_params(58, 2),
        name="in_proj",
    )(x1, x2, g1, w_in, head_gain, cos_t, sin_t)


def _band_attn_kernel(sink_ref, q_ref, kp_ref, km_ref, kn_ref, vp_ref, vm_ref, vn_ref, o_ref,
                      kbuf, vbuf, *, segs, tq):
    seq, pos0 = _tile_seq_pos(pl.program_id(0), tq, segs)
    kbuf[0:BLOCK, :] = kp_ref[...]
    kbuf[BLOCK:BLOCK + tq, :] = km_ref[...]
    kbuf[BLOCK + tq:, :] = kn_ref[...]
    vbuf[0:BLOCK, :] = vp_ref[...]
    vbuf[BLOCK:BLOCK + tq, :] = vm_ref[...]
    vbuf[BLOCK + tq:, :] = vn_ref[...]

    scale = HEAD_DIM ** -0.5
    c = scale * LOG2_E
    row = lax.broadcasted_iota(jnp.int32, (BLOCK, 3 * BLOCK), 0)
    col = lax.broadcasted_iota(jnp.int32, (BLOCK, 3 * BLOCK), 1)

    def q_block(qb, carry):
        r0 = pl.multiple_of(qb * BLOCK, BLOCK)
        base = pos0 + r0 - BLOCK
        lo = jnp.maximum(row, -base)
        hi = jnp.minimum(row + 2 * WINDOW, seq - 1 - base)
        valid = (col >= lo) & (col <= hi)
        for kv in range(N_KV_HEADS):
            kc = slice(kv * HEAD_DIM, (kv + 1) * HEAD_DIM)
            q4 = jnp.concatenate(
                [q_ref[pl.ds(r0, BLOCK), (kv * Q_PER_KV + g) * HEAD_DIM:(kv * Q_PER_KV + g + 1) * HEAD_DIM]
                 for g in range(Q_PER_KV)], axis=0)
            kcat = kbuf[pl.ds(r0, 3 * BLOCK), kc]
            vcat = vbuf[pl.ds(r0, 3 * BLOCK), kc]
            s4 = lax.dot_general(q4, kcat, (((1,), (1,)), ((), ())), preferred_element_type=F32)
            ps, inv = [], []
            for g in range(Q_PER_KV):
                s = jnp.where(valid, s4[g * BLOCK:(g + 1) * BLOCK], NEG)
                sink = sink_ref[0, kv * Q_PER_KV + g] * (1.0 / scale)
                m = jnp.maximum(jnp.max(s, axis=-1, keepdims=True), sink)
                p = jnp.exp2((s - m) * c)
                inv.append(1.0 / (jnp.sum(p, axis=-1, keepdims=True) + jnp.exp2((sink - m) * c)))
                ps.append(p.astype(BF16))
            o = jnp.dot(jnp.concatenate(ps, axis=0), vcat, preferred_element_type=F32)
            for g in range(Q_PER_KV):
                hc = (kv * Q_PER_KV + g) * HEAD_DIM
                o_ref[pl.ds(r0, BLOCK), hc:hc + HEAD_DIM] = (o[g * BLOCK:(g + 1) * BLOCK] * inv[g]).astype(BF16)
        return carry

    lax.fori_loop(0, tq // BLOCK, q_block, 0)


def _band_attn(u, sink, segs, tq):
    T = u.shape[0]
    nb = tq // BLOCK
    last_blk = T // BLOCK - 1
    kcol = COL_K // KV_WIDTH
    vcol = COL_V // KV_WIDTH
    prev_map = lambda c: (lambda t: (jnp.maximum(t * nb - 1, 0), c))
    next_map = lambda c: (lambda t: (jnp.minimum((t + 1) * nb, last_blk), c))
    main_map = lambda c: (lambda t: (t, c))
    kern = functools.partial(_band_attn_kernel, segs=segs, tq=tq)
    return pl.pallas_call(
        kern,
        grid=(T // tq,),
        in_specs=[
            pl.BlockSpec(memory_space=pltpu.SMEM),
            pl.BlockSpec((tq, ATTN_WIDTH), lambda t: (t, COL_Q // ATTN_WIDTH)),
            pl.BlockSpec((BLOCK, KV_WIDTH), prev_map(kcol)),
            pl.BlockSpec((tq, KV_WIDTH), main_map(kcol)),
            pl.BlockSpec((BLOCK, KV_WIDTH), next_map(kcol)),
            pl.BlockSpec((BLOCK, KV_WIDTH), prev_map(vcol)),
            pl.BlockSpec((tq, KV_WIDTH), main_map(vcol)),
            pl.BlockSpec((BLOCK, KV_WIDTH), next_map(vcol)),
        ],
        out_specs=pl.BlockSpec((tq, ATTN_WIDTH), lambda t: (t, 0)),
        out_shape=jax.ShapeDtypeStruct((T, ATTN_WIDTH), BF16),
        scratch_shapes=[pltpu.VMEM((tq + 2 * BLOCK, KV_WIDTH), BF16),
                        pltpu.VMEM((tq + 2 * BLOCK, KV_WIDTH), BF16)],
        compiler_params=_params(40),
        name="band_attn",
    )(sink, u, u, u, u, u, u, u)


def _mix_out_kernel(x1_ref, x2_ref, xpp_ref, xpm_ref, xpn_ref, at_ref, gp_ref, ga_ref,
                    pw_ref, ps_ref, pp_ref, ap_ref, wo_ref, g2_ref, rw_ref, rb_ref,
                    h_ref, hn_ref, ids_ref, wts_ref, xb, pin, *, segs, tm):
    t = pl.program_id(0)
    seq, pos0 = _tile_seq_pos(t, tm, segs)
    in_first = t < segs[0][0] // tm

    xb[0:POOL_HALO, :] = jnp.where(pos0 > 0, xpp_ref[...].astype(F32), 0.0)
    xb[POOL_HALO:POOL_HALO + tm, :] = xpm_ref[...].astype(F32)
    xb[POOL_HALO + tm:, :] = jnp.where(pos0 + tm < seq, xpn_ref[...].astype(F32), 0.0)
    rc = min(MIX_ROW_CHUNK, tm)
    for c in range(tm // rc):
        r0 = c * rc
        rows = slice(r0, r0 + rc)
        x = jnp.where(in_first, x1_ref[rows, :], x2_ref[rows, :])
        _mix_out_rows(r0, rc, pos0, seq, x, at_ref, gp_ref, ga_ref, pw_ref, ps_ref, pp_ref,
                      ap_ref, wo_ref, g2_ref, rw_ref, rb_ref, h_ref, hn_ref, ids_ref, wts_ref, xb, pin)


def _mix_out_rows(r0, rc, pos0, seq, x, at_ref, gp_ref, ga_ref, pw_ref, ps_ref, pp_ref, ap_ref, wo_ref,
                  g2_ref, rw_ref, rb_ref, h_ref, hn_ref, ids_ref, wts_ref, xb, pin):
    rows = slice(r0, r0 + rc)
    tpos = pos0 + r0 + lax.broadcasted_iota(jnp.int32, (rc, 1), 0)
    for g, half in enumerate(POOL_HALF):
        cols = slice(g * POOL_GC, (g + 1) * POOL_GC)
        c0 = POOL_HALO + r0
        win = xb[c0 - half:c0 - half + rc, cols]
        for d in range(-half + 1, half):
            win = win + xb[c0 + d:c0 + d + rc, cols]
        cnt = (jnp.minimum(tpos + half, seq) - jnp.maximum(tpos - half, 0)).astype(F32)
        mixed = (win / cnt - xb[c0:c0 + rc, cols]).astype(BF16)
        yg = jnp.dot(mixed, pw_ref[g], preferred_element_type=F32)
        pin[rows, cols] = (yg * ps_ref[:, cols]).astype(BF16)
    pool_out = jnp.dot(pin[rows, :], pp_ref[...], preferred_element_type=F32)

    attn_out = jnp.dot(at_ref[rows, :], ap_ref[...], preferred_element_type=F32)
    merged = (jax.nn.sigmoid(gp_ref[rows, :].astype(F32)) * pool_out
              + jax.nn.sigmoid(ga_ref[rows, :].astype(F32)) * attn_out)
    h = x + jnp.dot(merged.astype(BF16), wo_ref[...], preferred_element_type=F32)
    h_ref[rows, :] = h

    ms = jnp.mean(h * h, axis=-1, keepdims=True)
    hn = h * lax.rsqrt(ms + EPS) * g2_ref[...]
    _store_slabs(hn_ref, r0 * SLAB_PITCH, rc, hn)

    logit = jnp.dot(hn.astype(BF16), rw_ref[...], preferred_element_type=F32) + rb_ref[...]

    lane_i = lax.broadcasted_iota(jnp.int32, logit.shape, 1)
    lane = lane_i.astype(F32)
    big = float(LANES)
    is_grp = (lane_i >= N_EXPERTS) & (lane_i < N_EXPERTS + N_GROUPS)
    mg = jnp.max(jnp.where(is_grp, logit, -jnp.inf), axis=-1, keepdims=True)
    grp = jnp.min(jnp.where(is_grp & (logit == mg), lane - N_EXPERTS, big), axis=-1, keepdims=True)
    p_grp = 1.0 / jnp.sum(jnp.where(is_grp, jnp.exp(logit - mg), 0.0), axis=-1, keepdims=True)

    grp_of_lane = (lane_i // EXPERTS_PER_GROUP).astype(F32)
    in_grp = (lane_i < N_EXPERTS) & (grp_of_lane == grp)
    m1 = jnp.max(jnp.where(in_grp, logit, -jnp.inf), axis=-1, keepdims=True)
    i1 = jnp.min(jnp.where(in_grp & (logit == m1), lane, big), axis=-1, keepdims=True)
    rest = in_grp & (lane != i1)
    m2 = jnp.max(jnp.where(rest, logit, -jnp.inf), axis=-1, keepdims=True)
    i2 = jnp.min(jnp.where(rest & (logit == m2), lane, big), axis=-1, keepdims=True)
    e2 = jnp.exp(m2 - m1)
    den = 1.0 + e2
    ids_ref[rows, :] = jnp.where(lane_i == 0, i1, jnp.where(lane_i == 1, i2, 0.0)).astype(jnp.int32)
    wts_ref[rows, :] = jnp.where(lane_i == 0, p_grp * (1.0 / den),
                                 jnp.where(lane_i == 1, p_grp * (e2 / den), 0.0))


def _mix_out(x1, x2, u, attn, pool_w, pool_scale, pool_proj, attn_proj, w_out, g2, rw, rb, segs, tm):
    T = u.shape[0]
    n1 = segs[0][0] // tm
    nh = tm // POOL_HALO
    last_halo = T // POOL_HALO - 1
    pcol = COL_P // POOL_WIDTH
    kern = functools.partial(_mix_out_kernel, segs=segs, tm=tm)
    row_spec = lambda w: pl.BlockSpec((tm, w), lambda t: (t, 0))
    return pl.pallas_call(
        kern,
        grid=(T // tm,),
        in_specs=_split_specs(tm, D_MODEL, n1) + [
            pl.BlockSpec((POOL_HALO, POOL_WIDTH), lambda t: (jnp.maximum(t * nh - 1, 0), pcol)),
            pl.BlockSpec((tm, POOL_WIDTH), lambda t: (t, pcol)),
            pl.BlockSpec((POOL_HALO, POOL_WIDTH), lambda t: (jnp.minimum((t + 1) * nh, last_halo), pcol)),
            row_spec(ATTN_WIDTH),
            pl.BlockSpec((tm, D_MODEL), lambda t: (t, COL_GP // D_MODEL)),
            pl.BlockSpec((tm, D_MODEL), lambda t: (t, COL_GA // D_MODEL)),
            _resident((POOL_GROUPS, POOL_GC, POOL_GC)),
            _resident((1, POOL_WIDTH)),
            _resident((POOL_WIDTH, D_MODEL)),
            _resident((ATTN_WIDTH, D_MODEL)),
            _resident((D_MODEL, D_MODEL)),
            _resident((1, D_MODEL)),
            _resident((D_MODEL, LANES)),
            _resident((1, LANES)),
        ],
        out_specs=[row_spec(D_MODEL), pl.BlockSpec((tm * SLAB_PITCH, LANES), lambda t: (t, 0)),
                   row_spec(LANES), row_spec(LANES)],
        out_shape=[jax.ShapeDtypeStruct((T, D_MODEL), F32),
                   jax.ShapeDtypeStruct((T * SLAB_PITCH, LANES), U32),
                   jax.ShapeDtypeStruct((T, LANES), jnp.int32),
                   jax.ShapeDtypeStruct((T, LANES), F32)],
        scratch_shapes=[pltpu.VMEM((tm + 2 * POOL_HALO, POOL_WIDTH), F32),
                        pltpu.VMEM((tm, POOL_WIDTH), BF16)],
        compiler_params=_params(56),
        name="mix_out",
    )(x1, x2, u, u, u, attn, u, u, pool_w, pool_scale, pool_proj, attn_proj, w_out, g2, rw, rb)


def _combine_kernel(h_ref, wts_ref, o0_ref, o1_ref, y1_ref, y2_ref, *, tk, n1):
    t = pl.program_id(0)
    w = wts_ref[...]
    w0 = w[:, 0:1]
    w1 = w[:, 1:2]
    o0 = _load_slabs(o0_ref, 0, tk)
    o1 = _load_slabs(o1_ref, 0, tk)

    def write(y_ref):
        for j in range(COL_PIECES):
            cols = slice(j * LANES, (j + 1) * LANES)
            y_ref[:, cols] = h_ref[:, cols] + (w0 * o0[j] + w1 * o1[j])

    pl.when(t < n1)(lambda: write(y1_ref))
    pl.when(t >= n1)(lambda: write(y2_ref))


def _combine(h, wts, o_slabs, segs, tk):
    T = h.shape[0]
    nt = T // tk
    n1 = segs[0][0] // tk
    return pl.pallas_call(
        functools.partial(_combine_kernel, tk=tk, n1=n1),
        grid=(nt,),
        in_specs=[
            pl.BlockSpec((tk, D_MODEL), lambda t: (t, 0)),
            pl.BlockSpec((tk, LANES), lambda t: (t, 0)),
            pl.BlockSpec((tk * SLAB_PITCH, LANES), lambda t: (t, 0)),
            pl.BlockSpec((tk * SLAB_PITCH, LANES), lambda t: (nt + t, 0)),
        ],
        out_specs=_split_specs(tk, D_MODEL, n1),
        out_shape=[jax.ShapeDtypeStruct((segs[0][0], D_MODEL), F32),
                   jax.ShapeDtypeStruct((segs[1][0], D_MODEL), F32)],
        compiler_params=_params(48),
        name="combine",
    )(h, wts, o_slabs, o_slabs)


def _moe_ffn_kernel(be_ref, nu_ref, nx_ref, g0_ref, gnext_ref, sprev_ref, scur_ref, hn_ref,
                    wg_hbm, wu_hbm, wd_hbm, o_ref, xbuf, obuf, wg_f32, wu_f32, wd_f32, wg_ref, wu_ref, wd_ref,
                    sem_g, sem_s, sem_w, *, tb, dummy0):
    b = pl.program_id(0)
    n_used = nu_ref[0]
    expert = be_ref[b]

    def weight_copies(e):
        return [pltpu.make_async_copy(src.at[e], dst, sem_w)
                for src, dst in ((wg_hbm, wg_f32), (wu_hbm, wu_f32), (wd_hbm, wd_f32))]

    @pl.when(b == 0)
    def _():
        for cp in weight_copies(expert):
            cp.start()

    @pl.when((b < n_used) & ((b == 0) | (expert != be_ref[jnp.maximum(b - 1, 0)])))
    def _():
        for cp in weight_copies(expert):
            cp.wait()
        for f32_ref, bf16_ref in ((wg_f32, wg_ref), (wu_f32, wu_ref), (wd_f32, wd_ref)):
            n_chunks = f32_ref.shape[0] // WEIGHT_CAST_ROWS

            def cast_chunk(ci, carry, f32_ref=f32_ref, bf16_ref=bf16_ref):
                rows = pl.ds(pl.multiple_of(ci * WEIGHT_CAST_ROWS, WEIGHT_CAST_ROWS), WEIGHT_CAST_ROWS)
                bf16_ref[rows, :] = f32_ref[rows, :].astype(BF16)
                return carry

            lax.fori_loop(0, n_chunks, cast_chunk, 0)
        nxt = nx_ref[b]

        @pl.when(nxt >= 0)
        def _():
            for cp in weight_copies(nxt):
                cp.start()
    slot = b % 2
    half = tb * SLAB_PITCH
    mine = pl.multiple_of(slot * half, 8)
    other = pl.multiple_of((1 - slot) * half, 8)

    def slab_copy(src, src_row, dst, dst_row, sem):
        return pltpu.make_async_copy(src.at[pl.ds(src_row, SLAB_PITCH)], dst.at[pl.ds(dst_row, SLAB_PITCH)], sem)

    def half_copy(src, dst, dst_row, sem):
        return pltpu.make_async_copy(src.at[pl.ds(0, half)], dst.at[pl.ds(dst_row, half)], sem)

    def start_gathers(tok_ref, base, sem):
        for r in range(tb):
            slab_copy(hn_ref, tok_ref[0, 0, r] * SLAB_PITCH, xbuf, base + r * SLAB_PITCH, sem).start(priority=r % 2)

    def start_scatters(dst_of_row, base, sem):
        for r in range(tb):
            slab_copy(obuf, base + r * SLAB_PITCH, o_ref, dst_of_row(r) * SLAB_PITCH, sem).start(priority=r % 2)

    @pl.when(b == 0)
    def _():
        obuf[...] = jnp.zeros_like(obuf)
        start_gathers(g0_ref, 0, sem_g.at[0])
        half_copy(obuf, o_ref, dummy0 * SLAB_PITCH, sem_s.at[0]).start()

    @pl.when(b < n_used)
    def _():
        half_copy(hn_ref, xbuf, mine, sem_g.at[slot]).wait()
        half_copy(obuf, o_ref, 0, sem_s.at[slot]).wait()
        start_gathers(gnext_ref, other, sem_g.at[1 - slot])
        first = b == 0
        start_scatters(lambda r: jnp.where(first, dummy0 + tb + r, sprev_ref[0, 0, r]), other, sem_s.at[1 - slot])

        x = jnp.concatenate([p.astype(BF16) for p in _load_slabs(xbuf, mine, tb)], axis=1)
        gate = jnp.dot(x, wg_ref[...], preferred_element_type=F32)
        up = jnp.dot(x, wu_ref[...], preferred_element_type=F32)
        mid = (jax.nn.silu(gate) * up).astype(BF16)
        out = jnp.dot(mid, wd_ref[...], preferred_element_type=F32)
        _store_slabs(obuf, mine, tb, out)

        @pl.when(b == n_used - 1)
        def _():
            half_copy(hn_ref, xbuf, other, sem_g.at[1 - slot]).wait()
            start_scatters(lambda r: scur_ref[0, 0, r], mine, sem_s.at[slot])
            half_copy(obuf, o_ref, 0, sem_s.at[1 - slot]).wait()
            half_copy(obuf, o_ref, 0, sem_s.at[slot]).wait()


def _moe_ffn(hn_slabs, plan, w_gate, w_up, w_down, n_tokens, tb):
    nblk = plan["gather_tok"].shape[0]
    dummy0 = TOP_K * n_tokens
    n_slabs = dummy0 + 2 * tb
    idx_spec = lambda f: pl.BlockSpec((1, 1, tb), lambda i, *_: (f(i), 0, 0), memory_space=pltpu.SMEM)
    hbm = pl.BlockSpec(memory_space=pl.ANY)
    grid_spec = pltpu.PrefetchScalarGridSpec(
        num_scalar_prefetch=3,
        grid=(nblk,),
        in_specs=[
            idx_spec(lambda i: 0),
            idx_spec(lambda i: jnp.minimum(i + 1, nblk - 1)),
            idx_spec(lambda i: jnp.maximum(i - 1, 0)),
            idx_spec(lambda i: i),
            hbm, hbm, hbm, hbm,
        ],
        out_specs=hbm,
        scratch_shapes=[pltpu.VMEM((2 * tb * SLAB_PITCH, LANES), U32),
                        pltpu.VMEM((2 * tb * SLAB_PITCH, LANES), U32),
                        pltpu.VMEM((D_MODEL, D_FF), F32),
                        pltpu.VMEM((D_MODEL, D_FF), F32),
                        pltpu.VMEM((D_FF, D_MODEL), F32),
                        pltpu.VMEM((D_MODEL, D_FF), BF16),
                        pltpu.VMEM((D_MODEL, D_FF), BF16),
                        pltpu.VMEM((D_FF, D_MODEL), BF16),
                        pltpu.SemaphoreType.DMA((2,)),
                        pltpu.SemaphoreType.DMA((2,)),
                        pltpu.SemaphoreType.DMA(())],
    )
    return pl.pallas_call(
        functools.partial(_moe_ffn_kernel, tb=tb, dummy0=dummy0),
        grid_spec=grid_spec,
        out_shape=jax.ShapeDtypeStruct((n_slabs * SLAB_PITCH, LANES), U32),
        compiler_params=_params(58),
        name="moe_ffn",
    )(plan["blk_exp"], plan["n_used"], plan["next_exp"], plan["gather_tok"], plan["gather_tok"], plan["scatter_dst"],
      plan["scatter_dst"], hn_slabs, w_gate, w_up, w_down)


def _slot_plan(ids, tb):
    T = ids.shape[0]
    M = T * TOP_K
    i32 = jnp.int32
    e = ids[:, :TOP_K].reshape(M)
    skey = jnp.sort(e * M + jnp.arange(M, dtype=i32))
    sm = skey % M
    edges = jnp.arange(N_EXPERTS + 1, dtype=i32) * M
    bounds = jnp.sum((skey[None, :] < edges[:, None]).astype(i32), axis=1)
    start = bounds[:-1]
    cnt = bounds[1:] - start
    nb = (cnt + tb - 1) // tb
    blk_end = jnp.cumsum(nb)
    n_used = blk_end[-1]
    nblk = M // tb + N_EXPERTS
    b = jnp.arange(nblk, dtype=i32)
    be = jnp.minimum(jnp.sum((blk_end[None, :] <= b[:, None]).astype(i32), axis=1), N_EXPERTS - 1)
    j = b - (blk_end[be] - nb[be])
    row0 = start[be] + j * tb
    n_valid = jnp.where(b < n_used, jnp.clip(cnt[be] - j * tb, 0, tb), 0)
    r = jnp.arange(tb, dtype=i32)[None, :]
    valid = r < n_valid[:, None]
    m = sm[jnp.minimum(row0[:, None] + r, M - 1)]
    tok = m // TOP_K
    k = m % TOP_K
    spare = TOP_K * T + (b % 2)[:, None] * tb + r
    blk_exp = jnp.where(b < n_used, be, be[jnp.maximum(n_used - 1, 0)])
    ex = jnp.arange(N_EXPERTS, dtype=i32)
    later = (ex[None, :] > ex[:, None]) & (cnt[None, :] > 0)
    next_of = jnp.min(jnp.where(later, ex[None, :], N_EXPERTS), axis=1)
    next_of = jnp.where(next_of < N_EXPERTS, next_of, -1)
    return dict(blk_exp=blk_exp.astype(i32), n_used=n_used.astype(i32).reshape(1),
                next_exp=next_of[blk_exp].astype(i32),
                gather_tok=jnp.where(valid, tok, 0).astype(i32).reshape(nblk, 1, tb),
                scatter_dst=jnp.where(valid, k * T + tok, spare).astype(i32).reshape(nblk, 1, tb))


def _rope_tables(seq):
    inv = ROPE_THETA ** (-jnp.arange(ROT_HALF, dtype=F32) / ROT_HALF)
    ang = jnp.arange(seq, dtype=jnp.int32).astype(F32)[:, None] * inv[None, :]
    cos = jnp.cos(ang)
    sin = jnp.sin(ang)
    ones = jnp.ones((seq, HEAD_DIM - ROT_DIM), F32)
    cos_t = jnp.concatenate([cos, cos, ones], axis=1)
    sin_t = jnp.concatenate([-sin, sin, 0.0 * ones], axis=1)
    return cos_t, sin_t


def kernel(x_prompt, x_sample, norm1_g, w_in, pool_w, pool_scale, pool_proj, q_norm_g, k_norm_g, sink,
           attn_proj, w_out, norm2_g, router_group_w, router_group_b, router_expert_w, router_expert_b,
           w_gate, w_up, w_down):
    assert norm1_g.shape[0] == 1, "single-layer trunk"
    (b1, s1, d), (b2, s2, _) = x_prompt.shape, x_sample.shape
    t1, t2 = b1 * s1, b2 * s2
    segs = ((t1, s1), (t2, s2))
    s_min = min(s1, s2)
    x1 = x_prompt.reshape(t1, d)
    x2 = x_sample.reshape(t2, d)

    gain_pad = jnp.ones((-QK_WIDTH % IN_TN,), F32)
    head_gain = jnp.concatenate([jnp.tile(q_norm_g[0], N_HEADS), jnp.tile(k_norm_g[0], N_KV_HEADS),
                                 gain_pad])[None, :]
    pad = LANES - N_EXPERTS - N_GROUPS
    rw = jnp.concatenate([router_expert_w[0], router_group_w[0], jnp.zeros((D_MODEL, pad), F32)], axis=1)
    rb = jnp.concatenate([router_expert_b[0], router_group_b[0], jnp.zeros((pad,), F32)])[None, :]
    cos_t, sin_t = _rope_tables(max(s1, s2))

    u = _in_proj(x1, x2, norm1_g[0][None, :], w_in[0].astype(BF16), head_gain, cos_t, sin_t, segs,
                 tm=min(1024, s_min))
    attn = _band_attn(u, sink[0][None, :], segs, tq=min(512, s_min))
    h, hn, ids, wts = _mix_out(x1, x2, u, attn, pool_w[0].astype(BF16), pool_scale[0][None, :],
                               pool_proj[0].astype(BF16), attn_proj[0].astype(BF16), w_out[0].astype(BF16),
                               norm2_g[0][None, :], rw.astype(BF16), rb, segs, tm=min(256, s_min))
    tb = 256
    plan = _slot_plan(ids, tb)
    o_slabs = _moe_ffn(hn, plan, w_gate[0], w_up[0], w_down[0], t1 + t2, tb)
    y1, y2 = _combine(h, wts, o_slabs, segs, tk=min(512, s_min))
    return (y1.reshape(b1, s1, d), y2.reshape(b2, s2, d))
```

```python
import functools

import jax
import jax.numpy as jnp
from jax import lax
from jax.experimental import pallas as pl
from jax.experimental.pallas import tpu as pltpu

F32 = jnp.float32
BF16 = jnp.bfloat16

D_MODEL = 2048
POOL_GROUPS = 4
POOL_HALF = (1, 2, 4, 8)
POOL_WIDTH = D_MODEL // 2
POOL_GC = POOL_WIDTH // POOL_GROUPS
HEAD_DIM = 128
N_HEADS = D_MODEL // HEAD_DIM
N_KV_HEADS = N_HEADS // 4
Q_PER_KV = N_HEADS // N_KV_HEADS
ATTN_WIDTH = N_HEADS * HEAD_DIM
KV_WIDTH = N_KV_HEADS * HEAD_DIM
WINDOW = 128
BLOCK = 128
ROT_DIM = HEAD_DIM // 4
ROT_HALF = ROT_DIM // 2
ROPE_THETA = 500000.0
ROPE_SPLIT = 128
IN_WIDTH = POOL_WIDTH + ATTN_WIDTH + 2 * KV_WIDTH + 2 * D_MODEL
N_GROUPS = 4
EXPERTS_PER_GROUP = 8
N_EXPERTS = N_GROUPS * EXPERTS_PER_GROUP
TOP_K = 2
D_FF = D_MODEL // 2
EPS = 1e-6
NEG = -1e30
LOG2_E = 1.4426950408889634

LANES = 128
POOL_HALO = 16
MIX_ROW_CHUNK = 256
WEIGHT_CAST_ROWS = 128
MIB = 1024 * 1024
COL_PIECES = D_MODEL // LANES
SLAB_ROWS = COL_PIECES // 2
SLAB_PITCH = SLAB_ROWS + 1
U32 = jnp.uint32

COL_Q = 0
COL_K = COL_Q + ATTN_WIDTH
COL_V = COL_K + KV_WIDTH
COL_P = COL_V + KV_WIDTH
COL_GP = COL_P + POOL_WIDTH
COL_GA = COL_GP + D_MODEL
QK_WIDTH = ATTN_WIDTH + KV_WIDTH
IN_TN = 1024


def _params(vmem_mib, n_axes=1):
    return pltpu.CompilerParams(dimension_semantics=("arbitrary",) * n_axes,
                                vmem_limit_bytes=vmem_mib * MIB)


def _resident(shape):
    return pl.BlockSpec(shape, lambda *_: (0,) * len(shape), pipeline_mode=pl.Buffered(1))


def _tile_seq_pos(t, tile, segs):
    (t1, s1), (_, s2) = segs
    n1 = t1 // tile
    first = t < n1
    seq = jnp.where(first, s1, s2)
    pos0 = jnp.where(first, (t * tile) % s1, ((t - n1) * tile) % s2)
    return seq, pos0


def _split_specs(tile, width, n1):
    return [pl.BlockSpec((tile, width), lambda t, *_: (jnp.minimum(t, n1 - 1), 0)),
            pl.BlockSpec((tile, width), lambda t, *_: (jnp.maximum(t - n1, 0), 0))]


def _bf16_bits(x):
    return lax.bitcast_convert_type(x.astype(BF16).astype(F32), U32)


def _store_slabs(ref, base_row, n, val):
    for j in range(SLAB_ROWS):
        first = val[:, j * LANES:(j + 1) * LANES]
        second = val[:, (SLAB_ROWS + j) * LANES:(SLAB_ROWS + j + 1) * LANES]
        ref[pl.ds(base_row + j, n, stride=SLAB_PITCH), :] = _bf16_bits(first) | (_bf16_bits(second) >> 16)
    ref[pl.ds(base_row + SLAB_ROWS, n, stride=SLAB_PITCH), :] = jnp.zeros((n, LANES), U32)


def _load_slabs(ref, base_row, n):
    words = [ref[pl.ds(base_row + j, n, stride=SLAB_PITCH), :] for j in range(SLAB_ROWS)]
    first = [lax.bitcast_convert_type(w & jnp.uint32(0xFFFF0000), F32) for w in words]
    second = [lax.bitcast_convert_type(w << 16, F32) for w in words]
    return first + second


def _in_proj_kernel(x1_ref, x2_ref, g1_ref, w_ref, hg_ref, cos_ref, sin_ref, o_ref, xn_ref, *,
                    n1, row_chunk):
    i = pl.program_id(0)
    j = pl.program_id(1)

    def norm_rows(x_ref):
        x = x_ref[...]
        ms = jnp.mean(x * x, axis=-1, keepdims=True)
        xn_ref[...] = (x * lax.rsqrt(ms + EPS) * g1_ref[...]).astype(BF16)

    pl.when((j == 0) & (i < n1))(lambda: norm_rows(x1_ref))
    pl.when((j == 0) & (i >= n1))(lambda: norm_rows(x2_ref))

    tm, tn = o_ref.shape

    def chunk_dot(c):
        rows = slice(c * row_chunk, (c + 1) * row_chunk)
        return rows, jnp.dot(xn_ref[rows, :], w_ref[...], preferred_element_type=F32)

    def epilogue(n_heads):
        lane = lax.broadcasted_iota(jnp.int32, (1, HEAD_DIM), 1)
        for c in range(tm // row_chunk):
            rows, acc = chunk_dot(c)
            if n_heads:
                cos = cos_ref[rows, :]
                sin = sin_ref[rows, :]
            for h in range(n_heads):
                cols = slice(h * HEAD_DIM, (h + 1) * HEAD_DIM)
                a = acc[:, cols]
                ms = jnp.mean(a * a, axis=-1, keepdims=True)
                y = a * lax.rsqrt(ms + EPS) * hg_ref[:, cols]
                partner = jnp.where(lane < ROT_HALF,
                                    pltpu.roll(y, HEAD_DIM - ROT_HALF, 1),
                                    pltpu.roll(y, ROT_HALF, 1))
                o_ref[rows, cols] = (y * cos + partner * sin).astype(BF16)
            if n_heads * HEAD_DIM < tn:
                o_ref[rows, n_heads * HEAD_DIM:] = acc[:, n_heads * HEAD_DIM:].astype(BF16)

    n_q_blocks = ATTN_WIDTH // tn
    k_heads_in_last = (QK_WIDTH - n_q_blocks * tn) // HEAD_DIM
    pl.when(j < n_q_blocks)(lambda: epilogue(tn // HEAD_DIM))
    pl.when(j == n_q_blocks)(lambda: epilogue(k_heads_in_last))
    pl.when(j > n_q_blocks)(lambda: epilogue(0))


def _w_in_block(j):
    n_pool = POOL_WIDTH // IN_TN
    n_qkv = (ATTN_WIDTH + 2 * KV_WIDTH) // IN_TN
    return jnp.where(j < n_qkv, j + n_pool, jnp.where(j < n_qkv + n_pool, j - n_qkv, j))


def _in_proj(x1, x2, g1, w_in, head_gain, cos_t, sin_t, segs, tm):
    (t1, s1), (t2, s2) = segs
    n1 = t1 // tm
    p1 = s1 // tm
    p2 = s2 // tm
    pos_tile = lambda i, j: (jnp.where(i < n1, i % p1, (i - n1) % p2), 0)
    assert ATTN_WIDTH % IN_TN == 0 and QK_WIDTH - ATTN_WIDTH <= IN_TN
    kern = functools.partial(_in_proj_kernel, n1=n1, row_chunk=min(256, tm))
    return pl.pallas_call(
        kern,
        grid=((t1 + t2) // tm, IN_WIDTH // IN_TN),
        in_specs=_split_specs(tm, D_MODEL, n1) + [
            pl.BlockSpec((1, D_MODEL), lambda i, j: (0, 0)),
            pl.BlockSpec((D_MODEL, IN_TN), lambda i, j: (0, _w_in_block(j))),
            pl.BlockSpec((1, IN_TN), lambda i, j: (0, jnp.minimum(j, ATTN_WIDTH // IN_TN))),
            pl.BlockSpec((tm, HEAD_DIM), pos_tile),
            pl.BlockSpec((tm, HEAD_DIM), pos_tile),
        ],
        out_specs=pl.BlockSpec((tm, IN_TN), lambda i, j: (i, j)),
        out_shape=jax.ShapeDtypeStruct((t1 + t2, IN_WIDTH), BF16),
        scratch_shapes=[pltpu.VMEM((tm, D_MODEL), BF16)],
        compiler_params=_params(58, 2),
        name="in_proj",
    )(x1, x2, g1, w_in, head_gain, cos_t, sin_t)


def _band_attn_kernel(sink_ref, q_ref, kp_ref, km_ref, kn_ref, vp_ref, vm_ref, vn_ref, o_ref,
                      kbuf, vbuf, *, segs, tq):
    seq, pos0 = _tile_seq_pos(pl.program_id(0), tq, segs)
    kbuf[0:BLOCK, :] = kp_ref[...]
    kbuf[BLOCK:BLOCK + tq, :] = km_ref[...]
    kbuf[BLOCK + tq:, :] = kn_ref[...]
    vbuf[0:BLOCK, :] = vp_ref[...]
    vbuf[BLOCK:BLOCK + tq, :] = vm_ref[...]
    vbuf[BLOCK + tq:, :] = vn_ref[...]

    scale = HEAD_DIM ** -0.5
    c = scale * LOG2_E
    row = lax.broadcasted_iota(jnp.int32, (BLOCK, 3 * BLOCK), 0)
    col = lax.broadcasted_iota(jnp.int32, (BLOCK, 3 * BLOCK), 1)

    def q_block(qb, carry):
        r0 = pl.multiple_of(qb * BLOCK, BLOCK)
        base = pos0 + r0 - BLOCK
        lo = jnp.maximum(row, -base)
        hi = jnp.minimum(row + 2 * WINDOW, seq - 1 - base)
        valid = (col >= lo) & (col <= hi)
        for kv in range(N_KV_HEADS):
            kc = slice(kv * HEAD_DIM, (kv + 1) * HEAD_DIM)
            q4 = jnp.concatenate(
                [q_ref[pl.ds(r0, BLOCK), (kv * Q_PER_KV + g) * HEAD_DIM:(kv * Q_PER_KV + g + 1) * HEAD_DIM]
                 for g in range(Q_PER_KV)], axis=0)
            kcat = kbuf[pl.ds(r0, 3 * BLOCK), kc]
            vcat = vbuf[pl.ds(r0, 3 * BLOCK), kc]
            s4 = lax.dot_general(q4, kcat, (((1,), (1,)), ((), ())), preferred_element_type=F32)
            ps, inv = [], []
            for g in range(Q_PER_KV):
                s = jnp.where(valid, s4[g * BLOCK:(g + 1) * BLOCK], NEG)
                sink = sink_ref[0, kv * Q_PER_KV + g] * (1.0 / scale)
                m = jnp.maximum(jnp.max(s, axis=-1, keepdims=True), sink)
                p = jnp.exp2((s - m) * c)
                inv.append(1.0 / (jnp.sum(p, axis=-1, keepdims=True) + jnp.exp2((sink - m) * c)))
                ps.append(p.astype(BF16))
            o = jnp.dot(jnp.concatenate(ps, axis=0), vcat, preferred_element_type=F32)
            for g in range(Q_PER_KV):
                hc = (kv * Q_PER_KV + g) * HEAD_DIM
                o_ref[pl.ds(r0, BLOCK), hc:hc + HEAD_DIM] = (o[g * BLOCK:(g + 1) * BLOCK] * inv[g]).astype(BF16)
        return carry

    lax.fori_loop(0, tq // BLOCK, q_block, 0)


def _band_attn(u, sink, segs, tq):
    T = u.shape[0]
    nb = tq // BLOCK
    last_blk = T // BLOCK - 1
    kcol = COL_K // KV_WIDTH
    vcol = COL_V // KV_WIDTH
    prev_map = lambda c: (lambda t: (jnp.maximum(t * nb - 1, 0), c))
    next_map = lambda c: (lambda t: (jnp.minimum((t + 1) * nb, last_blk), c))
    main_map = lambda c: (lambda t: (t, c))
    kern = functools.partial(_band_attn_kernel, segs=segs, tq=tq)
    return pl.pallas_call(
        kern,
        grid=(T // tq,),
        in_specs=[
            pl.BlockSpec(memory_space=pltpu.SMEM),
            pl.BlockSpec((tq, ATTN_WIDTH), lambda t: (t, COL_Q // ATTN_WIDTH)),
            pl.BlockSpec((BLOCK, KV_WIDTH), prev_map(kcol)),
            pl.BlockSpec((tq, KV_WIDTH), main_map(kcol)),
            pl.BlockSpec((BLOCK, KV_WIDTH), next_map(kcol)),
            pl.BlockSpec((BLOCK, KV_WIDTH), prev_map(vcol)),
            pl.BlockSpec((tq, KV_WIDTH), main_map(vcol)),
            pl.BlockSpec((BLOCK, KV_WIDTH), next_map(vcol)),
        ],
        out_specs=pl.BlockSpec((tq, ATTN_WIDTH), lambda t: (t, 0)),
        out_shape=jax.ShapeDtypeStruct((T, ATTN_WIDTH), BF16),
        scratch_shapes=[pltpu.VMEM((tq + 2 * BLOCK, KV_WIDTH), BF16),
                        pltpu.VMEM((tq + 2 * BLOCK, KV_WIDTH), BF16)],
        compiler_params=_params(40),
        name="band_attn",
    )(sink, u, u, u, u, u, u, u)


def _mix_out_kernel(x1_ref, x2_ref, xpp_ref, xpm_ref, xpn_ref, at_ref, gp_ref, ga_ref,
                    pw_ref, ps_ref, pp_ref, ap_ref, wo_ref, g2_ref, rw_ref, rb_ref,
                    h_ref, hn_ref, ids_ref, wts_ref, xb, pin, *, segs, tm):
    t = pl.program_id(0)
    seq, pos0 = _tile_seq_pos(t, tm, segs)
    in_first = t < segs[0][0] // tm

    xb[0:POOL_HALO, :] = jnp.where(pos0 > 0, xpp_ref[...].astype(F32), 0.0)
    xb[POOL_HALO:POOL_HALO + tm, :] = xpm_ref[...].astype(F32)
    xb[POOL_HALO + tm:, :] = jnp.where(pos0 + tm < seq, xpn_ref[...].astype(F32), 0.0)
    rc = min(MIX_ROW_CHUNK, tm)
    for c in range(tm // rc):
        r0 = c * rc
        rows = slice(r0, r0 + rc)
        x = jnp.where(in_first, x1_ref[rows, :], x2_ref[rows, :])
        _mix_out_rows(r0, rc, pos0, seq, x, at_ref, gp_ref, ga_ref, pw_ref, ps_ref, pp_ref,
                      ap_ref, wo_ref, g2_ref, rw_ref, rb_ref, h_ref, hn_ref, ids_ref, wts_ref, xb, pin)


def _mix_out_rows(r0, rc, pos0, seq, x, at_ref, gp_ref, ga_ref, pw_ref, ps_ref, pp_ref, ap_ref, wo_ref,
                  g2_ref, rw_ref, rb_ref, h_ref, hn_ref, ids_ref, wts_ref, xb, pin):
    rows = slice(r0, r0 + rc)
    tpos = pos0 + r0 + lax.broadcasted_iota(jnp.int32, (rc, 1), 0)
    for g, half in enumerate(POOL_HALF):
        cols = slice(g * POOL_GC, (g + 1) * POOL_GC)
        c0 = POOL_HALO + r0
        win = xb[c0 - half:c0 - half + rc, cols]
        for d in range(-half + 1, half):
            win = win + xb[c0 + d:c0 + d + rc, cols]
        cnt = (jnp.minimum(tpos + half, seq) - jnp.maximum(tpos - half, 0)).astype(F32)
        mixed = (win / cnt - xb[c0:c0 + rc, cols]).astype(BF16)
        yg = jnp.dot(mixed, pw_ref[g], preferred_element_type=F32)
        pin[rows, cols] = (yg * ps_ref[:, cols]).astype(BF16)
    pool_out = jnp.dot(pin[rows, :], pp_ref[...], preferred_element_type=F32)

    attn_out = jnp.dot(at_ref[rows, :], ap_ref[...], preferred_element_type=F32)
    merged = (jax.nn.sigmoid(gp_ref[rows, :].astype(F32)) * pool_out
              + jax.nn.sigmoid(ga_ref[rows, :].astype(F32)) * attn_out)
    h = x + jnp.dot(merged.astype(BF16), wo_ref[...], preferred_element_type=F32)
    h_ref[rows, :] = h

    ms = jnp.mean(h * h, axis=-1, keepdims=True)
    hn = h * lax.rsqrt(ms + EPS) * g2_ref[...]
    _store_slabs(hn_ref, r0 * SLAB_PITCH, rc, hn)

    logit = jnp.dot(hn.astype(BF16), rw_ref[...], preferred_element_type=F32) + rb_ref[...]

    lane_i = lax.broadcasted_iota(jnp.int32, logit.shape, 1)
    lane = lane_i.astype(F32)
    big = float(LANES)
    is_grp = (lane_i >= N_EXPERTS) & (lane_i < N_EXPERTS + N_GROUPS)
    mg = jnp.max(jnp.where(is_grp, logit, -jnp.inf), axis=-1, keepdims=True)
    grp = jnp.min(jnp.where(is_grp & (logit == mg), lane - N_EXPERTS, big), axis=-1, keepdims=True)
    p_grp = 1.0 / jnp.sum(jnp.where(is_grp, jnp.exp(logit - mg), 0.0), axis=-1, keepdims=True)

    grp_of_lane = (lane_i // EXPERTS_PER_GROUP).astype(F32)
    in_grp = (lane_i < N_EXPERTS) & (grp_of_lane == grp)
    m1 = jnp.max(jnp.where(in_grp, logit, -jnp.inf), axis=-1, keepdims=True)
    i1 = jnp.min(jnp.where(in_grp & (logit == m1), lane, big), axis=-1, keepdims=True)
    rest = in_grp & (lane != i1)
    m2 = jnp.max(jnp.where(rest, logit, -jnp.inf), axis=-1, keepdims=True)
    i2 = jnp.min(jnp.where(rest & (logit == m2), lane, big), axis=-1, keepdims=True)
    e2 = jnp.exp(m2 - m1)
    den = 1.0 + e2
    ids_ref[rows, :] = jnp.where(lane_i == 0, i1, jnp.where(lane_i == 1, i2, 0.0)).astype(jnp.int32)
    wts_ref[rows, :] = jnp.where(lane_i == 0, p_grp * (1.0 / den),
                                 jnp.where(lane_i == 1, p_grp * (e2 / den), 0.0))


def _mix_out(x1, x2, u, attn, pool_w, pool_scale, pool_proj, attn_proj, w_out, g2, rw, rb, segs, tm):
    T = u.shape[0]
    n1 = segs[0][0] // tm
    nh = tm // POOL_HALO
    last_halo = T // POOL_HALO - 1
    pcol = COL_P // POOL_WIDTH
    kern = functools.partial(_mix_out_kernel, segs=segs, tm=tm)
    row_spec = lambda w: pl.BlockSpec((tm, w), lambda t: (t, 0))
    return pl.pallas_call(
        kern,
        grid=(T // tm,),
        in_specs=_split_specs(tm, D_MODEL, n1) + [
            pl.BlockSpec((POOL_HALO, POOL_WIDTH), lambda t: (jnp.maximum(t * nh - 1, 0), pcol)),
            pl.BlockSpec((tm, POOL_WIDTH), lambda t: (t, pcol)),
            pl.BlockSpec((POOL_HALO, POOL_WIDTH), lambda t: (jnp.minimum((t + 1) * nh, last_halo), pcol)),
            row_spec(ATTN_WIDTH),
            pl.BlockSpec((tm, D_MODEL), lambda t: (t, COL_GP // D_MODEL)),
            pl.BlockSpec((tm, D_MODEL), lambda t: (t, COL_GA // D_MODEL)),
            _resident((POOL_GROUPS, POOL_GC, POOL_GC)),
            _resident((1, POOL_WIDTH)),
            _resident((POOL_WIDTH, D_MODEL)),
            _resident((ATTN_WIDTH, D_MODEL)),
            _resident((D_MODEL, D_MODEL)),
            _resident((1, D_MODEL)),
            _resident((D_MODEL, LANES)),
            _resident((1, LANES)),
        ],
        out_specs=[row_spec(D_MODEL), pl.BlockSpec((tm * SLAB_PITCH, LANES), lambda t: (t, 0)),
                   row_spec(LANES), row_spec(LANES)],
        out_shape=[jax.ShapeDtypeStruct((T, D_MODEL), F32),
                   jax.ShapeDtypeStruct((T * SLAB_PITCH, LANES), U32),
                   jax.ShapeDtypeStruct((T, LANES), jnp.int32),
                   jax.ShapeDtypeStruct((T, LANES), F32)],
        scratch_shapes=[pltpu.VMEM((tm + 2 * POOL_HALO, POOL_WIDTH), F32),
                        pltpu.VMEM((tm, POOL_WIDTH), BF16)],
        compiler_params=_params(56),
        name="mix_out",
    )(x1, x2, u, u, u, attn, u, u, pool_w, pool_scale, pool_proj, attn_proj, w_out, g2, rw, rb)


def _combine_kernel(h_ref, wts_ref, o0_ref, o1_ref, y1_ref, y2_ref, *, tk, n1):
    t = pl.program_id(0)
    w = wts_ref[...]
    w0 = w[:, 0:1]
    w1 = w[:, 1:2]
    o0 = _load_slabs(o0_ref, 0, tk)
    o1 = _load_slabs(o1_ref, 0, tk)

    def write(y_ref):
        for j in range(COL_PIECES):
            cols = slice(j * LANES, (j + 1) * LANES)
            y_ref[:, cols] = h_ref[:, cols] + (w0 * o0[j] + w1 * o1[j])

    pl.when(t < n1)(lambda: write(y1_ref))
    pl.when(t >= n1)(lambda: write(y2_ref))


def _combine(h, wts, o_slabs, segs, tk):
    T = h.shape[0]
    nt = T // tk
    n1 = segs[0][0] // tk
    return pl.pallas_call(
        functools.partial(_combine_kernel, tk=tk, n1=n1),
        grid=(nt,),
        in_specs=[
            pl.BlockSpec((tk, D_MODEL), lambda t: (t, 0)),
            pl.BlockSpec((tk, LANES), lambda t: (t, 0)),
            pl.BlockSpec((tk * SLAB_PITCH, LANES), lambda t: (t, 0)),
            pl.BlockSpec((tk * SLAB_PITCH, LANES), lambda t: (nt + t, 0)),
        ],
        out_specs=_split_specs(tk, D_MODEL, n1),
        out_shape=[jax.ShapeDtypeStruct((segs[0][0], D_MODEL), F32),
                   jax.ShapeDtypeStruct((segs[1][0], D_MODEL), F32)],
        compiler_params=_params(48),
        name="combine",
    )(h, wts, o_slabs, o_slabs)


def _moe_ffn_kernel(be_ref, nu_ref, nx_ref, g0_ref, gnext_ref, sprev_ref, scur_ref, hn_ref,
                    wg_hbm, wu_hbm, wd_hbm, o_ref, xbuf, obuf, wg_f32, wu_f32, wd_f32, wg_ref, wu_ref, wd_ref,
                    sem_g, sem_s, sem_w, *, tb, dummy0):
    b = pl.program_id(0)
    n_used = nu_ref[0]
    expert = be_ref[b]

    def weight_copies(e):
        return [pltpu.make_async_copy(src.at[e], dst, sem_w)
                for src, dst in ((wg_hbm, wg_f32), (wu_hbm, wu_f32), (wd_hbm, wd_f32))]

    @pl.when(b == 0)
    def _():
        for cp in weight_copies(expert):
            cp.start()

    @pl.when((b < n_used) & ((b == 0) | (expert != be_ref[jnp.maximum(b - 1, 0)])))
    def _():
        for cp in weight_copies(expert):
            cp.wait()
        for f32_ref, bf16_ref in ((wg_f32, wg_ref), (wu_f32, wu_ref), (wd_f32, wd_ref)):
            n_chunks = f32_ref.shape[0] // WEIGHT_CAST_ROWS

            def cast_chunk(ci, carry, f32_ref=f32_ref, bf16_ref=bf16_ref):
                rows = pl.ds(pl.multiple_of(ci * WEIGHT_CAST_ROWS, WEIGHT_CAST_ROWS), WEIGHT_CAST_ROWS)
                bf16_ref[rows, :] = f32_ref[rows, :].astype(BF16)
                return carry

            lax.fori_loop(0, n_chunks, cast_chunk, 0)
        nxt = nx_ref[b]

        @pl.when(nxt >= 0)
        def _():
            for cp in weight_copies(nxt):
                cp.start()
    slot = b % 2
    half = tb * SLAB_PITCH
    mine = pl.multiple_of(slot * half, 8)
    other = pl.multiple_of((1 - slot) * half, 8)

    def slab_copy(src, src_row, dst, dst_row, sem):
        return pltpu.make_async_copy(src.at[pl.ds(src_row, SLAB_PITCH)], dst.at[pl.ds(dst_row, SLAB_PITCH)], sem)

    def half_copy(src, dst, dst_row, sem):
        return pltpu.make_async_copy(src.at[pl.ds(0, half)], dst.at[pl.ds(dst_row, half)], sem)

    def start_gathers(tok_ref, base, sem):
        for r in range(tb):
            slab_copy(hn_ref, tok_ref[0, 0, r] * SLAB_PITCH, xbuf, base + r * SLAB_PITCH, sem).start(priority=r % 2)

    def start_scatters(dst_of_row, base, sem):
        for r in range(tb):
            slab_copy(obuf, base + r * SLAB_PITCH, o_ref, dst_of_row(r) * SLAB_PITCH, sem).start(priority=r % 2)

    @pl.when(b == 0)
    def _():
        obuf[...] = jnp.zeros_like(obuf)
        start_gathers(g0_ref, 0, sem_g.at[0])
        half_copy(obuf, o_ref, dummy0 * SLAB_PITCH, sem_s.at[0]).start()

    @pl.when(b < n_used)
    def _():
        half_copy(hn_ref, xbuf, mine, sem_g.at[slot]).wait()
        half_copy(obuf, o_ref, 0, sem_s.at[slot]).wait()
        start_gathers(gnext_ref, other, sem_g.at[1 - slot])
        first = b == 0
        start_scatters(lambda r: jnp.where(first, dummy0 + tb + r, sprev_ref[0, 0, r]), other, sem_s.at[1 - slot])

        x = jnp.concatenate([p.astype(BF16) for p in _load_slabs(xbuf, mine, tb)], axis=1)
        gate = jnp.dot(x, wg_ref[...], preferred_element_type=F32)
        up = jnp.dot(x, wu_ref[...], preferred_element_type=F32)
        mid = (jax.nn.silu(gate) * up).astype(BF16)
        out = jnp.dot(mid, wd_ref[...], preferred_element_type=F32)
        _store_slabs(obuf, mine, tb, out)

        @pl.when(b == n_used - 1)
        def _():
            half_copy(hn_ref, xbuf, other, sem_g.at[1 - slot]).wait()
            start_scatters(lambda r: scur_ref[0, 0, r], mine, sem_s.at[slot])
            half_copy(obuf, o_ref, 0, sem_s.at[1 - slot]).wait()
            half_copy(obuf, o_ref, 0, sem_s.at[slot]).wait()


def _moe_ffn(hn_slabs, plan, w_gate, w_up, w_down, n_tokens, tb):
    nblk = plan["gather_tok"].shape[0]
    dummy0 = TOP_K * n_tokens
    n_slabs = dummy0 + 2 * tb
    idx_spec = lambda f: pl.BlockSpec((1, 1, tb), lambda i, *_: (f(i), 0, 0), memory_space=pltpu.SMEM)
    hbm = pl.BlockSpec(memory_space=pl.ANY)
    grid_spec = pltpu.PrefetchScalarGridSpec(
        num_scalar_prefetch=3,
        grid=(nblk,),
        in_specs=[
            idx_spec(lambda i: 0),
            idx_spec(lambda i: jnp.minimum(i + 1, nblk - 1)),
            idx_spec(lambda i: jnp.maximum(i - 1, 0)),
            idx_spec(lambda i: i),
            hbm, hbm, hbm, hbm,
        ],
        out_specs=hbm,
        scratch_shapes=[pltpu.VMEM((2 * tb * SLAB_PITCH, LANES), U32),
                        pltpu.VMEM((2 * tb * SLAB_PITCH, LANES), U32),
                        pltpu.VMEM((D_MODEL, D_FF), F32),
                        pltpu.VMEM((D_MODEL, D_FF), F32),
                        pltpu.VMEM((D_FF, D_MODEL), F32),
                        pltpu.VMEM((D_MODEL, D_FF), BF16),
                        pltpu.VMEM((D_MODEL, D_FF), BF16),
                        pltpu.VMEM((D_FF, D_MODEL), BF16),
                        pltpu.SemaphoreType.DMA((2,)),
                        pltpu.SemaphoreType.DMA((2,)),
                        pltpu.SemaphoreType.DMA(())],
    )
    return pl.pallas_call(
        functools.partial(_moe_ffn_kernel, tb=tb, dummy0=dummy0),
        grid_spec=grid_spec,
        out_shape=jax.ShapeDtypeStruct((n_slabs * SLAB_PITCH, LANES), U32),
        compiler_params=_params(58),
        name="moe_ffn",
    )(plan["blk_exp"], plan["n_used"], plan["next_exp"], plan["gather_tok"], plan["gather_tok"], plan["scatter_dst"],
      plan["scatter_dst"], hn_slabs, w_gate, w_up, w_down)


def _slot_plan(ids, tb):
    T = ids.shape[0]
    M = T * TOP_K
    i32 = jnp.int32
    e = ids[:, :TOP_K].reshape(M)
    skey = jnp.sort(e * M + jnp.arange(M, dtype=i32))
    sm = skey % M
    edges = jnp.arange(N_EXPERTS + 1, dtype=i32) * M
    bounds = jnp.sum((skey[None, :] < edges[:, None]).astype(i32), axis=1)
    start = bounds[:-1]
    cnt = bounds[1:] - start
    nb = (cnt + tb - 1) // tb
    blk_end = jnp.cumsum(nb)
    n_used = blk_end[-1]
    nblk = M // tb + N_EXPERTS
    b = jnp.arange(nblk, dtype=i32)
    be = jnp.minimum(jnp.sum((blk_end[None, :] <= b[:, None]).astype(i32), axis=1), N_EXPERTS - 1)
    j = b - (blk_end[be] - nb[be])
    row0 = start[be] + j * tb
    n_valid = jnp.where(b < n_used, jnp.clip(cnt[be] - j * tb, 0, tb), 0)
    r = jnp.arange(tb, dtype=i32)[None, :]
    valid = r < n_valid[:, None]
    m = sm[jnp.minimum(row0[:, None] + r, M - 1)]
    tok = m // TOP_K
    k = m % TOP_K
    spare = TOP_K * T + (b % 2)[:, None] * tb + r
    blk_exp = jnp.where(b < n_used, be, be[jnp.maximum(n_used - 1, 0)])
    ex = jnp.arange(N_EXPERTS, dtype=i32)
    later = (ex[None, :] > ex[:, None]) & (cnt[None, :] > 0)
    next_of = jnp.min(jnp.where(later, ex[None, :], N_EXPERTS), axis=1)
    next_of = jnp.where(next_of < N_EXPERTS, next_of, -1)
    return dict(blk_exp=blk_exp.astype(i32), n_used=n_used.astype(i32).reshape(1),
                next_exp=next_of[blk_exp].astype(i32),
                gather_tok=jnp.where(valid, tok, 0).astype(i32).reshape(nblk, 1, tb),
                scatter_dst=jnp.where(valid, k * T + tok, spare).astype(i32).reshape(nblk, 1, tb))


def _rope_tables(seq):
    inv = ROPE_THETA ** (-jnp.arange(ROT_HALF, dtype=F32) / ROT_HALF)
    assert seq % ROPE_SPLIT == 0
    coarse = (jnp.arange(seq // ROPE_SPLIT, dtype=jnp.int32) * ROPE_SPLIT).astype(F32)[:, None] * inv[None, :]
    fine = jnp.arange(ROPE_SPLIT, dtype=jnp.int32).astype(F32)[:, None] * inv[None, :]
    cc, sc = jnp.cos(coarse)[:, None, :], jnp.sin(coarse)[:, None, :]
    cf, sf = jnp.cos(fine)[None, :, :], jnp.sin(fine)[None, :, :]
    cos = (cc * cf - sc * sf).reshape(seq, ROT_HALF)
    sin = (sc * cf + cc * sf).reshape(seq, ROT_HALF)
    ones = jnp.ones((seq, HEAD_DIM - ROT_DIM), F32)
    cos_t = jnp.concatenate([cos, cos, ones], axis=1)
    sin_t = jnp.concatenate([-sin, sin, 0.0 * ones], axis=1)
    return cos_t, sin_t


def kernel(x_prompt, x_sample, norm1_g, w_in, pool_w, pool_scale, pool_proj, q_norm_g, k_norm_g, sink,
           attn_proj, w_out, norm2_g, router_group_w, router_group_b, router_expert_w, router_expert_b,
           w_gate, w_up, w_down):
    assert norm1_g.shape[0] == 1, "single-layer trunk"
    (b1, s1, d), (b2, s2, _) = x_prompt.shape, x_sample.shape
    t1, t2 = b1 * s1, b2 * s2
    segs = ((t1, s1), (t2, s2))
    s_min = min(s1, s2)
    x1 = x_prompt.reshape(t1, d)
    x2 = x_sample.reshape(t2, d)

    gain_pad = jnp.ones((-QK_WIDTH % IN_TN,), F32)
    head_gain = jnp.concatenate([jnp.tile(q_norm_g[0], N_HEADS), jnp.tile(k_norm_g[0], N_KV_HEADS),
                                 gain_pad])[None, :]
    pad = LANES - N_EXPERTS - N_GROUPS
    rw = jnp.concatenate([router_expert_w[0], router_group_w[0], jnp.zeros((D_MODEL, pad), F32)], axis=1)
    rb = jnp.concatenate([router_expert_b[0], router_group_b[0], jnp.zeros((pad,), F32)])[None, :]
    cos_t, sin_t = _rope_tables(max(s1, s2))

    u = _in_proj(x1, x2, norm1_g[0][None, :], w_in[0].astype(BF16), head_gain, cos_t, sin_t, segs,
                 tm=min(1024, s_min))
    attn = _band_attn(u, sink[0][None, :], segs, tq=min(1024, s_min))
    h, hn, ids, wts = _mix_out(x1, x2, u, attn, pool_w[0].astype(BF16), pool_scale[0][None, :],
                               pool_proj[0].astype(BF16), attn_proj[0].astype(BF16), w_out[0].astype(BF16),
                               norm2_g[0][None, :], rw.astype(BF16), rb, segs, tm=min(256, s_min))
    tb = 256
    plan = _slot_plan(ids, tb)
    o_slabs = _moe_ffn(hn, plan, w_gate[0], w_up[0], w_down[0], t1 + t2, tb)
    y1, y2 = _combine(h, wts, o_slabs, segs, tk=min(512, s_min))
    return (y1.reshape(b1, s1, d), y2.reshape(b2, s2, d))
```

```python
import functools

import jax
import jax.numpy as jnp
from jax import lax
from jax.experimental import pallas as pl
from jax.experimental.pallas import tpu as pltpu

F32 = jnp.float32
BF16 = jnp.bfloat16

D_MODEL = 2048
POOL_GROUPS = 4
POOL_HALF = (1, 2, 4, 8)
POOL_WIDTH = D_MODEL // 2
POOL_GC = POOL_WIDTH // POOL_GROUPS
HEAD_DIM = 128
N_HEADS = D_MODEL // HEAD_DIM
N_KV_HEADS = N_HEADS // 4
Q_PER_KV = N_HEADS // N_KV_HEADS
ATTN_WIDTH = N_HEADS * HEAD_DIM
KV_WIDTH = N_KV_HEADS * HEAD_DIM
WINDOW = 128
BLOCK = 128
ROT_DIM = HEAD_DIM // 4
ROT_HALF = ROT_DIM // 2
ROPE_THETA = 500000.0
ROPE_SPLIT = 128
IN_WIDTH = POOL_WIDTH + ATTN_WIDTH + 2 * KV_WIDTH + 2 * D_MODEL
N_GROUPS = 4
EXPERTS_PER_GROUP = 8
N_EXPERTS = N_GROUPS * EXPERTS_PER_GROUP
TOP_K = 2
D_FF = D_MODEL // 2
EPS = 1e-6
NEG = -1e30
LOG2_E = 1.4426950408889634

LANES = 128
POOL_HALO = 16
MIX_ROW_CHUNK = 256
WEIGHT_CAST_ROWS = 128
MIB = 1024 * 1024
COL_PIECES = D_MODEL // LANES
SLAB_ROWS = COL_PIECES // 2
SLAB_PITCH = SLAB_ROWS + 1
U32 = jnp.uint32

COL_Q = 0
COL_K = COL_Q + ATTN_WIDTH
COL_V = COL_K + KV_WIDTH
COL_P = COL_V + KV_WIDTH
COL_GP = COL_P + POOL_WIDTH
COL_GA = COL_GP + D_MODEL
QK_WIDTH = ATTN_WIDTH + KV_WIDTH
IN_TN = 1024


def _params(vmem_mib, n_axes=1):
    return pltpu.CompilerParams(dimension_semantics=("arbitrary",) * n_axes,
                                vmem_limit_bytes=vmem_mib * MIB)


def _resident(shape):
    return pl.BlockSpec(shape, lambda *_: (0,) * len(shape), pipeline_mode=pl.Buffered(1))


def _tile_seq_pos(t, tile, segs):
    (t1, s1), (_, s2) = segs
    n1 = t1 // tile
    first = t < n1
    seq = jnp.where(first, s1, s2)
    pos0 = jnp.where(first, (t * tile) % s1, ((t - n1) * tile) % s2)
    return seq, pos0


def _split_specs(tile, width, n1):
    return [pl.BlockSpec((tile, width), lambda t, *_: (jnp.minimum(t, n1 - 1), 0)),
            pl.BlockSpec((tile, width), lambda t, *_: (jnp.maximum(t - n1, 0), 0))]


def _bf16_bits(x):
    return lax.bitcast_convert_type(x.astype(BF16).astype(F32), U32)


def _store_slabs(ref, base_row, n, val):
    for j in range(SLAB_ROWS):
        first = val[:, j * LANES:(j + 1) * LANES]
        second = val[:, (SLAB_ROWS + j) * LANES:(SLAB_ROWS + j + 1) * LANES]
        ref[pl.ds(base_row + j, n, stride=SLAB_PITCH), :] = _bf16_bits(first) | (_bf16_bits(second) >> 16)
    ref[pl.ds(base_row + SLAB_ROWS, n, stride=SLAB_PITCH), :] = jnp.zeros((n, LANES), U32)


def _load_slabs(ref, base_row, n):
    words = [ref[pl.ds(base_row + j, n, stride=SLAB_PITCH), :] for j in range(SLAB_ROWS)]
    first = [lax.bitcast_convert_type(w & jnp.uint32(0xFFFF0000), F32) for w in words]
    second = [lax.bitcast_convert_type(w << 16, F32) for w in words]
    return first + second


def _in_proj_kernel(x1_ref, x2_ref, g1_ref, w_ref, hg_ref, cos_ref, sin_ref, o_ref, xn_ref, *,
                    n1, row_chunk):
    i = pl.program_id(0)
    j = pl.program_id(1)

    def norm_rows(x_ref):
        x = x_ref[...]
        ms = jnp.mean(x * x, axis=-1, keepdims=True)
        xn_ref[...] = (x * lax.rsqrt(ms + EPS) * g1_ref[...]).astype(BF16)

    pl.when((j == 0) & (i < n1))(lambda: norm_rows(x1_ref))
    pl.when((j == 0) & (i >= n1))(lambda: norm_rows(x2_ref))

    tm, tn = o_ref.shape

    def chunk_dot(c):
        rows = slice(c * row_chunk, (c + 1) * row_chunk)
        return rows, jnp.dot(xn_ref[rows, :], w_ref[...], preferred_element_type=F32)

    def epilogue(n_heads):
        lane = lax.broadcasted_iota(jnp.int32, (1, HEAD_DIM), 1)
        for c in range(tm // row_chunk):
            rows, acc = chunk_dot(c)
            if n_heads:
                cos = cos_ref[rows, :]
                sin = sin_ref[rows, :]
            for h in range(n_heads):
                cols = slice(h * HEAD_DIM, (h + 1) * HEAD_DIM)
                a = acc[:, cols]
                ms = jnp.mean(a * a, axis=-1, keepdims=True)
                y = a * lax.rsqrt(ms + EPS) * hg_ref[:, cols]
                partner = jnp.where(lane < ROT_HALF,
                                    pltpu.roll(y, HEAD_DIM - ROT_HALF, 1),
                                    pltpu.roll(y, ROT_HALF, 1))
                o_ref[rows, cols] = (y * cos + partner * sin).astype(BF16)
            if n_heads * HEAD_DIM < tn:
                o_ref[rows, n_heads * HEAD_DIM:] = acc[:, n_heads * HEAD_DIM:].astype(BF16)

    n_q_blocks = ATTN_WIDTH // tn
    k_heads_in_last = (QK_WIDTH - n_q_blocks * tn) // HEAD_DIM
    pl.when(j < n_q_blocks)(lambda: epilogue(tn // HEAD_DIM))
    pl.when(j == n_q_blocks)(lambda: epilogue(k_heads_in_last))
    pl.when(j > n_q_blocks)(lambda: epilogue(0))


def _w_in_block(j):
    n_pool = POOL_WIDTH // IN_TN
    n_qkv = (ATTN_WIDTH + 2 * KV_WIDTH) // IN_TN
    return jnp.where(j < n_qkv, j + n_pool, jnp.where(j < n_qkv + n_pool, j - n_qkv, j))


def _in_proj(x1, x2, g1, w_in, head_gain, cos_t, sin_t, segs, tm):
    (t1, s1), (t2, s2) = segs
    n1 = t1 // tm
    p1 = s1 // tm
    p2 = s2 // tm
    pos_tile = lambda i, j: (jnp.where(i < n1, i % p1, (i - n1) % p2), 0)
    assert ATTN_WIDTH % IN_TN == 0 and QK_WIDTH - ATTN_WIDTH <= IN_TN
    kern = functools.partial(_in_proj_kernel, n1=n1, row_chunk=min(256, tm))
    return pl.pallas_call(
        kern,
        grid=((t1 + t2) // tm, IN_WIDTH // IN_TN),
        in_specs=_split_specs(tm, D_MODEL, n1) + [
            pl.BlockSpec((1, D_MODEL), lambda i, j: (0, 0)),
            pl.BlockSpec((D_MODEL, IN_TN), lambda i, j: (0, _w_in_block(j))),
            pl.BlockSpec((1, IN_TN), lambda i, j: (0, jnp.minimum(j, ATTN_WIDTH // IN_TN))),
            pl.BlockSpec((tm, HEAD_DIM), pos_tile),
            pl.BlockSpec((tm, HEAD_DIM), pos_tile),
        ],
        out_specs=pl.BlockSpec((tm, IN_TN), lambda i, j: (i, j)),
        out_shape=jax.ShapeDtypeStruct((t1 + t2, IN_WIDTH), BF16),
        scratch_shapes=[pltpu.VMEM((tm, D_MODEL), BF16)],
        compiler_params=_params(58, 2),
        name="in_proj",
    )(x1, x2, g1, w_in, head_gain, cos_t, sin_t)


def _band_attn_kernel(sink_ref, q_ref, kp_ref, km_ref, kn_ref, vp_ref, vm_ref, vn_ref, o_ref,
                      kbuf, vbuf, *, segs, tq):
    seq, pos0 = _tile_seq_pos(pl.program_id(0), tq, segs)
    kbuf[0:BLOCK, :] = kp_ref[...]
    kbuf[BLOCK:BLOCK + tq, :] = km_ref[...]
    kbuf[BLOCK + tq:, :] = kn_ref[...]
    vbuf[0:BLOCK, :] = vp_ref[...]
    vbuf[BLOCK:BLOCK + tq, :] = vm_ref[...]
    vbuf[BLOCK + tq:, :] = vn_ref[...]

    scale = HEAD_DIM ** -0.5
    c = scale * LOG2_E
    row = lax.broadcasted_iota(jnp.int32, (BLOCK, 3 * BLOCK), 0)
    col = lax.broadcasted_iota(jnp.int32, (BLOCK, 3 * BLOCK), 1)

    def q_block(qb, carry):
        r0 = pl.multiple_of(qb * BLOCK, BLOCK)
        base = pos0 + r0 - BLOCK
        lo = jnp.maximum(row, -base)
        hi = jnp.minimum(row + 2 * WINDOW, seq - 1 - base)
        valid = (col >= lo) & (col <= hi)
        for kv in range(N_KV_HEADS):
            kc = slice(kv * HEAD_DIM, (kv + 1) * HEAD_DIM)
            q4 = jnp.concatenate(
                [q_ref[pl.ds(r0, BLOCK), (kv * Q_PER_KV + g) * HEAD_DIM:(kv * Q_PER_KV + g + 1) * HEAD_DIM]
                 for g in range(Q_PER_KV)], axis=0)
            kcat = kbuf[pl.ds(r0, 3 * BLOCK), kc]
            vcat = vbuf[pl.ds(r0, 3 * BLOCK), kc]
            s4 = lax.dot_general(q4, kcat, (((1,), (1,)), ((), ())), preferred_element_type=F32)
            ps, inv = [], []
            for g in range(Q_PER_KV):
                s = jnp.where(valid, s4[g * BLOCK:(g + 1) * BLOCK], NEG)
                sink = sink_ref[0, kv * Q_PER_KV + g] * (1.0 / scale)
                m = jnp.maximum(jnp.max(s, axis=-1, keepdims=True), sink)
                p = jnp.exp2((s - m) * c)
                inv.append(1.0 / (jnp.sum(p, axis=-1, keepdims=True) + jnp.exp2((sink - m) * c)))
                ps.append(p.astype(BF16))
            o = jnp.dot(jnp.concatenate(ps, axis=0), vcat, preferred_element_type=F32)
            for g in range(Q_PER_KV):
                hc = (kv * Q_PER_KV + g) * HEAD_DIM
                o_ref[pl.ds(r0, BLOCK), hc:hc + HEAD_DIM] = (o[g * BLOCK:(g + 1) * BLOCK] * inv[g]).astype(BF16)
        return carry

    lax.fori_loop(0, tq // BLOCK, q_block, 0)


def _band_attn(u, sink, segs, tq):
    T = u.shape[0]
    nb = tq // BLOCK
    last_blk = T // BLOCK - 1
    kcol = COL_K // KV_WIDTH
    vcol = COL_V // KV_WIDTH
    prev_map = lambda c: (lambda t: (jnp.maximum(t * nb - 1, 0), c))
    next_map = lambda c: (lambda t: (jnp.minimum((t + 1) * nb, last_blk), c))
    main_map = lambda c: (lambda t: (t, c))
    kern = functools.partial(_band_attn_kernel, segs=segs, tq=tq)
    return pl.pallas_call(
        kern,
        grid=(T // tq,),
        in_specs=[
            pl.BlockSpec(memory_space=pltpu.SMEM),
            pl.BlockSpec((tq, ATTN_WIDTH), lambda t: (t, COL_Q // ATTN_WIDTH)),
            pl.BlockSpec((BLOCK, KV_WIDTH), prev_map(kcol)),
            pl.BlockSpec((tq, KV_WIDTH), main_map(kcol)),
            pl.BlockSpec((BLOCK, KV_WIDTH), next_map(kcol)),
            pl.BlockSpec((BLOCK, KV_WIDTH), prev_map(vcol)),
            pl.BlockSpec((tq, KV_WIDTH), main_map(vcol)),
            pl.BlockSpec((BLOCK, KV_WIDTH), next_map(vcol)),
        ],
        out_specs=pl.BlockSpec((tq, ATTN_WIDTH), lambda t: (t, 0)),
        out_shape=jax.ShapeDtypeStruct((T, ATTN_WIDTH), BF16),
        scratch_shapes=[pltpu.VMEM((tq + 2 * BLOCK, KV_WIDTH), BF16),
                        pltpu.VMEM((tq + 2 * BLOCK, KV_WIDTH), BF16)],
        compiler_params=_params(40),
        name="band_attn",
    )(sink, u, u, u, u, u, u, u)


def _mix_out_kernel(x1_ref, x2_ref, xpp_ref, xpm_ref, xpn_ref, at_ref, gp_ref, ga_ref,
                    pw_ref, ps_ref, pp_ref, ap_ref, wo_ref, g2_ref, rw_ref, rb_ref,
                    h_ref, hn_ref, ids_ref, wts_ref, xb, pin, *, segs, tm):
    t = pl.program_id(0)
    seq, pos0 = _tile_seq_pos(t, tm, segs)
    in_first = t < segs[0][0] // tm

    xb[0:POOL_HALO, :] = jnp.where(pos0 > 0, xpp_ref[...].astype(F32), 0.0)
    xb[POOL_HALO:POOL_HALO + tm, :] = xpm_ref[...].astype(F32)
    xb[POOL_HALO + tm:, :] = jnp.where(pos0 + tm < seq, xpn_ref[...].astype(F32), 0.0)
    rc = min(MIX_ROW_CHUNK, tm)
    for c in range(tm // rc):
        r0 = c * rc
        rows = slice(r0, r0 + rc)
        x = jnp.where(in_first, x1_ref[rows, :], x2_ref[rows, :])
        _mix_out_rows(r0, rc, pos0, seq, x, at_ref, gp_ref, ga_ref, pw_ref, ps_ref, pp_ref,
                      ap_ref, wo_ref, g2_ref, rw_ref, rb_ref, h_ref, hn_ref, ids_ref, wts_ref, xb, pin)


def _mix_out_rows(r0, rc, pos0, seq, x, at_ref, gp_ref, ga_ref, pw_ref, ps_ref, pp_ref, ap_ref, wo_ref,
                  g2_ref, rw_ref, rb_ref, h_ref, hn_ref, ids_ref, wts_ref, xb, pin):
    rows = slice(r0, r0 + rc)
    tpos = pos0 + r0 + lax.broadcasted_iota(jnp.int32, (rc, 1), 0)
    for g, half in enumerate(POOL_HALF):
        cols = slice(g * POOL_GC, (g + 1) * POOL_GC)
        c0 = POOL_HALO + r0
        win = xb[c0 - half:c0 - half + rc, cols]
        for d in range(-half + 1, half):
            win = win + xb[c0 + d:c0 + d + rc, cols]
        cnt = (jnp.minimum(tpos + half, seq) - jnp.maximum(tpos - half, 0)).astype(F32)
        mixed = (win / cnt - xb[c0:c0 + rc, cols]).astype(BF16)
        yg = jnp.dot(mixed, pw_ref[g], preferred_element_type=F32)
        pin[rows, cols] = (yg * ps_ref[:, cols]).astype(BF16)
    pool_out = jnp.dot(pin[rows, :], pp_ref[...], preferred_element_type=F32)

    attn_out = jnp.dot(at_ref[rows, :], ap_ref[...], preferred_element_type=F32)
    merged = (jax.nn.sigmoid(gp_ref[rows, :].astype(F32)) * pool_out
              + jax.nn.sigmoid(ga_ref[rows, :].astype(F32)) * attn_out)
    h = x + jnp.dot(merged.astype(BF16), wo_ref[...], preferred_element_type=F32)
    h_ref[rows, :] = h

    ms = jnp.mean(h * h, axis=-1, keepdims=True)
    hn = h * lax.rsqrt(ms + EPS) * g2_ref[...]
    _store_slabs(hn_ref, r0 * SLAB_PITCH, rc, hn)

    logit = jnp.dot(hn.astype(BF16), rw_ref[...], preferred_element_type=F32) + rb_ref[...]

    lane_i = lax.broadcasted_iota(jnp.int32, logit.shape, 1)
    is_grp = (lane_i >= N_EXPERTS) & (lane_i < N_EXPERTS + N_GROUPS)
    grp_logit = jnp.where(is_grp, logit, -jnp.inf)
    mg = jnp.max(grp_logit, axis=-1, keepdims=True)
    grp = jnp.argmax(grp_logit, axis=-1, keepdims=True).astype(jnp.int32) - N_EXPERTS
    p_grp = 1.0 / jnp.sum(jnp.where(is_grp, jnp.exp(logit - mg), 0.0), axis=-1, keepdims=True)

    in_grp = (lane_i < N_EXPERTS) & ((lane_i // EXPERTS_PER_GROUP) == grp)
    first_logit = jnp.where(in_grp, logit, -jnp.inf)
    m1 = jnp.max(first_logit, axis=-1, keepdims=True)
    i1 = jnp.argmax(first_logit, axis=-1, keepdims=True).astype(jnp.int32)
    second_logit = jnp.where(in_grp & (lane_i != i1), logit, -jnp.inf)
    m2 = jnp.max(second_logit, axis=-1, keepdims=True)
    i2 = jnp.argmax(second_logit, axis=-1, keepdims=True).astype(jnp.int32)
    e2 = jnp.exp(m2 - m1)
    den = 1.0 + e2
    ids_ref[rows, :] = jnp.where(lane_i == 0, i1, jnp.where(lane_i == 1, i2, 0))
    wts_ref[rows, :] = jnp.where(lane_i == 0, p_grp * (1.0 / den),
                                 jnp.where(lane_i == 1, p_grp * (e2 / den), 0.0))


def _mix_out(x1, x2, u, attn, pool_w, pool_scale, pool_proj, attn_proj, w_out, g2, rw, rb, segs, tm):
    T = u.shape[0]
    n1 = segs[0][0] // tm
    nh = tm // POOL_HALO
    last_halo = T // POOL_HALO - 1
    pcol = COL_P // POOL_WIDTH
    kern = functools.partial(_mix_out_kernel, segs=segs, tm=tm)
    row_spec = lambda w: pl.BlockSpec((tm, w), lambda t: (t, 0))
    return pl.pallas_call(
        kern,
        grid=(T // tm,),
        in_specs=_split_specs(tm, D_MODEL, n1) + [
            pl.BlockSpec((POOL_HALO, POOL_WIDTH), lambda t: (jnp.maximum(t * nh - 1, 0), pcol)),
            pl.BlockSpec((tm, POOL_WIDTH), lambda t: (t, pcol)),
            pl.BlockSpec((POOL_HALO, POOL_WIDTH), lambda t: (jnp.minimum((t + 1) * nh, last_halo), pcol)),
            row_spec(ATTN_WIDTH),
            pl.BlockSpec((tm, D_MODEL), lambda t: (t, COL_GP // D_MODEL)),
            pl.BlockSpec((tm, D_MODEL), lambda t: (t, COL_GA // D_MODEL)),
            _resident((POOL_GROUPS, POOL_GC, POOL_GC)),
            _resident((1, POOL_WIDTH)),
            _resident((POOL_WIDTH, D_MODEL)),
            _resident((ATTN_WIDTH, D_MODEL)),
            _resident((D_MODEL, D_MODEL)),
            _resident((1, D_MODEL)),
            _resident((D_MODEL, LANES)),
            _resident((1, LANES)),
        ],
        out_specs=[row_spec(D_MODEL), pl.BlockSpec((tm * SLAB_PITCH, LANES), lambda t: (t, 0)),
                   row_spec(LANES), row_spec(LANES)],
        out_shape=[jax.ShapeDtypeStruct((T, D_MODEL), F32),
                   jax.ShapeDtypeStruct((T * SLAB_PITCH, LANES), U32),
                   jax.ShapeDtypeStruct((T, LANES), jnp.int32),
                   jax.ShapeDtypeStruct((T, LANES), F32)],
        scratch_shapes=[pltpu.VMEM((tm + 2 * POOL_HALO, POOL_WIDTH), F32),
                        pltpu.VMEM((tm, POOL_WIDTH), BF16)],
        compiler_params=_params(56),
        name="mix_out",
    )(x1, x2, u, u, u, attn, u, u, pool_w, pool_scale, pool_proj, attn_proj, w_out, g2, rw, rb)


def _combine_kernel(h_ref, wts_ref, o0_ref, o1_ref, y1_ref, y2_ref, *, tk, n1):
    t = pl.program_id(0)
    w = wts_ref[...]
    w0 = w[:, 0:1]
    w1 = w[:, 1:2]
    o0 = _load_slabs(o0_ref, 0, tk)
    o1 = _load_slabs(o1_ref, 0, tk)

    def write(y_ref):
        for j in range(COL_PIECES):
            cols = slice(j * LANES, (j + 1) * LANES)
            y_ref[:, cols] = h_ref[:, cols] + (w0 * o0[j] + w1 * o1[j])

    pl.when(t < n1)(lambda: write(y1_ref))
    pl.when(t >= n1)(lambda: write(y2_ref))


def _combine(h, wts, o_slabs, segs, tk):
    T = h.shape[0]
    nt = T // tk
    n1 = segs[0][0] // tk
    return pl.pallas_call(
        functools.partial(_combine_kernel, tk=tk, n1=n1),
        grid=(nt,),
        in_specs=[
            pl.BlockSpec((tk, D_MODEL), lambda t: (t, 0)),
            pl.BlockSpec((tk, LANES), lambda t: (t, 0)),
            pl.BlockSpec((tk * SLAB_PITCH, LANES), lambda t: (t, 0)),
            pl.BlockSpec((tk * SLAB_PITCH, LANES), lambda t: (nt + t, 0)),
        ],
        out_specs=_split_specs(tk, D_MODEL, n1),
        out_shape=[jax.ShapeDtypeStruct((segs[0][0], D_MODEL), F32),
                   jax.ShapeDtypeStruct((segs[1][0], D_MODEL), F32)],
        compiler_params=_params(48),
        name="combine",
    )(h, wts, o_slabs, o_slabs)


def _moe_ffn_kernel(be_ref, nu_ref, nx_ref, g0_ref, gnext_ref, sprev_ref, scur_ref, hn_ref,
                    wg_hbm, wu_hbm, wd_hbm, o_ref, xbuf, obuf, wg_f32, wu_f32, wd_f32, wg_ref, wu_ref, wd_ref,
                    sem_g, sem_s, sem_w, *, tb, dummy0):
    b = pl.program_id(0)
    n_used = nu_ref[0]
    expert = be_ref[b]

    def weight_copies(e):
        return [pltpu.make_async_copy(src.at[e], dst, sem_w)
                for src, dst in ((wg_hbm, wg_f32), (wu_hbm, wu_f32), (wd_hbm, wd_f32))]

    @pl.when(b == 0)
    def _():
        for cp in weight_copies(expert):
            cp.start()

    @pl.when((b < n_used) & ((b == 0) | (expert != be_ref[jnp.maximum(b - 1, 0)])))
    def _():
        for cp in weight_copies(expert):
            cp.wait()
        for f32_ref, bf16_ref in ((wg_f32, wg_ref), (wu_f32, wu_ref), (wd_f32, wd_ref)):
            n_chunks = f32_ref.shape[0] // WEIGHT_CAST_ROWS

            def cast_chunk(ci, carry, f32_ref=f32_ref, bf16_ref=bf16_ref):
                rows = pl.ds(pl.multiple_of(ci * WEIGHT_CAST_ROWS, WEIGHT_CAST_ROWS), WEIGHT_CAST_ROWS)
                bf16_ref[rows, :] = f32_ref[rows, :].astype(BF16)
                return carry

            lax.fori_loop(0, n_chunks, cast_chunk, 0)
        nxt = nx_ref[b]

        @pl.when(nxt >= 0)
        def _():
            for cp in weight_copies(nxt):
                cp.start()
    slot = b % 2
    half = tb * SLAB_PITCH
    mine = pl.multiple_of(slot * half, 8)
    other = pl.multiple_of((1 - slot) * half, 8)

    def slab_copy(src, src_row, dst, dst_row, sem):
        return pltpu.make_async_copy(src.at[pl.ds(src_row, SLAB_PITCH)], dst.at[pl.ds(dst_row, SLAB_PITCH)], sem)

    def half_copy(src, dst, dst_row, sem):
        return pltpu.make_async_copy(src.at[pl.ds(0, half)], dst.at[pl.ds(dst_row, half)], sem)

    def start_gathers(tok_ref, base, sem):
        for r in range(tb):
            slab_copy(hn_ref, tok_ref[0, 0, r] * SLAB_PITCH, xbuf, base + r * SLAB_PITCH, sem).start(priority=r % 2)

    def start_scatters(dst_of_row, base, sem):
        for r in range(tb):
            slab_copy(obuf, base + r * SLAB_PITCH, o_ref, dst_of_row(r) * SLAB_PITCH, sem).start(priority=r % 2)

    @pl.when(b == 0)
    def _():
        obuf[...] = jnp.zeros_like(obuf)
        start_gathers(g0_ref, 0, sem_g.at[0])
        half_copy(obuf, o_ref, dummy0 * SLAB_PITCH, sem_s.at[0]).start()

    @pl.when(b < n_used)
    def _():
        half_copy(hn_ref, xbuf, mine, sem_g.at[slot]).wait()
        half_copy(obuf, o_ref, 0, sem_s.at[slot]).wait()
        start_gathers(gnext_ref, other, sem_g.at[1 - slot])
        first = b == 0
        start_scatters(lambda r: jnp.where(first, dummy0 + tb + r, sprev_ref[0, 0, r]), other, sem_s.at[1 - slot])

        x = jnp.concatenate([p.astype(BF16) for p in _load_slabs(xbuf, mine, tb)], axis=1)
        gate = jnp.dot(x, wg_ref[...], preferred_element_type=F32)
        up = jnp.dot(x, wu_ref[...], preferred_element_type=F32)
        mid = (jax.nn.silu(gate) * up).astype(BF16)
        out = jnp.dot(mid, wd_ref[...], preferred_element_type=F32)
        _store_slabs(obuf, mine, tb, out)

        @pl.when(b == n_used - 1)
        def _():
            half_copy(hn_ref, xbuf, other, sem_g.at[1 - slot]).wait()
            start_scatters(lambda r: scur_ref[0, 0, r], mine, sem_s.at[slot])
            half_copy(obuf, o_ref, 0, sem_s.at[1 - slot]).wait()
            half_copy(obuf, o_ref, 0, sem_s.at[slot]).wait()


def _moe_ffn(hn_slabs, plan, w_gate, w_up, w_down, n_tokens, tb):
    nblk = plan["gather_tok"].shape[0]
    dummy0 = TOP_K * n_tokens
    n_slabs = dummy0 + 2 * tb
    idx_spec = lambda f: pl.BlockSpec((1, 1, tb), lambda i, *_: (f(i), 0, 0), memory_space=pltpu.SMEM)
    hbm = pl.BlockSpec(memory_space=pl.ANY)
    grid_spec = pltpu.PrefetchScalarGridSpec(
        num_scalar_prefetch=3,
        grid=(nblk,),
        in_specs=[
            idx_spec(lambda i: 0),
            idx_spec(lambda i: jnp.minimum(i + 1, nblk - 1)),
            idx_spec(lambda i: jnp.maximum(i - 1, 0)),
            idx_spec(lambda i: i),
            hbm, hbm, hbm, hbm,
        ],
        out_specs=hbm,
        scratch_shapes=[pltpu.VMEM((2 * tb * SLAB_PITCH, LANES), U32),
                        pltpu.VMEM((2 * tb * SLAB_PITCH, LANES), U32),
                        pltpu.VMEM((D_MODEL, D_FF), F32),
                        pltpu.VMEM((D_MODEL, D_FF), F32),
                        pltpu.VMEM((D_FF, D_MODEL), F32),
                        pltpu.VMEM((D_MODEL, D_FF), BF16),
                        pltpu.VMEM((D_MODEL, D_FF), BF16),
                        pltpu.VMEM((D_FF, D_MODEL), BF16),
                        pltpu.SemaphoreType.DMA((2,)),
                        pltpu.SemaphoreType.DMA((2,)),
                        pltpu.SemaphoreType.DMA(())],
    )
    return pl.pallas_call(
        functools.partial(_moe_ffn_kernel, tb=tb, dummy0=dummy0),
        grid_spec=grid_spec,
        out_shape=jax.ShapeDtypeStruct((n_slabs * SLAB_PITCH, LANES), U32),
        compiler_params=_params(58),
        name="moe_ffn",
    )(plan["blk_exp"], plan["n_used"], plan["next_exp"], plan["gather_tok"], plan["gather_tok"], plan["scatter_dst"],
      plan["scatter_dst"], hn_slabs, w_gate, w_up, w_down)


def _slot_plan(ids, tb):
    T = ids.shape[0]
    M = T * TOP_K
    i32 = jnp.int32
    e = ids[:, :TOP_K].reshape(M)
    skey = jnp.sort(e * M + jnp.arange(M, dtype=i32))
    sm = skey % M
    edges = jnp.arange(N_EXPERTS + 1, dtype=i32) * M
    bounds = jnp.sum((skey[None, :] < edges[:, None]).astype(i32), axis=1)
    start = bounds[:-1]
    cnt = bounds[1:] - start
    nb = (cnt + tb - 1) // tb
    blk_end = jnp.cumsum(nb)
    n_used = blk_end[-1]
    nblk = M // tb + N_EXPERTS
    b = jnp.arange(nblk, dtype=i32)
    be = jnp.minimum(jnp.sum((blk_end[None, :] <= b[:, None]).astype(i32), axis=1), N_EXPERTS - 1)
    j = b - (blk_end[be] - nb[be])
    row0 = start[be] + j * tb
    n_valid = jnp.where(b < n_used, jnp.clip(cnt[be] - j * tb, 0, tb), 0)
    r = jnp.arange(tb, dtype=i32)[None, :]
    valid = r < n_valid[:, None]
    m = sm[jnp.minimum(row0[:, None] + r, M - 1)]
    tok = m // TOP_K
    k = m % TOP_K
    spare = TOP_K * T + (b % 2)[:, None] * tb + r
    blk_exp = jnp.where(b < n_used, be, be[jnp.maximum(n_used - 1, 0)])
    ex = jnp.arange(N_EXPERTS, dtype=i32)
    later = (ex[None, :] > ex[:, None]) & (cnt[None, :] > 0)
    next_of = jnp.min(jnp.where(later, ex[None, :], N_EXPERTS), axis=1)
    next_of = jnp.where(next_of < N_EXPERTS, next_of, -1)
    return dict(blk_exp=blk_exp.astype(i32), n_used=n_used.astype(i32).reshape(1),
                next_exp=next_of[blk_exp].astype(i32),
                gather_tok=jnp.where(valid, tok, 0).astype(i32).reshape(nblk, 1, tb),
                scatter_dst=jnp.where(valid, k * T + tok, spare).astype(i32).reshape(nblk, 1, tb))


def _rope_tables(seq):
    inv = ROPE_THETA ** (-jnp.arange(ROT_HALF, dtype=F32) / ROT_HALF)
    assert seq % ROPE_SPLIT == 0
    coarse = (jnp.arange(seq // ROPE_SPLIT, dtype=jnp.int32) * ROPE_SPLIT).astype(F32)[:, None] * inv[None, :]
    fine = jnp.arange(ROPE_SPLIT, dtype=jnp.int32).astype(F32)[:, None] * inv[None, :]
    cc, sc = jnp.cos(coarse)[:, None, :], jnp.sin(coarse)[:, None, :]
    cf, sf = jnp.cos(fine)[None, :, :], jnp.sin(fine)[None, :, :]
    cos = (cc * cf - sc * sf).reshape(seq, ROT_HALF)
    sin = (sc * cf + cc * sf).reshape(seq, ROT_HALF)
    ones = jnp.ones((seq, HEAD_DIM - ROT_DIM), F32)
    cos_t = jnp.concatenate([cos, cos, ones], axis=1)
    sin_t = jnp.concatenate([-sin, sin, 0.0 * ones], axis=1)
    return cos_t, sin_t


def kernel(x_prompt, x_sample, norm1_g, w_in, pool_w, pool_scale, pool_proj, q_norm_g, k_norm_g, sink,
           attn_proj, w_out, norm2_g, router_group_w, router_group_b, router_expert_w, router_expert_b,
           w_gate, w_up, w_down):
    assert norm1_g.shape[0] == 1, "single-layer trunk"
    (b1, s1, d), (b2, s2, _) = x_prompt.shape, x_sample.shape
    t1, t2 = b1 * s1, b2 * s2
    segs = ((t1, s1), (t2, s2))
    s_min = min(s1, s2)
    x1 = x_prompt.reshape(t1, d)
    x2 = x_sample.reshape(t2, d)

    gain_pad = jnp.ones((-QK_WIDTH % IN_TN,), F32)
    head_gain = jnp.concatenate([jnp.tile(q_norm_g[0], N_HEADS), jnp.tile(k_norm_g[0], N_KV_HEADS),
                                 gain_pad])[None, :]
    pad = LANES - N_EXPERTS - N_GROUPS
    rw = jnp.concatenate([router_expert_w[0], router_group_w[0], jnp.zeros((D_MODEL, pad), F32)], axis=1)
    rb = jnp.concatenate([router_expert_b[0], router_group_b[0], jnp.zeros((pad,), F32)])[None, :]
    cos_t, sin_t = _rope_tables(max(s1, s2))

    u = _in_proj(x1, x2, norm1_g[0][None, :], w_in[0].astype(BF16), head_gain, cos_t, sin_t, segs,
                 tm=min(1024, s_min))
    attn = _band_attn(u, sink[0][None, :], segs, tq=min(1024, s_min))
    h, hn, ids, wts = _mix_out(x1, x2, u, attn, pool_w[0].astype(BF16), pool_scale[0][None, :],
                               pool_proj[0].astype(BF16), attn_proj[0].astype(BF16), w_out[0].astype(BF16),
                               norm2_g[0][None, :], rw.astype(BF16), rb, segs, tm=min(256, s_min))
    tb = 256
    plan = _slot_plan(ids, tb)
    o_slabs = _moe_ffn(hn, plan, w_gate[0], w_up[0], w_down[0], t1 + t2, tb)
    y1, y2 = _combine(h, wts, o_slabs, segs, tk=min(512, s_min))
    return (y1.reshape(b1, s1, d), y2.reshape(b2, s2, d))
```

```python
import functools

import jax
import jax.numpy as jnp
from jax import lax
from jax.experimental import pallas as pl
from jax.experimental.pallas import tpu as pltpu

F32 = jnp.float32
BF16 = jnp.bfloat16

D_MODEL = 2048
POOL_GROUPS = 4
POOL_HALF = (1, 2, 4, 8)
POOL_WIDTH = D_MODEL // 2
POOL_GC = POOL_WIDTH // POOL_GROUPS
HEAD_DIM = 128
N_HEADS = D_MODEL // HEAD_DIM
N_KV_HEADS = N_HEADS // 4
Q_PER_KV = N_HEADS // N_KV_HEADS
ATTN_WIDTH = N_HEADS * HEAD_DIM
KV_WIDTH = N_KV_HEADS * HEAD_DIM
WINDOW = 128
BLOCK = 128
ROT_DIM = HEAD_DIM // 4
ROT_HALF = ROT_DIM // 2
ROPE_THETA = 500000.0
ROPE_SPLIT = 128
IN_WIDTH = POOL_WIDTH + ATTN_WIDTH + 2 * KV_WIDTH + 2 * D_MODEL
N_GROUPS = 4
EXPERTS_PER_GROUP = 8
N_EXPERTS = N_GROUPS * EXPERTS_PER_GROUP
TOP_K = 2
D_FF = D_MODEL // 2
EPS = 1e-6
NEG = -1e30
LOG2_E = 1.4426950408889634

LANES = 128
POOL_HALO = 16
MIX_ROW_CHUNK = 256
WEIGHT_CAST_ROWS = 128
MIB = 1024 * 1024
COL_PIECES = D_MODEL // LANES
SLAB_ROWS = COL_PIECES // 2
SLAB_PITCH = SLAB_ROWS + 1
U32 = jnp.uint32

COL_Q = 0
COL_K = COL_Q + ATTN_WIDTH
COL_V = COL_K + KV_WIDTH
COL_P = COL_V + KV_WIDTH
COL_GP = COL_P + POOL_WIDTH
COL_GA = COL_GP + D_MODEL
QK_WIDTH = ATTN_WIDTH + KV_WIDTH
IN_TN = 1024


TOKEN_TILE = dict(in_proj=1024, band_attn=1024, mix_out=256, moe_ffn=256, combine=512)
VMEM_MIB = dict(in_proj=58, band_attn=40, mix_out=56, moe_ffn=58, combine=48)


def _params(name, n_axes=1):
    return pltpu.CompilerParams(dimension_semantics=("arbitrary",) * n_axes,
                                vmem_limit_bytes=VMEM_MIB[name] * MIB)


def _resident(shape):
    return pl.BlockSpec(shape, lambda *_: (0,) * len(shape), pipeline_mode=pl.Buffered(1))


def _tile_seq_pos(t, tile, segs):
    (t1, s1), (_, s2) = segs
    n1 = t1 // tile
    first = t < n1
    seq = jnp.where(first, s1, s2)
    pos0 = jnp.where(first, (t * tile) % s1, ((t - n1) * tile) % s2)
    return seq, pos0


def _split_specs(tile, width, n1):
    return [pl.BlockSpec((tile, width), lambda t, *_: (jnp.minimum(t, n1 - 1), 0)),
            pl.BlockSpec((tile, width), lambda t, *_: (jnp.maximum(t - n1, 0), 0))]


def _bf16_bits(x):
    return lax.bitcast_convert_type(x.astype(BF16).astype(F32), U32)


def _store_slabs(ref, base_row, n, val):
    for j in range(SLAB_ROWS):
        first = val[:, j * LANES:(j + 1) * LANES]
        second = val[:, (SLAB_ROWS + j) * LANES:(SLAB_ROWS + j + 1) * LANES]
        ref[pl.ds(base_row + j, n, stride=SLAB_PITCH), :] = _bf16_bits(first) | (_bf16_bits(second) >> 16)
    ref[pl.ds(base_row + SLAB_ROWS, n, stride=SLAB_PITCH), :] = jnp.zeros((n, LANES), U32)


def _load_slabs(ref, base_row, n):
    words = [ref[pl.ds(base_row + j, n, stride=SLAB_PITCH), :] for j in range(SLAB_ROWS)]
    first = [lax.bitcast_convert_type(w & jnp.uint32(0xFFFF0000), F32) for w in words]
    second = [lax.bitcast_convert_type(w << 16, F32) for w in words]
    return first + second


def _in_proj_kernel(x1_ref, x2_ref, g1_ref, w_ref, hg_ref, cos_ref, sin_ref, o_ref, xn_ref, *,
                    n1, row_chunk):
    i = pl.program_id(0)
    j = pl.program_id(1)

    def norm_rows(x_ref):
        x = x_ref[...]
        ms = jnp.mean(x * x, axis=-1, keepdims=True)
        xn_ref[...] = (x * lax.rsqrt(ms + EPS) * g1_ref[...]).astype(BF16)

    pl.when((j == 0) & (i < n1))(lambda: norm_rows(x1_ref))
    pl.when((j == 0) & (i >= n1))(lambda: norm_rows(x2_ref))

    tm, tn = o_ref.shape

    def chunk_dot(c):
        rows = slice(c * row_chunk, (c + 1) * row_chunk)
        return rows, jnp.dot(xn_ref[rows, :], w_ref[...], preferred_element_type=F32)

    def epilogue(n_heads):
        lane = lax.broadcasted_iota(jnp.int32, (1, HEAD_DIM), 1)
        for c in range(tm // row_chunk):
            rows, acc = chunk_dot(c)
            if n_heads:
                cos = cos_ref[rows, :]
                sin = sin_ref[rows, :]
            for h in range(n_heads):
                cols = slice(h * HEAD_DIM, (h + 1) * HEAD_DIM)
                a = acc[:, cols]
                ms = jnp.mean(a * a, axis=-1, keepdims=True)
                y = a * lax.rsqrt(ms + EPS) * hg_ref[:, cols]
                partner = jnp.where(lane < ROT_HALF,
                                    pltpu.roll(y, HEAD_DIM - ROT_HALF, 1),
                                    pltpu.roll(y, ROT_HALF, 1))
                o_ref[rows, cols] = (y * cos + partner * sin).astype(BF16)
            if n_heads * HEAD_DIM < tn:
                o_ref[rows, n_heads * HEAD_DIM:] = acc[:, n_heads * HEAD_DIM:].astype(BF16)

    n_q_blocks = ATTN_WIDTH // tn
    k_heads_in_last = (QK_WIDTH - n_q_blocks * tn) // HEAD_DIM
    pl.when(j < n_q_blocks)(lambda: epilogue(tn // HEAD_DIM))
    pl.when(j == n_q_blocks)(lambda: epilogue(k_heads_in_last))
    pl.when(j > n_q_blocks)(lambda: epilogue(0))


def _w_in_block(j):
    n_pool = POOL_WIDTH // IN_TN
    n_qkv = (ATTN_WIDTH + 2 * KV_WIDTH) // IN_TN
    return jnp.where(j < n_qkv, j + n_pool, jnp.where(j < n_qkv + n_pool, j - n_qkv, j))


def _in_proj(x1, x2, g1, w_in, head_gain, cos_t, sin_t, segs, tm):
    (t1, s1), (t2, s2) = segs
    n1 = t1 // tm
    p1 = s1 // tm
    p2 = s2 // tm
    pos_tile = lambda i, j: (jnp.where(i < n1, i % p1, (i - n1) % p2), 0)
    assert ATTN_WIDTH % IN_TN == 0 and QK_WIDTH - ATTN_WIDTH <= IN_TN
    kern = functools.partial(_in_proj_kernel, n1=n1, row_chunk=min(256, tm))
    return pl.pallas_call(
        kern,
        grid=((t1 + t2) // tm, IN_WIDTH // IN_TN),
        in_specs=_split_specs(tm, D_MODEL, n1) + [
            pl.BlockSpec((1, D_MODEL), lambda i, j: (0, 0)),
            pl.BlockSpec((D_MODEL, IN_TN), lambda i, j: (0, _w_in_block(j))),
            pl.BlockSpec((1, IN_TN), lambda i, j: (0, jnp.minimum(j, ATTN_WIDTH // IN_TN))),
            pl.BlockSpec((tm, HEAD_DIM), pos_tile),
            pl.BlockSpec((tm, HEAD_DIM), pos_tile),
        ],
        out_specs=pl.BlockSpec((tm, IN_TN), lambda i, j: (i, j)),
        out_shape=jax.ShapeDtypeStruct((t1 + t2, IN_WIDTH), BF16),
        scratch_shapes=[pltpu.VMEM((tm, D_MODEL), BF16)],
        compiler_params=_params("in_proj", 2),
        name="in_proj",
    )(x1, x2, g1, w_in, head_gain, cos_t, sin_t)


def _band_attn_kernel(sink_ref, q_ref, kp_ref, km_ref, kn_ref, vp_ref, vm_ref, vn_ref, o_ref,
                      kbuf, vbuf, *, segs, tq):
    seq, pos0 = _tile_seq_pos(pl.program_id(0), tq, segs)
    kbuf[0:BLOCK, :] = kp_ref[...]
    kbuf[BLOCK:BLOCK + tq, :] = km_ref[...]
    kbuf[BLOCK + tq:, :] = kn_ref[...]
    vbuf[0:BLOCK, :] = vp_ref[...]
    vbuf[BLOCK:BLOCK + tq, :] = vm_ref[...]
    vbuf[BLOCK + tq:, :] = vn_ref[...]

    scale = HEAD_DIM ** -0.5
    c = scale * LOG2_E
    row = lax.broadcasted_iota(jnp.int32, (BLOCK, 3 * BLOCK), 0)
    col = lax.broadcasted_iota(jnp.int32, (BLOCK, 3 * BLOCK), 1)

    def q_block(qb, carry):
        r0 = pl.multiple_of(qb * BLOCK, BLOCK)
        base = pos0 + r0 - BLOCK
        lo = jnp.maximum(row, -base)
        hi = jnp.minimum(row + 2 * WINDOW, seq - 1 - base)
        bias = jnp.where((col >= lo) & (col <= hi), 0.0, NEG)
        for kv in range(N_KV_HEADS):
            kc = slice(kv * HEAD_DIM, (kv + 1) * HEAD_DIM)
            q4 = jnp.concatenate(
                [q_ref[pl.ds(r0, BLOCK), (kv * Q_PER_KV + g) * HEAD_DIM:(kv * Q_PER_KV + g + 1) * HEAD_DIM]
                 for g in range(Q_PER_KV)], axis=0)
            kcat = kbuf[pl.ds(r0, 3 * BLOCK), kc]
            vcat = vbuf[pl.ds(r0, 3 * BLOCK), kc]
            s4 = lax.dot_general(q4, kcat, (((1,), (1,)), ((), ())), preferred_element_type=F32)
            ps, inv = [], []
            for g in range(Q_PER_KV):
                s = s4[g * BLOCK:(g + 1) * BLOCK] + bias
                sink = sink_ref[0, kv * Q_PER_KV + g] * (1.0 / scale)
                m = jnp.maximum(jnp.max(s, axis=-1, keepdims=True), sink)
                p = jnp.exp2((s - m) * c)
                inv.append(1.0 / (jnp.sum(p, axis=-1, keepdims=True) + jnp.exp2((sink - m) * c)))
                ps.append(p.astype(BF16))
            o = jnp.dot(jnp.concatenate(ps, axis=0), vcat, preferred_element_type=F32)
            for g in range(Q_PER_KV):
                hc = (kv * Q_PER_KV + g) * HEAD_DIM
                o_ref[pl.ds(r0, BLOCK), hc:hc + HEAD_DIM] = (o[g * BLOCK:(g + 1) * BLOCK] * inv[g]).astype(BF16)
        return carry

    lax.fori_loop(0, tq // BLOCK, q_block, 0)


def _band_attn(u, sink, segs, tq):
    T = u.shape[0]
    nb = tq // BLOCK
    last_blk = T // BLOCK - 1
    kcol = COL_K // KV_WIDTH
    vcol = COL_V // KV_WIDTH
    prev_map = lambda c: (lambda t: (jnp.maximum(t * nb - 1, 0), c))
    next_map = lambda c: (lambda t: (jnp.minimum((t + 1) * nb, last_blk), c))
    main_map = lambda c: (lambda t: (t, c))
    kern = functools.partial(_band_attn_kernel, segs=segs, tq=tq)
    return pl.pallas_call(
        kern,
        grid=(T // tq,),
        in_specs=[
            pl.BlockSpec(memory_space=pltpu.SMEM),
            pl.BlockSpec((tq, ATTN_WIDTH), lambda t: (t, COL_Q // ATTN_WIDTH)),
            pl.BlockSpec((BLOCK, KV_WIDTH), prev_map(kcol)),
            pl.BlockSpec((tq, KV_WIDTH), main_map(kcol)),
            pl.BlockSpec((BLOCK, KV_WIDTH), next_map(kcol)),
            pl.BlockSpec((BLOCK, KV_WIDTH), prev_map(vcol)),
            pl.BlockSpec((tq, KV_WIDTH), main_map(vcol)),
            pl.BlockSpec((BLOCK, KV_WIDTH), next_map(vcol)),
        ],
        out_specs=pl.BlockSpec((tq, ATTN_WIDTH), lambda t: (t, 0)),
        out_shape=jax.ShapeDtypeStruct((T, ATTN_WIDTH), BF16),
        scratch_shapes=[pltpu.VMEM((tq + 2 * BLOCK, KV_WIDTH), BF16),
                        pltpu.VMEM((tq + 2 * BLOCK, KV_WIDTH), BF16)],
        compiler_params=_params("band_attn"),
        name="band_attn",
    )(sink, u, u, u, u, u, u, u)


def _mix_out_kernel(x1_ref, x2_ref, xpp_ref, xpm_ref, xpn_ref, at_ref, gp_ref, ga_ref,
                    pw_ref, ps_ref, pp_ref, ap_ref, wo_ref, g2_ref, rw_ref, rb_ref,
                    h_ref, hn_ref, ids_ref, wts_ref, xb, pin, *, segs, tm):
    t = pl.program_id(0)
    seq, pos0 = _tile_seq_pos(t, tm, segs)
    in_first = t < segs[0][0] // tm

    xb[0:POOL_HALO, :] = jnp.where(pos0 > 0, xpp_ref[...].astype(F32), 0.0)
    xb[POOL_HALO:POOL_HALO + tm, :] = xpm_ref[...].astype(F32)
    xb[POOL_HALO + tm:, :] = jnp.where(pos0 + tm < seq, xpn_ref[...].astype(F32), 0.0)
    rc = min(MIX_ROW_CHUNK, tm)
    for c in range(tm // rc):
        r0 = c * rc
        rows = slice(r0, r0 + rc)
        x = jnp.where(in_first, x1_ref[rows, :], x2_ref[rows, :])
        _mix_out_rows(r0, rc, pos0, seq, x, at_ref, gp_ref, ga_ref, pw_ref, ps_ref, pp_ref,
                      ap_ref, wo_ref, g2_ref, rw_ref, rb_ref, h_ref, hn_ref, ids_ref, wts_ref, xb, pin)


def _mix_out_rows(r0, rc, pos0, seq, x, at_ref, gp_ref, ga_ref, pw_ref, ps_ref, pp_ref, ap_ref, wo_ref,
                  g2_ref, rw_ref, rb_ref, h_ref, hn_ref, ids_ref, wts_ref, xb, pin):
    rows = slice(r0, r0 + rc)
    tpos = pos0 + r0 + lax.broadcasted_iota(jnp.int32, (rc, 1), 0)
    for g, half in enumerate(POOL_HALF):
        cols = slice(g * POOL_GC, (g + 1) * POOL_GC)
        c0 = POOL_HALO + r0
        win = xb[c0 - half:c0 - half + rc, cols]
        for d in range(-half + 1, half):
            win = win + xb[c0 + d:c0 + d + rc, cols]
        cnt = (jnp.minimum(tpos + half, seq) - jnp.maximum(tpos - half, 0)).astype(F32)
        mixed = (win / cnt - xb[c0:c0 + rc, cols]).astype(BF16)
        yg = jnp.dot(mixed, pw_ref[g], preferred_element_type=F32)
        pin[rows, cols] = (yg * ps_ref[:, cols]).astype(BF16)
    pool_out = jnp.dot(pin[rows, :], pp_ref[...], preferred_element_type=F32)

    attn_out = jnp.dot(at_ref[rows, :], ap_ref[...], preferred_element_type=F32)
    merged = (jax.nn.sigmoid(gp_ref[rows, :].astype(F32)) * pool_out
              + jax.nn.sigmoid(ga_ref[rows, :].astype(F32)) * attn_out)
    h = x + jnp.dot(merged.astype(BF16), wo_ref[...], preferred_element_type=F32)
    h_ref[rows, :] = h

    ms = jnp.mean(h * h, axis=-1, keepdims=True)
    hn = h * lax.rsqrt(ms + EPS) * g2_ref[...]
    _store_slabs(hn_ref, r0 * SLAB_PITCH, rc, hn)

    logit = jnp.dot(hn.astype(BF16), rw_ref[...], preferred_element_type=F32) + rb_ref[...]

    lane_i = lax.broadcasted_iota(jnp.int32, logit.shape, 1)
    lane = lane_i.astype(F32)
    big = float(LANES)
    is_grp = (lane_i >= N_EXPERTS) & (lane_i < N_EXPERTS + N_GROUPS)
    mg = jnp.max(jnp.where(is_grp, logit, -jnp.inf), axis=-1, keepdims=True)
    grp = jnp.min(jnp.where(is_grp & (logit == mg), lane - N_EXPERTS, big), axis=-1, keepdims=True)
    p_grp = 1.0 / jnp.sum(jnp.where(is_grp, jnp.exp(logit - mg), 0.0), axis=-1, keepdims=True)

    grp_of_lane = (lane_i // EXPERTS_PER_GROUP).astype(F32)
    in_grp = (lane_i < N_EXPERTS) & (grp_of_lane == grp)
    m1 = jnp.max(jnp.where(in_grp, logit, -jnp.inf), axis=-1, keepdims=True)
    i1 = jnp.min(jnp.where(in_grp & (logit == m1), lane, big), axis=-1, keepdims=True)
    rest = in_grp & (lane != i1)
    m2 = jnp.max(jnp.where(rest, logit, -jnp.inf), axis=-1, keepdims=True)
    i2 = jnp.min(jnp.where(rest & (logit == m2), lane, big), axis=-1, keepdims=True)
    e2 = jnp.exp(m2 - m1)
    den = 1.0 + e2
    ids_ref[rows, :] = jnp.where(lane_i == 0, i1, jnp.where(lane_i == 1, i2, 0.0)).astype(jnp.int32)
    wts_ref[rows, :] = jnp.where(lane_i == 0, p_grp * (1.0 / den),
                                 jnp.where(lane_i == 1, p_grp * (e2 / den), 0.0))


def _mix_out(x1, x2, u, attn, pool_w, pool_scale, pool_proj, attn_proj, w_out, g2, rw, rb, segs, tm):
    T = u.shape[0]
    n1 = segs[0][0] // tm
    nh = tm // POOL_HALO
    last_halo = T // POOL_HALO - 1
    pcol = COL_P // POOL_WIDTH
    kern = functools.partial(_mix_out_kernel, segs=segs, tm=tm)
    row_spec = lambda w: pl.BlockSpec((tm, w), lambda t: (t, 0))
    return pl.pallas_call(
        kern,
        grid=(T // tm,),
        in_specs=_split_specs(tm, D_MODEL, n1) + [
            pl.BlockSpec((POOL_HALO, POOL_WIDTH), lambda t: (jnp.maximum(t * nh - 1, 0), pcol)),
            pl.BlockSpec((tm, POOL_WIDTH), lambda t: (t, pcol)),
            pl.BlockSpec((POOL_HALO, POOL_WIDTH), lambda t: (jnp.minimum((t + 1) * nh, last_halo), pcol)),
            row_spec(ATTN_WIDTH),
            pl.BlockSpec((tm, D_MODEL), lambda t: (t, COL_GP // D_MODEL)),
            pl.BlockSpec((tm, D_MODEL), lambda t: (t, COL_GA // D_MODEL)),
            _resident((POOL_GROUPS, POOL_GC, POOL_GC)),
            _resident((1, POOL_WIDTH)),
            _resident((POOL_WIDTH, D_MODEL)),
            _resident((ATTN_WIDTH, D_MODEL)),
            _resident((D_MODEL, D_MODEL)),
            _resident((1, D_MODEL)),
            _resident((D_MODEL, LANES)),
            _resident((1, LANES)),
        ],
        out_specs=[row_spec(D_MODEL), pl.BlockSpec((tm * SLAB_PITCH, LANES), lambda t: (t, 0)),
                   row_spec(LANES), row_spec(LANES)],
        out_shape=[jax.ShapeDtypeStruct((T, D_MODEL), F32),
                   jax.ShapeDtypeStruct((T * SLAB_PITCH, LANES), U32),
                   jax.ShapeDtypeStruct((T, LANES), jnp.int32),
                   jax.ShapeDtypeStruct((T, LANES), F32)],
        scratch_shapes=[pltpu.VMEM((tm + 2 * POOL_HALO, POOL_WIDTH), F32),
                        pltpu.VMEM((tm, POOL_WIDTH), BF16)],
        compiler_params=_params("mix_out"),
        name="mix_out",
    )(x1, x2, u, u, u, attn, u, u, pool_w, pool_scale, pool_proj, attn_proj, w_out, g2, rw, rb)


def _combine_kernel(h_ref, wts_ref, o0_ref, o1_ref, y1_ref, y2_ref, *, tk, n1):
    t = pl.program_id(0)
    w = wts_ref[...]
    w0 = w[:, 0:1]
    w1 = w[:, 1:2]
    o0 = _load_slabs(o0_ref, 0, tk)
    o1 = _load_slabs(o1_ref, 0, tk)

    def write(y_ref):
        for j in range(COL_PIECES):
            cols = slice(j * LANES, (j + 1) * LANES)
            y_ref[:, cols] = h_ref[:, cols] + (w0 * o0[j] + w1 * o1[j])

    pl.when(t < n1)(lambda: write(y1_ref))
    pl.when(t >= n1)(lambda: write(y2_ref))


def _combine(h, wts, o_slabs, segs, tk):
    T = h.shape[0]
    nt = T // tk
    n1 = segs[0][0] // tk
    return pl.pallas_call(
        functools.partial(_combine_kernel, tk=tk, n1=n1),
        grid=(nt,),
        in_specs=[
            pl.BlockSpec((tk, D_MODEL), lambda t: (t, 0)),
            pl.BlockSpec((tk, LANES), lambda t: (t, 0)),
            pl.BlockSpec((tk * SLAB_PITCH, LANES), lambda t: (t, 0)),
            pl.BlockSpec((tk * SLAB_PITCH, LANES), lambda t: (nt + t, 0)),
        ],
        out_specs=_split_specs(tk, D_MODEL, n1),
        out_shape=[jax.ShapeDtypeStruct((segs[0][0], D_MODEL), F32),
                   jax.ShapeDtypeStruct((segs[1][0], D_MODEL), F32)],
        compiler_params=_params("combine"),
        name="combine",
    )(h, wts, o_slabs, o_slabs)


def _moe_ffn_kernel(be_ref, nu_ref, nx_ref, g0_ref, gnext_ref, sprev_ref, scur_ref, hn_ref,
                    wg_hbm, wu_hbm, wd_hbm, o_ref, xbuf, obuf, wg_f32, wu_f32, wd_f32, wg_ref, wu_ref, wd_ref,
                    sem_g, sem_s, sem_w, *, tb, dummy0):
    b = pl.program_id(0)
    n_used = nu_ref[0]
    expert = be_ref[b]

    def weight_copies(e):
        return [pltpu.make_async_copy(src.at[e], dst, sem_w)
                for src, dst in ((wg_hbm, wg_f32), (wu_hbm, wu_f32), (wd_hbm, wd_f32))]

    @pl.when(b == 0)
    def _():
        for cp in weight_copies(expert):
            cp.start()

    @pl.when((b < n_used) & ((b == 0) | (expert != be_ref[jnp.maximum(b - 1, 0)])))
    def _():
        for cp in weight_copies(expert):
            cp.wait()
        for f32_ref, bf16_ref in ((wg_f32, wg_ref), (wu_f32, wu_ref), (wd_f32, wd_ref)):
            n_chunks = f32_ref.shape[0] // WEIGHT_CAST_ROWS

            def cast_chunk(ci, carry, f32_ref=f32_ref, bf16_ref=bf16_ref):
                rows = pl.ds(pl.multiple_of(ci * WEIGHT_CAST_ROWS, WEIGHT_CAST_ROWS), WEIGHT_CAST_ROWS)
                bf16_ref[rows, :] = f32_ref[rows, :].astype(BF16)
                return carry

            lax.fori_loop(0, n_chunks, cast_chunk, 0)
        nxt = nx_ref[b]

        @pl.when(nxt >= 0)
        def _():
            for cp in weight_copies(nxt):
                cp.start()
    slot = b % 2
    half = tb * SLAB_PITCH
    mine = pl.multiple_of(slot * half, 8)
    other = pl.multiple_of((1 - slot) * half, 8)

    def slab_copy(src, src_row, dst, dst_row, sem):
        return pltpu.make_async_copy(src.at[pl.ds(src_row, SLAB_PITCH)], dst.at[pl.ds(dst_row, SLAB_PITCH)], sem)

    def half_copy(src, dst, dst_row, sem):
        return pltpu.make_async_copy(src.at[pl.ds(0, half)], dst.at[pl.ds(dst_row, half)], sem)

    def start_gathers(tok_ref, base, sem):
        for r in range(tb):
            slab_copy(hn_ref, tok_ref[0, 0, r] * SLAB_PITCH, xbuf, base + r * SLAB_PITCH, sem).start(priority=r % 2)

    def start_scatters(dst_of_row, base, sem):
        for r in range(tb):
            slab_copy(obuf, base + r * SLAB_PITCH, o_ref, dst_of_row(r) * SLAB_PITCH, sem).start(priority=r % 2)

    @pl.when(b == 0)
    def _():
        obuf[...] = jnp.zeros_like(obuf)
        start_gathers(g0_ref, 0, sem_g.at[0])
        half_copy(obuf, o_ref, dummy0 * SLAB_PITCH, sem_s.at[0]).start()

    @pl.when(b < n_used)
    def _():
        half_copy(hn_ref, xbuf, mine, sem_g.at[slot]).wait()
        half_copy(obuf, o_ref, 0, sem_s.at[slot]).wait()
        start_gathers(gnext_ref, other, sem_g.at[1 - slot])
        first = b == 0
        start_scatters(lambda r: jnp.where(first, dummy0 + tb + r, sprev_ref[0, 0, r]), other, sem_s.at[1 - slot])

        x = jnp.concatenate([p.astype(BF16) for p in _load_slabs(xbuf, mine, tb)], axis=1)
        gate = jnp.dot(x, wg_ref[...], preferred_element_type=F32)
        up = jnp.dot(x, wu_ref[...], preferred_element_type=F32)
        mid = (jax.nn.silu(gate) * up).astype(BF16)
        out = jnp.dot(mid, wd_ref[...], preferred_element_type=F32)
        _store_slabs(obuf, mine, tb, out)

        @pl.when(b == n_used - 1)
        def _():
            half_copy(hn_ref, xbuf, other, sem_g.at[1 - slot]).wait()
            start_scatters(lambda r: scur_ref[0, 0, r], mine, sem_s.at[slot])
            half_copy(obuf, o_ref, 0, sem_s.at[1 - slot]).wait()
            half_copy(obuf, o_ref, 0, sem_s.at[slot]).wait()


def _moe_ffn(hn_slabs, plan, w_gate, w_up, w_down, n_tokens, tb):
    nblk = plan["gather_tok"].shape[0]
    dummy0 = TOP_K * n_tokens
    n_slabs = dummy0 + 2 * tb
    idx_spec = lambda f: pl.BlockSpec((1, 1, tb), lambda i, *_: (f(i), 0, 0), memory_space=pltpu.SMEM)
    hbm = pl.BlockSpec(memory_space=pl.ANY)
    grid_spec = pltpu.PrefetchScalarGridSpec(
        num_scalar_prefetch=3,
        grid=(nblk,),
        in_specs=[
            idx_spec(lambda i: 0),
            idx_spec(lambda i: jnp.minimum(i + 1, nblk - 1)),
            idx_spec(lambda i: jnp.maximum(i - 1, 0)),
            idx_spec(lambda i: i),
            hbm, hbm, hbm, hbm,
        ],
        out_specs=hbm,
        scratch_shapes=[pltpu.VMEM((2 * tb * SLAB_PITCH, LANES), U32),
                        pltpu.VMEM((2 * tb * SLAB_PITCH, LANES), U32),
                        pltpu.VMEM((D_MODEL, D_FF), F32),
                        pltpu.VMEM((D_MODEL, D_FF), F32),
                        pltpu.VMEM((D_FF, D_MODEL), F32),
                        pltpu.VMEM((D_MODEL, D_FF), BF16),
                        pltpu.VMEM((D_MODEL, D_FF), BF16),
                        pltpu.VMEM((D_FF, D_MODEL), BF16),
                        pltpu.SemaphoreType.DMA((2,)),
                        pltpu.SemaphoreType.DMA((2,)),
                        pltpu.SemaphoreType.DMA(())],
    )
    return pl.pallas_call(
        functools.partial(_moe_ffn_kernel, tb=tb, dummy0=dummy0),
        grid_spec=grid_spec,
        out_shape=jax.ShapeDtypeStruct((n_slabs * SLAB_PITCH, LANES), U32),
        compiler_params=_params("moe_ffn"),
        name="moe_ffn",
    )(plan["blk_exp"], plan["n_used"], plan["next_exp"], plan["gather_tok"], plan["gather_tok"], plan["scatter_dst"],
      plan["scatter_dst"], hn_slabs, w_gate, w_up, w_down)


def _slot_plan(ids, tb):
    T = ids.shape[0]
    M = T * TOP_K
    i32 = jnp.int32
    e = ids[:, :TOP_K].reshape(M)
    skey = jnp.sort(e * M + jnp.arange(M, dtype=i32))
    sm = skey % M
    edges = jnp.arange(N_EXPERTS + 1, dtype=i32) * M
    bounds = jnp.sum((skey[None, :] < edges[:, None]).astype(i32), axis=1)
    start = bounds[:-1]
    cnt = bounds[1:] - start
    nb = (cnt + tb - 1) // tb
    blk_end = jnp.cumsum(nb)
    n_used = blk_end[-1]
    nblk = M // tb + N_EXPERTS
    b = jnp.arange(nblk, dtype=i32)
    be = jnp.minimum(jnp.sum((blk_end[None, :] <= b[:, None]).astype(i32), axis=1), N_EXPERTS - 1)
    j = b - (blk_end[be] - nb[be])
    row0 = start[be] + j * tb
    n_valid = jnp.where(b < n_used, jnp.clip(cnt[be] - j * tb, 0, tb), 0)
    r = jnp.arange(tb, dtype=i32)[None, :]
    valid = r < n_valid[:, None]
    m = sm[jnp.minimum(row0[:, None] + r, M - 1)]
    tok = m // TOP_K
    k = m % TOP_K
    spare = TOP_K * T + (b % 2)[:, None] * tb + r
    blk_exp = jnp.where(b < n_used, be, be[jnp.maximum(n_used - 1, 0)])
    ex = jnp.arange(N_EXPERTS, dtype=i32)
    later = (ex[None, :] > ex[:, None]) & (cnt[None, :] > 0)
    next_of = jnp.min(jnp.where(later, ex[None, :], N_EXPERTS), axis=1)
    next_of = jnp.where(next_of < N_EXPERTS, next_of, -1)
    return dict(blk_exp=blk_exp.astype(i32), n_used=n_used.astype(i32).reshape(1),
                next_exp=next_of[blk_exp].astype(i32),
                gather_tok=jnp.where(valid, tok, 0).astype(i32).reshape(nblk, 1, tb),
                scatter_dst=jnp.where(valid, k * T + tok, spare).astype(i32).reshape(nblk, 1, tb))


def _rope_tables(seq):
    inv = ROPE_THETA ** (-jnp.arange(ROT_HALF, dtype=F32) / ROT_HALF)
    assert seq % ROPE_SPLIT == 0
    coarse = (jnp.arange(seq // ROPE_SPLIT, dtype=jnp.int32) * ROPE_SPLIT).astype(F32)[:, None] * inv[None, :]
    fine = jnp.arange(ROPE_SPLIT, dtype=jnp.int32).astype(F32)[:, None] * inv[None, :]
    cc, sc = jnp.cos(coarse)[:, None, :], jnp.sin(coarse)[:, None, :]
    cf, sf = jnp.cos(fine)[None, :, :], jnp.sin(fine)[None, :, :]
    cos = (cc * cf - sc * sf).reshape(seq, ROT_HALF)
    sin = (sc * cf + cc * sf).reshape(seq, ROT_HALF)
    ones = jnp.ones((seq, HEAD_DIM - ROT_DIM), F32)
    cos_t = jnp.concatenate([cos, cos, ones], axis=1)
    sin_t = jnp.concatenate([-sin, sin, 0.0 * ones], axis=1)
    return cos_t, sin_t


def kernel(x_prompt, x_sample, norm1_g, w_in, pool_w, pool_scale, pool_proj, q_norm_g, k_norm_g, sink,
           attn_proj, w_out, norm2_g, router_group_w, router_group_b, router_expert_w, router_expert_b,
           w_gate, w_up, w_down):
    assert norm1_g.shape[0] == 1, "single-layer trunk"
    (b1, s1, d), (b2, s2, _) = x_prompt.shape, x_sample.shape
    t1, t2 = b1 * s1, b2 * s2
    segs = ((t1, s1), (t2, s2))
    s_min = min(s1, s2)
    x1 = x_prompt.reshape(t1, d)
    x2 = x_sample.reshape(t2, d)

    gain_pad = jnp.ones((-QK_WIDTH % IN_TN,), F32)
    head_gain = jnp.concatenate([jnp.tile(q_norm_g[0], N_HEADS), jnp.tile(k_norm_g[0], N_KV_HEADS),
                                 gain_pad])[None, :]
    pad = LANES - N_EXPERTS - N_GROUPS
    rw = jnp.concatenate([router_expert_w[0], router_group_w[0], jnp.zeros((D_MODEL, pad), F32)], axis=1)
    rb = jnp.concatenate([router_expert_b[0], router_group_b[0], jnp.zeros((pad,), F32)])[None, :]
    cos_t, sin_t = _rope_tables(max(s1, s2))

    tile = {name: min(rows, s_min) for name, rows in TOKEN_TILE.items()}
    assert all(s1 % t == 0 and s2 % t == 0 for t in tile.values()), "token tiles must divide both sequences"

    u = _in_proj(x1, x2, norm1_g[0][None, :], w_in[0].astype(BF16), head_gain, cos_t, sin_t, segs,
                 tm=tile["in_proj"])
    attn = _band_attn(u, sink[0][None, :], segs, tq=tile["band_attn"])
    h, hn, ids, wts = _mix_out(x1, x2, u, attn, pool_w[0].astype(BF16), pool_scale[0][None, :],
                               pool_proj[0].astype(BF16), attn_proj[0].astype(BF16), w_out[0].astype(BF16),
                               norm2_g[0][None, :], rw.astype(BF16), rb, segs, tm=tile["mix_out"])
    tb = TOKEN_TILE["moe_ffn"]
    plan = _slot_plan(ids, tb)
    o_slabs = _moe_ffn(hn, plan, w_gate[0], w_up[0], w_down[0], t1 + t2, tb)
    y1, y2 = _combine(h, wts, o_slabs, segs, tk=tile["combine"])
    return (y1.reshape(b1, s1, d), y2.reshape(b2, s2, d))
```

```python
import functools

import jax
import jax.numpy as jnp
from jax import lax
from jax.experimental import pallas as pl
from jax.experimental.pallas import tpu as pltpu

F32 = jnp.float32
BF16 = jnp.bfloat16

D_MODEL = 2048
POOL_GROUPS = 4
POOL_HALF = (1, 2, 4, 8)
POOL_WIDTH = D_MODEL // 2
POOL_GC = POOL_WIDTH // POOL_GROUPS
HEAD_DIM = 128
N_HEADS = D_MODEL // HEAD_DIM
N_KV_HEADS = N_HEADS // 4
Q_PER_KV = N_HEADS // N_KV_HEADS
ATTN_WIDTH = N_HEADS * HEAD_DIM
KV_WIDTH = N_KV_HEADS * HEAD_DIM
WINDOW = 128
BLOCK = 128
ROT_DIM = HEAD_DIM // 4
ROT_HALF = ROT_DIM // 2
ROPE_THETA = 500000.0
ROPE_SPLIT = 128
IN_WIDTH = POOL_WIDTH + ATTN_WIDTH + 2 * KV_WIDTH + 2 * D_MODEL
N_GROUPS = 4
EXPERTS_PER_GROUP = 8
N_EXPERTS = N_GROUPS * EXPERTS_PER_GROUP
TOP_K = 2
D_FF = D_MODEL // 2
EPS = 1e-6
NEG = -1e30
LOG2_E = 1.4426950408889634

LANES = 128
POOL_HALO = 16
MIX_ROW_CHUNK = 256
WEIGHT_CAST_ROWS = 128
MIB = 1024 * 1024
COL_PIECES = D_MODEL // LANES
SLAB_ROWS = COL_PIECES // 2
SLAB_PITCH = SLAB_ROWS + 1
U32 = jnp.uint32

COL_Q = 0
COL_K = COL_Q + ATTN_WIDTH
COL_V = COL_K + KV_WIDTH
COL_P = COL_V + KV_WIDTH
COL_GP = COL_P + POOL_WIDTH
COL_GA = COL_GP + D_MODEL
QK_WIDTH = ATTN_WIDTH + KV_WIDTH
IN_TN = 1024


TOKEN_TILE = dict(in_proj=1024, band_attn=1024, mix_out=256, moe_ffn=256, combine=512)
VMEM_MIB = dict(in_proj=58, band_attn=40, mix_out=56, moe_ffn=58, combine=48)


def _params(name, n_axes=1):
    return pltpu.CompilerParams(dimension_semantics=("arbitrary",) * n_axes,
                                vmem_limit_bytes=VMEM_MIB[name] * MIB)


def _resident(shape):
    return pl.BlockSpec(shape, lambda *_: (0,) * len(shape), pipeline_mode=pl.Buffered(1))


def _tile_seq_pos(t, tile, segs):
    (t1, s1), (_, s2) = segs
    n1 = t1 // tile
    first = t < n1
    seq = jnp.where(first, s1, s2)
    pos0 = jnp.where(first, (t * tile) % s1, ((t - n1) * tile) % s2)
    return seq, pos0


def _split_specs(tile, width, n1):
    return [pl.BlockSpec((tile, width), lambda t, *_: (jnp.minimum(t, n1 - 1), 0)),
            pl.BlockSpec((tile, width), lambda t, *_: (jnp.maximum(t - n1, 0), 0))]


def _bf16_bits(x):
    return lax.bitcast_convert_type(x.astype(BF16).astype(F32), U32)


def _store_slabs(ref, base_row, n, val):
    for j in range(SLAB_ROWS):
        first = val[:, j * LANES:(j + 1) * LANES]
        second = val[:, (SLAB_ROWS + j) * LANES:(SLAB_ROWS + j + 1) * LANES]
        ref[pl.ds(base_row + j, n, stride=SLAB_PITCH), :] = _bf16_bits(first) | (_bf16_bits(second) >> 16)
    ref[pl.ds(base_row + SLAB_ROWS, n, stride=SLAB_PITCH), :] = jnp.zeros((n, LANES), U32)


def _load_slabs(ref, base_row, n):
    words = [ref[pl.ds(base_row + j, n, stride=SLAB_PITCH), :] for j in range(SLAB_ROWS)]
    first = [lax.bitcast_convert_type(w & jnp.uint32(0xFFFF0000), F32) for w in words]
    second = [lax.bitcast_convert_type(w << 16, F32) for w in words]
    return first + second


def _in_proj_kernel(x1_ref, x2_ref, g1_ref, w_ref, hg_ref, cos_ref, sin_ref, o_ref, xn_ref, *,
                    n1, row_chunk):
    i = pl.program_id(0)
    j = pl.program_id(1)

    def norm_rows(x_ref):
        x = x_ref[...]
        ms = jnp.mean(x * x, axis=-1, keepdims=True)
        xn_ref[...] = (x * lax.rsqrt(ms + EPS) * g1_ref[...]).astype(BF16)

    pl.when((j == 0) & (i < n1))(lambda: norm_rows(x1_ref))
    pl.when((j == 0) & (i >= n1))(lambda: norm_rows(x2_ref))

    tm, tn = o_ref.shape

    def chunk_dot(c):
        rows = slice(c * row_chunk, (c + 1) * row_chunk)
        return rows, jnp.dot(xn_ref[rows, :], w_ref[...], preferred_element_type=F32)

    def epilogue(n_heads):
        lane = lax.broadcasted_iota(jnp.int32, (1, HEAD_DIM), 1)
        for c in range(tm // row_chunk):
            rows, acc = chunk_dot(c)
            if n_heads:
                cos = cos_ref[rows, :]
                sin = sin_ref[rows, :]
            for h in range(n_heads):
                cols = slice(h * HEAD_DIM, (h + 1) * HEAD_DIM)
                a = acc[:, cols]
                ms = jnp.mean(a * a, axis=-1, keepdims=True)
                y = a * lax.rsqrt(ms + EPS) * hg_ref[:, cols]
                partner = jnp.where(lane < ROT_HALF,
                                    pltpu.roll(y, HEAD_DIM - ROT_HALF, 1),
                                    pltpu.roll(y, ROT_HALF, 1))
                o_ref[rows, cols] = (y * cos + partner * sin).astype(BF16)
            if n_heads * HEAD_DIM < tn:
                o_ref[rows, n_heads * HEAD_DIM:] = acc[:, n_heads * HEAD_DIM:].astype(BF16)

    n_q_blocks = ATTN_WIDTH // tn
    k_heads_in_last = (QK_WIDTH - n_q_blocks * tn) // HEAD_DIM
    pl.when(j < n_q_blocks)(lambda: epilogue(tn // HEAD_DIM))
    pl.when(j == n_q_blocks)(lambda: epilogue(k_heads_in_last))
    pl.when(j > n_q_blocks)(lambda: epilogue(0))


def _w_in_block(j):
    n_pool = POOL_WIDTH // IN_TN
    n_qkv = (ATTN_WIDTH + 2 * KV_WIDTH) // IN_TN
    return jnp.where(j < n_qkv, j + n_pool, jnp.where(j < n_qkv + n_pool, j - n_qkv, j))


def _in_proj(x1, x2, g1, w_in, head_gain, cos_t, sin_t, segs, tm):
    (t1, s1), (t2, s2) = segs
    n1 = t1 // tm
    p1 = s1 // tm
    p2 = s2 // tm
    pos_tile = lambda i, j: (jnp.where(i < n1, i % p1, (i - n1) % p2), 0)
    assert ATTN_WIDTH % IN_TN == 0 and QK_WIDTH - ATTN_WIDTH <= IN_TN
    kern = functools.partial(_in_proj_kernel, n1=n1, row_chunk=min(256, tm))
    return pl.pallas_call(
        kern,
        grid=((t1 + t2) // tm, IN_WIDTH // IN_TN),
        in_specs=_split_specs(tm, D_MODEL, n1) + [
            pl.BlockSpec((1, D_MODEL), lambda i, j: (0, 0)),
            pl.BlockSpec((D_MODEL, IN_TN), lambda i, j: (0, _w_in_block(j))),
            pl.BlockSpec((1, IN_TN), lambda i, j: (0, jnp.minimum(j, ATTN_WIDTH // IN_TN))),
            pl.BlockSpec((tm, HEAD_DIM), pos_tile),
            pl.BlockSpec((tm, HEAD_DIM), pos_tile),
        ],
        out_specs=pl.BlockSpec((tm, IN_TN), lambda i, j: (i, j)),
        out_shape=jax.ShapeDtypeStruct((t1 + t2, IN_WIDTH), BF16),
        scratch_shapes=[pltpu.VMEM((tm, D_MODEL), BF16)],
        compiler_params=_params("in_proj", 2),
        name="in_proj",
    )(x1, x2, g1, w_in, head_gain, cos_t, sin_t)


def _band_attn_kernel(sink_ref, q_ref, kp_ref, km_ref, kn_ref, vp_ref, vm_ref, vn_ref, o_ref,
                      kbuf, vbuf, *, segs, tq):
    seq, pos0 = _tile_seq_pos(pl.program_id(0), tq, segs)
    kbuf[0:BLOCK, :] = kp_ref[...]
    kbuf[BLOCK:BLOCK + tq, :] = km_ref[...]
    kbuf[BLOCK + tq:, :] = kn_ref[...]
    vbuf[0:BLOCK, :] = vp_ref[...]
    vbuf[BLOCK:BLOCK + tq, :] = vm_ref[...]
    vbuf[BLOCK + tq:, :] = vn_ref[...]

    scale = HEAD_DIM ** -0.5
    c = scale * LOG2_E
    row = lax.broadcasted_iota(jnp.int32, (BLOCK, 3 * BLOCK), 0)
    col = lax.broadcasted_iota(jnp.int32, (BLOCK, 3 * BLOCK), 1)

    def q_block(qb, carry):
        r0 = pl.multiple_of(qb * BLOCK, BLOCK)
        base = pos0 + r0 - BLOCK
        lo = jnp.maximum(row, -base)
        hi = jnp.minimum(row + 2 * WINDOW, seq - 1 - base)
        bias = jnp.where((col >= lo) & (col <= hi), 0.0, NEG)
        for kv in range(N_KV_HEADS):
            kc = slice(kv * HEAD_DIM, (kv + 1) * HEAD_DIM)
            q4 = jnp.concatenate(
                [q_ref[pl.ds(r0, BLOCK), (kv * Q_PER_KV + g) * HEAD_DIM:(kv * Q_PER_KV + g + 1) * HEAD_DIM]
                 for g in range(Q_PER_KV)], axis=0)
            kcat = kbuf[pl.ds(r0, 3 * BLOCK), kc]
            vcat = vbuf[pl.ds(r0, 3 * BLOCK), kc]
            s4 = lax.dot_general(q4, kcat, (((1,), (1,)), ((), ())), preferred_element_type=F32)
            ps, inv = [], []
            for g in range(Q_PER_KV):
                s = s4[g * BLOCK:(g + 1) * BLOCK] + bias
                sink = sink_ref[0, kv * Q_PER_KV + g] * (1.0 / scale)
                m = jnp.maximum(jnp.max(s, axis=-1, keepdims=True), sink)
                p = jnp.exp2((s - m) * c)
                inv.append(1.0 / (jnp.sum(p, axis=-1, keepdims=True) + jnp.exp2((sink - m) * c)))
                ps.append(p.astype(BF16))
            o = jnp.dot(jnp.concatenate(ps, axis=0), vcat, preferred_element_type=F32)
            for g in range(Q_PER_KV):
                hc = (kv * Q_PER_KV + g) * HEAD_DIM
                o_ref[pl.ds(r0, BLOCK), hc:hc + HEAD_DIM] = (o[g * BLOCK:(g + 1) * BLOCK] * inv[g]).astype(BF16)
        return carry

    lax.fori_loop(0, tq // BLOCK, q_block, 0)


def _band_attn(u, sink, segs, tq):
    T = u.shape[0]
    nb = tq // BLOCK
    last_blk = T // BLOCK - 1
    kcol = COL_K // KV_WIDTH
    vcol = COL_V // KV_WIDTH
    prev_map = lambda c: (lambda t: (jnp.maximum(t * nb - 1, 0), c))
    next_map = lambda c: (lambda t: (jnp.minimum((t + 1) * nb, last_blk), c))
    main_map = lambda c: (lambda t: (t, c))
    kern = functools.partial(_band_attn_kernel, segs=segs, tq=tq)
    return pl.pallas_call(
        kern,
        grid=(T // tq,),
        in_specs=[
            pl.BlockSpec(memory_space=pltpu.SMEM),
            pl.BlockSpec((tq, ATTN_WIDTH), lambda t: (t, COL_Q // ATTN_WIDTH)),
            pl.BlockSpec((BLOCK, KV_WIDTH), prev_map(kcol)),
            pl.BlockSpec((tq, KV_WIDTH), main_map(kcol)),
            pl.BlockSpec((BLOCK, KV_WIDTH), next_map(kcol)),
            pl.BlockSpec((BLOCK, KV_WIDTH), prev_map(vcol)),
            pl.BlockSpec((tq, KV_WIDTH), main_map(vcol)),
            pl.BlockSpec((BLOCK, KV_WIDTH), next_map(vcol)),
        ],
        out_specs=pl.BlockSpec((tq, ATTN_WIDTH), lambda t: (t, 0)),
        out_shape=jax.ShapeDtypeStruct((T, ATTN_WIDTH), BF16),
        scratch_shapes=[pltpu.VMEM((tq + 2 * BLOCK, KV_WIDTH), BF16),
                        pltpu.VMEM((tq + 2 * BLOCK, KV_WIDTH), BF16)],
        compiler_params=_params("band_attn"),
        name="band_attn",
    )(sink, u, u, u, u, u, u, u)


def _mix_out_kernel(x1_ref, x2_ref, xpp_ref, xpm_ref, xpn_ref, at_ref, gp_ref, ga_ref,
                    pw_ref, ps_ref, pp_ref, ap_ref, wo_ref, g2_ref, rw_ref, rb_ref,
                    h_ref, hn_ref, ids_ref, wts_ref, xb, pin, *, segs, tm):
    t = pl.program_id(0)
    seq, pos0 = _tile_seq_pos(t, tm, segs)
    in_first = t < segs[0][0] // tm

    xb[0:POOL_HALO, :] = jnp.where(pos0 > 0, xpp_ref[...].astype(F32), 0.0)
    xb[POOL_HALO:POOL_HALO + tm, :] = xpm_ref[...].astype(F32)
    xb[POOL_HALO + tm:, :] = jnp.where(pos0 + tm < seq, xpn_ref[...].astype(F32), 0.0)
    rc = min(MIX_ROW_CHUNK, tm)
    for c in range(tm // rc):
        r0 = c * rc
        rows = slice(r0, r0 + rc)
        x = jnp.where(in_first, x1_ref[rows, :], x2_ref[rows, :])
        _mix_out_rows(r0, rc, pos0, seq, x, at_ref, gp_ref, ga_ref, pw_ref, ps_ref, pp_ref,
                      ap_ref, wo_ref, g2_ref, rw_ref, rb_ref, h_ref, hn_ref, ids_ref, wts_ref, xb, pin)


def _mix_out_rows(r0, rc, pos0, seq, x, at_ref, gp_ref, ga_ref, pw_ref, ps_ref, pp_ref, ap_ref, wo_ref,
                  g2_ref, rw_ref, rb_ref, h_ref, hn_ref, ids_ref, wts_ref, xb, pin):
    rows = slice(r0, r0 + rc)
    tpos = pos0 + r0 + lax.broadcasted_iota(jnp.int32, (rc, 1), 0)
    for g, half in enumerate(POOL_HALF):
        cols = slice(g * POOL_GC, (g + 1) * POOL_GC)
        c0 = POOL_HALO + r0
        win = xb[c0 - half:c0 - half + rc, cols]
        for d in range(-half + 1, half):
            win = win + xb[c0 + d:c0 + d + rc, cols]
        cnt = (jnp.minimum(tpos + half, seq) - jnp.maximum(tpos - half, 0)).astype(F32)
        mixed = (win / cnt - xb[c0:c0 + rc, cols]).astype(BF16)
        yg = jnp.dot(mixed, pw_ref[g], preferred_element_type=F32)
        pin[rows, cols] = (yg * ps_ref[:, cols]).astype(BF16)
    pool_out = jnp.dot(pin[rows, :], pp_ref[...], preferred_element_type=F32)

    attn_out = jnp.dot(at_ref[rows, :], ap_ref[...], preferred_element_type=F32)
    merged = (jax.nn.sigmoid(gp_ref[rows, :].astype(F32)) * pool_out
              + jax.nn.sigmoid(ga_ref[rows, :].astype(F32)) * attn_out)
    h = x + jnp.dot(merged.astype(BF16), wo_ref[...], preferred_element_type=F32)
    h_ref[rows, :] = h

    ms = jnp.mean(h * h, axis=-1, keepdims=True)
    hn = h * lax.rsqrt(ms + EPS) * g2_ref[...]
    _store_slabs(hn_ref, r0 * SLAB_PITCH, rc, hn)

    logit = jnp.dot(hn.astype(BF16), rw_ref[...], preferred_element_type=F32) + rb_ref[...]

    lane_i = lax.broadcasted_iota(jnp.int32, logit.shape, 1)
    lane = lane_i.astype(F32)
    big = float(LANES)
    is_grp = (lane_i >= N_EXPERTS) & (lane_i < N_EXPERTS + N_GROUPS)
    mg = jnp.max(jnp.where(is_grp, logit, -jnp.inf), axis=-1, keepdims=True)
    grp = jnp.min(jnp.where(is_grp & (logit == mg), lane - N_EXPERTS, big), axis=-1, keepdims=True)
    p_grp = 1.0 / jnp.sum(jnp.where(is_grp, jnp.exp(logit - mg), 0.0), axis=-1, keepdims=True)

    grp_of_lane = (lane_i // EXPERTS_PER_GROUP).astype(F32)
    in_grp = (lane_i < N_EXPERTS) & (grp_of_lane == grp)
    m1 = jnp.max(jnp.where(in_grp, logit, -jnp.inf), axis=-1, keepdims=True)
    i1 = jnp.min(jnp.where(in_grp & (logit == m1), lane, big), axis=-1, keepdims=True)
    rest = in_grp & (lane != i1)
    m2 = jnp.max(jnp.where(rest, logit, -jnp.inf), axis=-1, keepdims=True)
    i2 = jnp.min(jnp.where(rest & (logit == m2), lane, big), axis=-1, keepdims=True)
    e2 = jnp.exp(m2 - m1)
    den = 1.0 + e2
    ids_ref[rows, :] = jnp.where(lane_i == 0, i1, jnp.where(lane_i == 1, i2, 0.0)).astype(jnp.int32)
    wts_ref[rows, :] = jnp.where(lane_i == 0, p_grp * (1.0 / den),
                                 jnp.where(lane_i == 1, p_grp * (e2 / den), 0.0))


def _mix_out(x1, x2, u, attn, pool_w, pool_scale, pool_proj, attn_proj, w_out, g2, rw, rb, segs, tm):
    T = u.shape[0]
    n1 = segs[0][0] // tm
    nh = tm // POOL_HALO
    last_halo = T // POOL_HALO - 1
    pcol = COL_P // POOL_WIDTH
    kern = functools.partial(_mix_out_kernel, segs=segs, tm=tm)
    row_spec = lambda w: pl.BlockSpec((tm, w), lambda t: (t, 0))
    return pl.pallas_call(
        kern,
        grid=(T // tm,),
        in_specs=_split_specs(tm, D_MODEL, n1) + [
            pl.BlockSpec((POOL_HALO, POOL_WIDTH), lambda t: (jnp.maximum(t * nh - 1, 0), pcol)),
            pl.BlockSpec((tm, POOL_WIDTH), lambda t: (t, pcol)),
            pl.BlockSpec((POOL_HALO, POOL_WIDTH), lambda t: (jnp.minimum((t + 1) * nh, last_halo), pcol)),
            row_spec(ATTN_WIDTH),
            pl.BlockSpec((tm, D_MODEL), lambda t: (t, COL_GP // D_MODEL)),
            pl.BlockSpec((tm, D_MODEL), lambda t: (t, COL_GA // D_MODEL)),
            _resident((POOL_GROUPS, POOL_GC, POOL_GC)),
            _resident((1, POOL_WIDTH)),
            _resident((POOL_WIDTH, D_MODEL)),
            _resident((ATTN_WIDTH, D_MODEL)),
            _resident((D_MODEL, D_MODEL)),
            _resident((1, D_MODEL)),
            _resident((D_MODEL, LANES)),
            _resident((1, LANES)),
        ],
        out_specs=[row_spec(D_MODEL), pl.BlockSpec((tm * SLAB_PITCH, LANES), lambda t: (t, 0)),
                   row_spec(LANES), row_spec(LANES)],
        out_shape=[jax.ShapeDtypeStruct((T, D_MODEL), F32),
                   jax.ShapeDtypeStruct((T * SLAB_PITCH, LANES), U32),
                   jax.ShapeDtypeStruct((T, LANES), jnp.int32),
                   jax.ShapeDtypeStruct((T, LANES), F32)],
        scratch_shapes=[pltpu.VMEM((tm + 2 * POOL_HALO, POOL_WIDTH), F32),
                        pltpu.VMEM((tm, POOL_WIDTH), BF16)],
        compiler_params=_params("mix_out"),
        name="mix_out",
    )(x1, x2, u, u, u, attn, u, u, pool_w, pool_scale, pool_proj, attn_proj, w_out, g2, rw, rb)


def _combine_kernel(h_ref, wts_ref, o0_ref, o1_ref, y1_ref, y2_ref, *, tk, n1):
    t = pl.program_id(0)
    w = wts_ref[...]
    w0 = w[:, 0:1]
    w1 = w[:, 1:2]
    o0 = _load_slabs(o0_ref, 0, tk)
    o1 = _load_slabs(o1_ref, 0, tk)

    def write(y_ref):
        for j in range(COL_PIECES):
            cols = slice(j * LANES, (j + 1) * LANES)
            y_ref[:, cols] = h_ref[:, cols] + (w0 * o0[j] + w1 * o1[j])

    pl.when(t < n1)(lambda: write(y1_ref))
    pl.when(t >= n1)(lambda: write(y2_ref))


def _combine(h, wts, o_slabs, segs, tk):
    T = h.shape[0]
    nt = T // tk
    n1 = segs[0][0] // tk
    return pl.pallas_call(
        functools.partial(_combine_kernel, tk=tk, n1=n1),
        grid=(nt,),
        in_specs=[
            pl.BlockSpec((tk, D_MODEL), lambda t: (t, 0)),
            pl.BlockSpec((tk, LANES), lambda t: (t, 0)),
            pl.BlockSpec((tk * SLAB_PITCH, LANES), lambda t: (t, 0)),
            pl.BlockSpec((tk * SLAB_PITCH, LANES), lambda t: (nt + t, 0)),
        ],
        out_specs=_split_specs(tk, D_MODEL, n1),
        out_shape=[jax.ShapeDtypeStruct((segs[0][0], D_MODEL), F32),
                   jax.ShapeDtypeStruct((segs[1][0], D_MODEL), F32)],
        compiler_params=_params("combine"),
        name="combine",
    )(h, wts, o_slabs, o_slabs)


def _moe_ffn_kernel(be_ref, nu_ref, nx_ref, g0_ref, gnext_ref, sprev_ref, scur_ref, hn_ref,
                    wg_hbm, wu_hbm, wd_hbm, o_ref, xbuf, obuf, wg_f32, wu_f32, wd_f32, wg_ref, wu_ref, wd_ref,
                    sem_g, sem_s, sem_w, *, tb, dummy0):
    b = pl.program_id(0)
    n_used = nu_ref[0]
    expert = be_ref[b]

    def weight_copies(e):
        return [pltpu.make_async_copy(src.at[e], dst, sem_w)
                for src, dst in ((wg_hbm, wg_f32), (wu_hbm, wu_f32), (wd_hbm, wd_f32))]

    @pl.when(b == 0)
    def _():
        for cp in weight_copies(expert):
            cp.start()

    @pl.when((b < n_used) & ((b == 0) | (expert != be_ref[jnp.maximum(b - 1, 0)])))
    def _():
        for cp in weight_copies(expert):
            cp.wait()
        for f32_ref, bf16_ref in ((wg_f32, wg_ref), (wu_f32, wu_ref), (wd_f32, wd_ref)):
            n_chunks = f32_ref.shape[0] // WEIGHT_CAST_ROWS

            def cast_chunk(ci, carry, f32_ref=f32_ref, bf16_ref=bf16_ref):
                rows = pl.ds(pl.multiple_of(ci * WEIGHT_CAST_ROWS, WEIGHT_CAST_ROWS), WEIGHT_CAST_ROWS)
                bf16_ref[rows, :] = f32_ref[rows, :].astype(BF16)
                return carry

            lax.fori_loop(0, n_chunks, cast_chunk, 0)
        nxt = nx_ref[b]

        @pl.when(nxt >= 0)
        def _():
            for cp in weight_copies(nxt):
                cp.start()
    slot = b % 2
    half = tb * SLAB_PITCH
    mine = pl.multiple_of(slot * half, 8)
    other = pl.multiple_of((1 - slot) * half, 8)

    def slab_copy(src, src_row, dst, dst_row, sem):
        return pltpu.make_async_copy(src.at[pl.ds(src_row, SLAB_PITCH)], dst.at[pl.ds(dst_row, SLAB_PITCH)], sem)

    def half_copy(src, dst, dst_row, sem):
        return pltpu.make_async_copy(src.at[pl.ds(0, half)], dst.at[pl.ds(dst_row, half)], sem)

    def start_gathers(tok_ref, base, sem):
        for r in range(tb):
            slab_copy(hn_ref, tok_ref[0, 0, r] * SLAB_PITCH, xbuf, base + r * SLAB_PITCH, sem).start(priority=0)

    def start_scatters(dst_of_row, base, sem):
        for r in range(tb):
            slab_copy(obuf, base + r * SLAB_PITCH, o_ref, dst_of_row(r) * SLAB_PITCH, sem).start(priority=1)

    @pl.when(b == 0)
    def _():
        obuf[...] = jnp.zeros_like(obuf)
        start_gathers(g0_ref, 0, sem_g.at[0])
        half_copy(obuf, o_ref, dummy0 * SLAB_PITCH, sem_s.at[0]).start()

    @pl.when(b < n_used)
    def _():
        half_copy(hn_ref, xbuf, mine, sem_g.at[slot]).wait()
        half_copy(obuf, o_ref, 0, sem_s.at[slot]).wait()
        start_gathers(gnext_ref, other, sem_g.at[1 - slot])
        first = b == 0
        start_scatters(lambda r: jnp.where(first, dummy0 + tb + r, sprev_ref[0, 0, r]), other, sem_s.at[1 - slot])

        x = jnp.concatenate([p.astype(BF16) for p in _load_slabs(xbuf, mine, tb)], axis=1)
        gate = jnp.dot(x, wg_ref[...], preferred_element_type=F32)
        up = jnp.dot(x, wu_ref[...], preferred_element_type=F32)
        mid = (jax.nn.silu(gate) * up).astype(BF16)
        out = jnp.dot(mid, wd_ref[...], preferred_element_type=F32)
        _store_slabs(obuf, mine, tb, out)

        @pl.when(b == n_used - 1)
        def _():
            half_copy(hn_ref, xbuf, other, sem_g.at[1 - slot]).wait()
            start_scatters(lambda r: scur_ref[0, 0, r], mine, sem_s.at[slot])
            half_copy(obuf, o_ref, 0, sem_s.at[1 - slot]).wait()
            half_copy(obuf, o_ref, 0, sem_s.at[slot]).wait()


def _moe_ffn(hn_slabs, plan, w_gate, w_up, w_down, n_tokens, tb):
    nblk = plan["gather_tok"].shape[0]
    dummy0 = TOP_K * n_tokens
    n_slabs = dummy0 + 2 * tb
    idx_spec = lambda f: pl.BlockSpec((1, 1, tb), lambda i, *_: (f(i), 0, 0), memory_space=pltpu.SMEM)
    hbm = pl.BlockSpec(memory_space=pl.ANY)
    grid_spec = pltpu.PrefetchScalarGridSpec(
        num_scalar_prefetch=3,
        grid=(nblk,),
        in_specs=[
            idx_spec(lambda i: 0),
            idx_spec(lambda i: jnp.minimum(i + 1, nblk - 1)),
            idx_spec(lambda i: jnp.maximum(i - 1, 0)),
            idx_spec(lambda i: i),
            hbm, hbm, hbm, hbm,
        ],
        out_specs=hbm,
        scratch_shapes=[pltpu.VMEM((2 * tb * SLAB_PITCH, LANES), U32),
                        pltpu.VMEM((2 * tb * SLAB_PITCH, LANES), U32),
                        pltpu.VMEM((D_MODEL, D_FF), F32),
                        pltpu.VMEM((D_MODEL, D_FF), F32),
                        pltpu.VMEM((D_FF, D_MODEL), F32),
                        pltpu.VMEM((D_MODEL, D_FF), BF16),
                        pltpu.VMEM((D_MODEL, D_FF), BF16),
                        pltpu.VMEM((D_FF, D_MODEL), BF16),
                        pltpu.SemaphoreType.DMA((2,)),
                        pltpu.SemaphoreType.DMA((2,)),
                        pltpu.SemaphoreType.DMA(())],
    )
    return pl.pallas_call(
        functools.partial(_moe_ffn_kernel, tb=tb, dummy0=dummy0),
        grid_spec=grid_spec,
        out_shape=jax.ShapeDtypeStruct((n_slabs * SLAB_PITCH, LANES), U32),
        compiler_params=_params("moe_ffn"),
        name="moe_ffn",
    )(plan["blk_exp"], plan["n_used"], plan["next_exp"], plan["gather_tok"], plan["gather_tok"], plan["scatter_dst"],
      plan["scatter_dst"], hn_slabs, w_gate, w_up, w_down)


def _slot_plan(ids, tb):
    T = ids.shape[0]
    M = T * TOP_K
    i32 = jnp.int32
    e = ids[:, :TOP_K].reshape(M)
    skey = jnp.sort(e * M + jnp.arange(M, dtype=i32))
    sm = skey % M
    edges = jnp.arange(N_EXPERTS + 1, dtype=i32) * M
    bounds = jnp.sum((skey[None, :] < edges[:, None]).astype(i32), axis=1)
    start = bounds[:-1]
    cnt = bounds[1:] - start
    nb = (cnt + tb - 1) // tb
    blk_end = jnp.cumsum(nb)
    n_used = blk_end[-1]
    nblk = M // tb + N_EXPERTS
    b = jnp.arange(nblk, dtype=i32)
    be = jnp.minimum(jnp.sum((blk_end[None, :] <= b[:, None]).astype(i32), axis=1), N_EXPERTS - 1)
    j = b - (blk_end[be] - nb[be])
    row0 = start[be] + j * tb
    n_valid = jnp.where(b < n_used, jnp.clip(cnt[be] - j * tb, 0, tb), 0)
    r = jnp.arange(tb, dtype=i32)[None, :]
    valid = r < n_valid[:, None]
    m = sm[jnp.minimum(row0[:, None] + r, M - 1)]
    tok = m // TOP_K
    k = m % TOP_K
    spare = TOP_K * T + (b % 2)[:, None] * tb + r
    blk_exp = jnp.where(b < n_used, be, be[jnp.maximum(n_used - 1, 0)])
    ex = jnp.arange(N_EXPERTS, dtype=i32)
    later = (ex[None, :] > ex[:, None]) & (cnt[None, :] > 0)
    next_of = jnp.min(jnp.where(later, ex[None, :], N_EXPERTS), axis=1)
    next_of = jnp.where(next_of < N_EXPERTS, next_of, -1)
    return dict(blk_exp=blk_exp.astype(i32), n_used=n_used.astype(i32).reshape(1),
                next_exp=next_of[blk_exp].astype(i32),
                gather_tok=jnp.where(valid, tok, 0).astype(i32).reshape(nblk, 1, tb),
                scatter_dst=jnp.where(valid, k * T + tok, spare).astype(i32).reshape(nblk, 1, tb))


def _rope_tables(seq):
    inv = ROPE_THETA ** (-jnp.arange(ROT_HALF, dtype=F32) / ROT_HALF)
    assert seq % ROPE_SPLIT == 0
    coarse = (jnp.arange(seq // ROPE_SPLIT, dtype=jnp.int32) * ROPE_SPLIT).astype(F32)[:, None] * inv[None, :]
    fine = jnp.arange(ROPE_SPLIT, dtype=jnp.int32).astype(F32)[:, None] * inv[None, :]
    cc, sc = jnp.cos(coarse)[:, None, :], jnp.sin(coarse)[:, None, :]
    cf, sf = jnp.cos(fine)[None, :, :], jnp.sin(fine)[None, :, :]
    cos = (cc * cf - sc * sf).reshape(seq, ROT_HALF)
    sin = (sc * cf + cc * sf).reshape(seq, ROT_HALF)
    ones = jnp.ones((seq, HEAD_DIM - ROT_DIM), F32)
    cos_t = jnp.concatenate([cos, cos, ones], axis=1)
    sin_t = jnp.concatenate([-sin, sin, 0.0 * ones], axis=1)
    return cos_t, sin_t


def kernel(x_prompt, x_sample, norm1_g, w_in, pool_w, pool_scale, pool_proj, q_norm_g, k_norm_g, sink,
           attn_proj, w_out, norm2_g, router_group_w, router_group_b, router_expert_w, router_expert_b,
           w_gate, w_up, w_down):
    assert norm1_g.shape[0] == 1, "single-layer trunk"
    (b1, s1, d), (b2, s2, _) = x_prompt.shape, x_sample.shape
    t1, t2 = b1 * s1, b2 * s2
    segs = ((t1, s1), (t2, s2))
    s_min = min(s1, s2)
    x1 = x_prompt.reshape(t1, d)
    x2 = x_sample.reshape(t2, d)

    gain_pad = jnp.ones((-QK_WIDTH % IN_TN,), F32)
    head_gain = jnp.concatenate([jnp.tile(q_norm_g[0], N_HEADS), jnp.tile(k_norm_g[0], N_KV_HEADS),
                                 gain_pad])[None, :]
    pad = LANES - N_EXPERTS - N_GROUPS
    rw = jnp.concatenate([router_expert_w[0], router_group_w[0], jnp.zeros((D_MODEL, pad), F32)], axis=1)
    rb = jnp.concatenate([router_expert_b[0], router_group_b[0], jnp.zeros((pad,), F32)])[None, :]
    cos_t, sin_t = _rope_tables(max(s1, s2))

    tile = {name: min(rows, s_min) for name, rows in TOKEN_TILE.items()}
    assert all(s1 % t == 0 and s2 % t == 0 for t in tile.values()), "token tiles must divide both sequences"

    u = _in_proj(x1, x2, norm1_g[0][None, :], w_in[0].astype(BF16), head_gain, cos_t, sin_t, segs,
                 tm=tile["in_proj"])
    attn = _band_attn(u, sink[0][None, :], segs, tq=tile["band_attn"])
    h, hn, ids, wts = _mix_out(x1, x2, u, attn, pool_w[0].astype(BF16), pool_scale[0][None, :],
                               pool_proj[0].astype(BF16), attn_proj[0].astype(BF16), w_out[0].astype(BF16),
                               norm2_g[0][None, :], rw.astype(BF16), rb, segs, tm=tile["mix_out"])
    tb = TOKEN_TILE["moe_ffn"]
    plan = _slot_plan(ids, tb)
    o_slabs = _moe_ffn(hn, plan, w_gate[0], w_up[0], w_down[0], t1 + t2, tb)
    y1, y2 = _combine(h, wts, o_slabs, segs, tk=tile["combine"])
    return (y1.reshape(b1, s1, d), y2.reshape(b2, s2, d))
```
